```python
import math
import jax
import jax.numpy as jnp
from jax import lax
import numpy as np

D_MODEL = 2048
BATCH = 2
SEQ = 4096
DEPTH = 2
DEC_BATCH = 32
DEC_SEQ = 8
PAST_LEN = 8192
PAGE_SIZE = 128

NSA_HEADS = 8
NSA_KV_HEADS = 2
NSA_GROUP = NSA_HEADS // NSA_KV_HEADS
HEAD_DIM = 128
NSA_WIDTH = NSA_HEADS * HEAD_DIM
CMP_BLOCK = 64
N_SELECT = 16
WINDOW = 512
Q_BLOCK = 128
FORCE_SCORE = 1.0e4
MASK_VALUE = -1.0e30
N_KV_ROWS = 4
CONV_WIDTH = 512
CONV_K = 3
HG_HEADS = 4
HG_DK = 128
HG_DV = 128
HG_CHUNK = 64
REL_BUCKETS = 32
REL_MAX_DIST = 128
D_FF = -((-8 * D_MODEL) // (3 * 256)) * 256
PLE_DIM = 256
NORM_EPS = 1e-6
IN_SIZES = (NSA_WIDTH, 6 * NSA_KV_HEADS * HEAD_DIM, 3 * NSA_HEADS, 3 * CONV_WIDTH,
            2 * HG_HEADS * HG_DK + 2 * HG_HEADS * HG_DV, 3 * D_MODEL)
N_IN = sum(IN_SIZES)

kernel_name = 'hybrid_nsa_conv_hgrn2_decode_step'


def rmsnorm(x, g):
    xf = x.astype(jnp.float32)
    y = xf * lax.rsqrt(jnp.mean(xf * xf, axis=-1, keepdims=True) + NORM_EPS)
    return (y * g.astype(jnp.float32)).astype(x.dtype)


def split_cols(z, sizes):
    offs, acc = [], 0
    for s in sizes[:-1]:
        acc += s
        offs.append(acc)
    return jnp.split(z, offs, axis=-1)


def t5_bucket(dist):
    n = jnp.maximum(dist, 0)
    max_exact = REL_BUCKETS // 2
    nf = jnp.maximum(n, 1).astype(jnp.float32)
    large = max_exact + (jnp.log(nf / max_exact) / math.log(REL_MAX_DIST / max_exact)
                         * (REL_BUCKETS - max_exact)).astype(jnp.int32)
    large = jnp.minimum(large, REL_BUCKETS - 1)
    return jnp.where(n < max_exact, n, large)


def masked_softmax(s, mask):
    s = jnp.where(mask, s.astype(jnp.float32), MASK_VALUE)
    m = jnp.max(s, axis=-1, keepdims=True)
    e = jnp.where(mask, jnp.exp(s - m), 0.0)
    return e / jnp.maximum(e.sum(-1, keepdims=True), 1e-30)


def compress(kb, pe, w1, w2):
    x = kb + pe[None, None, :, None, :]
    b, nc = x.shape[:2]
    x = jnp.moveaxis(x, 3, 2).reshape(b, nc, NSA_KV_HEADS, CMP_BLOCK * HEAD_DIM)
    return jax.nn.silu(x @ w1) @ w2


def nsa_prepare(kv, pe, w1, w2):
    b, L = kv.shape[:2]
    nc = L // CMP_BLOCK
    cb = kv[:, :nc * CMP_BLOCK].reshape(b, nc, CMP_BLOCK, N_KV_ROWS, NSA_KV_HEADS, HEAD_DIM)
    kc = compress(cb[:, :, :, 0], pe[0], w1[0], w2[0])
    vc = compress(cb[:, :, :, 1], pe[1], w1[1], w2[1])
    c_end = (jnp.arange(nc) + 1) * CMP_BLOCK - 1
    ns = -(-L // CMP_BLOCK)
    sb = jnp.pad(kv[:, :, 2:4], ((0, 0), (0, ns * CMP_BLOCK - L), (0, 0), (0, 0), (0, 0)))
    sb = sb.reshape(b, ns, CMP_BLOCK, 2, NSA_KV_HEADS, HEAD_DIM)
    return kc, vc, c_end, sb[:, :, :, 0], sb[:, :, :, 1]


def nsa_core(q, gates, q_pos, kc, vc, c_end, ks_blk, vs_blk, kw, vw, kw_pos, rel_table):
    b, tq = q.shape[:2]
    bias_c = rel_table[t5_bucket(q_pos[:, None] - c_end[None, :])]
    bias_c = bias_c.reshape(tq, -1, NSA_KV_HEADS, NSA_GROUP).transpose(0, 2, 3, 1)
    s_c = jnp.einsum('btkgd,bckd->btkgc', q, kc) + bias_c
    mask_c = (c_end[None, :] <= q_pos[:, None])[:, None, None, :]
    p_c = masked_softmax(s_c, mask_c)
    o_c = jnp.einsum('btkgc,bckd->btkgd', p_c.astype(vc.dtype), vc)
    ns, nc = ks_blk.shape[1], kc.shape[1]
    imp = jnp.pad(p_c.sum(3), ((0, 0), (0, 0), (0, 0), (0, ns - nc)))
    blk = jnp.arange(ns)[None, :]
    cur = (q_pos // CMP_BLOCK)[:, None]
    eligible = (blk * CMP_BLOCK <= q_pos[:, None])[None, :, None, :]
    forced = ((blk == 0) | (blk == cur) | (blk == cur - 1))[None, :, None, :]
    score = jnp.where(eligible, imp + jnp.where(forced, FORCE_SCORE, 0.0), -1.0)
    n_sel = min(N_SELECT, ns)
    top_s, idx = lax.top_k(score, n_sel)
    sel_ok = top_s >= 0.0
    gather = jax.vmap(jax.vmap(lambda blocks, ix: blocks[ix]))
    idx_t = jnp.moveaxis(idx, 2, 1)
    k_sel = gather(jnp.moveaxis(ks_blk, 3, 1), idx_t)
    v_sel = gather(jnp.moveaxis(vs_blk, 3, 1), idx_t)
    kpos = idx[..., None] * CMP_BLOCK + jnp.arange(CMP_BLOCK)
    dist = q_pos[None, :, None, None, None] - kpos
    mask_s = ((dist >= 0) & sel_ok[..., None]).reshape(b, tq, NSA_KV_HEADS, 1, n_sel * CMP_BLOCK)
    table3 = rel_table.reshape(REL_BUCKETS, NSA_KV_HEADS, NSA_GROUP)
    bias_s = table3[t5_bucket(dist), jnp.arange(NSA_KV_HEADS)[None, None, :, None, None]]
    s_s = jnp.einsum('btkgd,bktnsd->btkgns', q, k_sel) + jnp.moveaxis(bias_s, -1, 3)
    p_s = masked_softmax(s_s.reshape(b, tq, NSA_KV_HEADS, NSA_GROUP, n_sel * CMP_BLOCK), mask_s)
    p_s = p_s.reshape(b, tq, NSA_KV_HEADS, NSA_GROUP, n_sel, CMP_BLOCK)
    o_s = jnp.einsum('btkgns,bktnsd->btkgd', p_s.astype(v_sel.dtype), v_sel)
    dist_w = q_pos[:, None] - kw_pos[None, :]
    mask_w = ((dist_w >= 0) & (dist_w <= WINDOW) & (kw_pos[None, :] >= 0))[:, None, None, :]
    bias_w = rel_table[t5_bucket(dist_w)].reshape(tq, -1, NSA_KV_HEADS, NSA_GROUP).transpose(0, 2, 3, 1)
    s_w = jnp.einsum('btkgd,bskd->btkgs', q, kw) + bias_w
    p_w = masked_softmax(s_w, mask_w)
    o_w = jnp.einsum('btkgs,bskd->btkgd', p_w.astype(vw.dtype), vw)
    return gates[..., 0:1] * o_c + gates[..., 1:2] * o_s + gates[..., 2:3] * o_w


def nsa_prompt(q, gates, kv, kvw, pe, w1, w2, rel_table):
    b, t = q.shape[:2]
    kc, vc, c_end, ks_blk, vs_blk = nsa_prepare(kv, pe, w1, w2)
    kw_pad = jnp.pad(kvw, ((0, 0), (WINDOW, 0), (0, 0), (0, 0), (0, 0)))
    nqb = t // Q_BLOCK
    qb = jnp.moveaxis(q.reshape(b, nqb, Q_BLOCK, NSA_KV_HEADS, NSA_GROUP, HEAD_DIM), 1, 0)
    gb = jnp.moveaxis(gates.reshape(b, nqb, Q_BLOCK, NSA_KV_HEADS, NSA_GROUP, 3), 1, 0)

    def one_block(args):
        q_i, g_i, i = args
        start = i * Q_BLOCK
        q_pos = start + jnp.arange(Q_BLOCK)
        w = lax.dynamic_slice_in_dim(kw_pad, start, WINDOW + Q_BLOCK, axis=1)
        kw_pos = start - WINDOW + jnp.arange(WINDOW + Q_BLOCK)
        return nsa_core(q_i, g_i, q_pos, kc, vc, c_end, ks_blk, vs_blk,
                        w[:, :, 0], w[:, :, 1], kw_pos, rel_table)

    out = lax.map(one_block, (qb, gb, jnp.arange(nqb)))
    out = jnp.moveaxis(out, 0, 1).reshape(b, t, NSA_WIDTH)
    return out, kvw[:, t - min(WINDOW, t):]


def nsa_sample(q, gates, kv_new, kvw_new, kv_past, win_buf, pe, w1, w2, rel_table):
    past, tn = kv_past.shape[1], q.shape[1]
    wb = win_buf.shape[1]
    kv_all = jnp.concatenate([kv_past, kv_new.astype(kv_past.dtype)], axis=1)
    kc, vc, c_end, ks_blk, vs_blk = nsa_prepare(kv_all, pe, w1, w2)
    kw = jnp.concatenate([win_buf, kvw_new.astype(win_buf.dtype)], axis=1)
    kw_pos = past - wb + jnp.arange(wb + tn)
    q_pos = past + jnp.arange(tn)
    out = nsa_core(q, gates, q_pos, kc, vc, c_end, ks_blk, vs_blk,
                   kw[:, :, 0], kw[:, :, 1], kw_pos, rel_table)
    keep = min(WINDOW, past + tn)
    return out.reshape(q.shape[0], tn, NSA_WIDTH), kw[:, wb + tn - keep:]


def short_conv(b_gate, c_gate, hin, w, prev):
    u = c_gate * hin
    ue = jnp.concatenate([prev.astype(u.dtype), u], axis=1)
    t = u.shape[1]
    y = w[0] * ue[:, 0:t]
    for j in range(1, CONV_K):
        y = y + w[j] * ue[:, j:j + t]
    return b_gate * y, ue[:, t:]


def hgrn2(q, fx, v, gx, lb, s0, norm_g):
    b, t = q.shape[:2]
    f32 = jnp.float32
    qf = jax.nn.silu(q.astype(f32)).reshape(b, t, HG_HEADS, HG_DK)
    fx = fx.astype(f32).reshape(b, t, HG_HEADS, HG_DK)
    vf = v.astype(f32).reshape(b, t, HG_HEADS, HG_DV)
    lb_pos = lb > 0.0
    log_lb = jnp.log(jnp.where(lb_pos, lb, 1.0))
    log_sig = jax.nn.log_sigmoid(fx)
    logf = jnp.where(lb_pos, jnp.logaddexp(log_lb, jnp.log1p(-lb) + log_sig), log_sig)
    k = (1.0 - lb) * jax.nn.sigmoid(-fx)
    chunk = min(HG_CHUNK, t)
    n_ch = -(-t // chunk)
    pad = n_ch * chunk - t

    def to_chunks(a):
        a = jnp.pad(a, ((0, 0), (0, pad), (0, 0), (0, 0)))
        return a.reshape(b, n_ch, chunk, HG_HEADS, a.shape[-1]).transpose(1, 0, 3, 2, 4)

    causal = jnp.tril(jnp.ones((chunk, chunk), dtype=bool))[:, :, None]

    def step(s, xs):
        qc, kc, vc, lc = xs
        g = jnp.cumsum(lc, axis=2)
        o = jnp.einsum('bhtd,bhde->bhte', qc * jnp.exp(g), s)
        diff = g[:, :, :, None, :] - g[:, :, None, :, :]
        decay = jnp.where(causal, jnp.exp(jnp.where(causal, diff, 0.0)), 0.0)
        a = jnp.einsum('bhtd,bhsd,bhtsd->bhts', qc, kc, decay)
        o = o + jnp.einsum('bhts,bhse->bhte', a, vc)
        g_last = g[:, :, -1]
        s = jnp.exp(g_last)[..., None] * s + jnp.einsum(
            'bhsd,bhse->bhde', kc * jnp.exp(g_last[:, :, None] - g), vc)
        return s, o

    s_fin, o = lax.scan(step, s0.astype(f32),
                        (to_chunks(qf), to_chunks(k), to_chunks(vf), to_chunks(logf)))
    o = o.transpose(1, 0, 3, 2, 4).reshape(b, n_ch * chunk, HG_HEADS, HG_DV)[:, :t]
    o = rmsnorm(o, norm_g).reshape(b, t, HG_HEADS * HG_DV)
    y = o * jax.nn.silu(gx.astype(f32))
    return y.astype(q.dtype), s_fin.astype(s0.dtype)


def trunk_layer(h, p_l, lw, lb_l, conv_prev, hg_s0, nsa_fn):
    b, t = h.shape[:2]
    xn = rmsnorm(h, lw['g_mix'])
    z = xn @ lw['w_in']
    zq, zkv, zng, zcv, zhg, zmg = split_cols(z, IN_SIZES)
    q = zq.reshape(b, t, NSA_KV_HEADS, NSA_GROUP, HEAD_DIM) * (HEAD_DIM ** -0.5)
    kv = zkv.reshape(b, t, 6, NSA_KV_HEADS, HEAD_DIM)
    gates = jax.nn.sigmoid(zng.reshape(b, t, NSA_KV_HEADS, NSA_GROUP, 3))
    y_a, win_state = nsa_fn(q, gates, kv[:, :, :N_KV_ROWS], kv[:, :, N_KV_ROWS:])
    cb, cc, ch = jnp.split(zcv, 3, axis=-1)
    y_b, conv_state = short_conv(cb, cc, ch, lw['conv_w'], conv_prev)
    hq, hf, hv, hgt = split_cols(zhg, (HG_HEADS * HG_DK, HG_HEADS * HG_DK, HG_HEADS * HG_DV, HG_HEADS * HG_DV))
    y_c, hg_state = hgrn2(hq, hf, hv, hgt, lb_l, hg_s0, lw['hg_norm'])
    g_a, g_b, g_c = jnp.split(jax.nn.sigmoid(zmg), 3, axis=-1)
    mixed = g_a * (y_a @ lw['w_pa']) + g_b * (y_b @ lw['w_pb']) + g_c * (y_c @ lw['w_pc'])
    h = h + mixed @ lw['w_o']
    xf = rmsnorm(h, lw['g_ffn'])
    h = h + (jax.nn.silu(xf @ lw['w_gate']) * (xf @ lw['w_up'])) @ lw['w_down']
    h = h + jax.nn.sigmoid(h @ lw['w_pg']) * (p_l @ lw['w_pp'])
    return h, (kv[:, :, :N_KV_ROWS], win_state, conv_state, hg_state)


def setup_inputs(seed: int = 0) -> dict:
    key = jax.random.key(seed)
    k = jax.random.split(key, 32)
    f32 = jnp.float32
    n_pages = PAST_LEN // PAGE_SIZE
    n_used = DEC_BATCH * n_pages
    n_pool = n_used + max(1, n_used // 4)
    wb = min(WINDOW, PAST_LEN)

    def nrm(kk, shape, scale=1.0):
        return jax.random.normal(kk, shape, f32) * scale

    def gain(kk, shape):
        return 1.0 + 0.05 * jax.random.normal(kk, shape, f32)

    page_table = jax.random.permutation(k[8], n_pool)[:n_used].reshape(DEC_BATCH, n_pages).astype(jnp.int32)
    return {
        'x_prompt': nrm(k[0], (BATCH, SEQ, D_MODEL)),
        'x_sample': nrm(k[1], (DEC_BATCH, DEC_SEQ, D_MODEL)),
        'p_prompt': nrm(k[2], (DEPTH, BATCH, SEQ, PLE_DIM)),
        'p_sample': nrm(k[3], (DEPTH, DEC_BATCH, DEC_SEQ, PLE_DIM)),
        'cache_kv': nrm(k[4], (DEPTH, n_pool, PAGE_SIZE, N_KV_ROWS, NSA_KV_HEADS, HEAD_DIM)),
        'cache_win': nrm(k[5], (DEPTH, DEC_BATCH, wb, 2, NSA_KV_HEADS, HEAD_DIM)),
        'state_conv': nrm(k[6], (DEPTH, DEC_BATCH, CONV_K - 1, CONV_WIDTH)),
        'state_hgrn': nrm(k[7], (DEPTH, DEC_BATCH, HG_HEADS, HG_DK, HG_DV), 0.3),
        'page_table': page_table,
        'rel_bias': nrm(k[9], (REL_BUCKETS, NSA_HEADS), 0.5),
        'g_mix': gain(k[10], (DEPTH, D_MODEL)),
        'w_in': nrm(k[11], (DEPTH, D_MODEL, N_IN), D_MODEL ** -0.5),
        'phi_pe': nrm(k[12], (DEPTH, 2, CMP_BLOCK, HEAD_DIM), 0.5),
        'phi_w1': nrm(k[13], (DEPTH, 2, CMP_BLOCK * HEAD_DIM, HEAD_DIM), (CMP_BLOCK * HEAD_DIM) ** -0.5),
        'phi_w2': nrm(k[14], (DEPTH, 2, HEAD_DIM, HEAD_DIM), HEAD_DIM ** -0.5),
        'conv_w': nrm(k[15], (DEPTH, CONV_K, CONV_WIDTH), CONV_K ** -0.5),
        'hg_lb_logits': nrm(k[16], (DEPTH, HG_HEADS * HG_DK), 0.5),
        'hg_norm': gain(k[17], (DEPTH, HG_DV)),
        'w_pa': nrm(k[18], (DEPTH, NSA_WIDTH, D_MODEL), NSA_WIDTH ** -0.5),
        'w_pb': nrm(k[19], (DEPTH, CONV_WIDTH, D_MODEL), CONV_WIDTH ** -0.5),
        'w_pc': nrm(k[20], (DEPTH, HG_HEADS * HG_DV, D_MODEL), (HG_HEADS * HG_DV) ** -0.5),
        'w_o': nrm(k[21], (DEPTH, D_MODEL, D_MODEL), D_MODEL ** -0.5),
        'g_ffn': gain(k[22], (DEPTH, D_MODEL)),
        'w_gate': nrm(k[23], (DEPTH, D_MODEL, D_FF), D_MODEL ** -0.5),
        'w_up': nrm(k[24], (DEPTH, D_MODEL, D_FF), D_MODEL ** -0.5),
        'w_down': nrm(k[25], (DEPTH, D_FF, D_MODEL), D_FF ** -0.5),
        'w_pg': nrm(k[26], (DEPTH, D_MODEL, D_MODEL), D_MODEL ** -0.5),
        'w_pp': nrm(k[27], (DEPTH, PLE_DIM, D_MODEL), PLE_DIM ** -0.5),
        'g_final': gain(k[28], (D_MODEL,)),
    }


def reference(x_prompt, x_sample, p_prompt, p_sample, cache_kv, cache_win, state_conv, state_hgrn,
              page_table, rel_bias, g_mix, w_in, phi_pe, phi_w1, phi_w2, conv_w, hg_lb_logits, hg_norm,
              w_pa, w_pb, w_pc, w_o, g_ffn, w_gate, w_up, w_down, w_pg, w_pp, g_final):
    n_dec, n_pages = page_table.shape
    page = cache_kv.shape[2]
    bp = x_prompt.shape[0]
    lb_sm = jax.nn.softmax(hg_lb_logits.astype(jnp.float32), axis=0)
    lb_all = jnp.cumsum(lb_sm, axis=0) - lb_sm[0]
    hp, hs = x_prompt, x_sample
    kv_p, kv_s, win_p, win_s, conv_p, conv_s, hg_p, hg_s = [], [], [], [], [], [], [], []
    for l in range(DEPTH):
        lw = {'g_mix': g_mix[l], 'w_in': w_in[l], 'conv_w': conv_w[l], 'hg_norm': hg_norm[l],
              'w_pa': w_pa[l], 'w_pb': w_pb[l], 'w_pc': w_pc[l], 'w_o': w_o[l], 'g_ffn': g_ffn[l],
              'w_gate': w_gate[l], 'w_up': w_up[l], 'w_down': w_down[l], 'w_pg': w_pg[l], 'w_pp': w_pp[l]}
        lb_l = lb_all[l].reshape(HG_HEADS, HG_DK)
        pe_l, w1_l, w2_l = phi_pe[l], phi_w1[l], phi_w2[l]
        hp, (r_kv, r_win, r_conv, r_hg) = trunk_layer(
            hp, p_prompt[l], lw, lb_l,
            jnp.zeros((bp, CONV_K - 1, CONV_WIDTH), hp.dtype),
            jnp.zeros((bp, HG_HEADS, HG_DK, HG_DV), hp.dtype),
            lambda q, g, kv, kvw: nsa_prompt(q, g, kv, kvw, pe_l, w1_l, w2_l, rel_bias))
        kv_p.append(r_kv); win_p.append(r_win); conv_p.append(r_conv); hg_p.append(r_hg)
        past = cache_kv[l][page_table].reshape(n_dec, n_pages * page, N_KV_ROWS, NSA_KV_HEADS, HEAD_DIM)
        win_l = cache_win[l]
        hs, (s_kv, s_win, s_conv, s_hg) = trunk_layer(
            hs, p_sample[l], lw, lb_l, state_conv[l], state_hgrn[l],
            lambda q, g, kv, kvw: nsa_sample(q, g, kv, kvw, past, win_l, pe_l, w1_l, w2_l, rel_bias))
        kv_s.append(s_kv); win_s.append(s_win); conv_s.append(s_conv); hg_s.append(s_hg)
    y_prompt = rmsnorm(hp, g_final)
    y_sample = rmsnorm(hs, g_final)
    return (y_prompt, y_sample, jnp.stack(kv_p), jnp.stack(kv_s), jnp.stack(win_p), jnp.stack(win_s),
            jnp.stack(conv_p), jnp.stack(conv_s), jnp.stack(hg_p), jnp.stack(hg_s))
```

```python
import functools
import math

import numpy as np
import jax
import jax.numpy as jnp
from jax import lax
from jax.experimental import pallas as pl
from jax.experimental.pallas import tpu as pltpu

F32 = jnp.float32
BF16 = jnp.bfloat16

LANES = 128
SUBLANES = 8
VMEM_LIMIT_BYTES = 56 * 1024 * 1024

NSA_HEADS = 8
NSA_KV_HEADS = 2
NSA_GROUP = NSA_HEADS // NSA_KV_HEADS
HEAD_DIM = 128
NSA_WIDTH = NSA_HEADS * HEAD_DIM
CMP_BLOCK = 64
N_SELECT = 16
WINDOW = 512
Q_BLOCK = 128
FORCE_SCORE = 1.0e4
MASK_VALUE = -1.0e30
N_KV_ROWS = 4
CONV_K = 3
HG_HEADS = 4
HG_DK = 128
HG_DV = 128
HG_BLOCK = SUBLANES
REL_BUCKETS = 32
REL_MAX_DIST = 128
NORM_EPS = 1e-6
N_GATES = 3 * NSA_HEADS
GATE_PAD = 4 * LANES


def _round_up(x, m):
    return -(-x // m) * m


def _pick_tile(n, pref, unit):
    if n <= pref:
        return n
    best = None
    for t in range(unit, pref + 1, unit):
        if n % t == 0:
            best = t
    assert best is not None, (n, pref, unit)
    return best


def _divmod_const(x, n):
    if n & (n - 1) == 0:
        return lax.shift_right_logical(x, n.bit_length() - 1), x & (n - 1)
    return x // n, lax.rem(x, n)


def _t5_thresholds():
    n = np.arange(0, REL_MAX_DIST + 1)
    max_exact = REL_BUCKETS // 2
    nf = np.maximum(n, 1).astype(np.float32)
    ratio = np.log(nf / np.float32(max_exact)) / np.float32(math.log(REL_MAX_DIST / max_exact))
    large = max_exact + (ratio * np.float32(REL_BUCKETS - max_exact)).astype(np.int32)
    large = np.minimum(large, REL_BUCKETS - 1)
    bucket = np.where(n < max_exact, n, large)
    assert np.all(np.diff(bucket) >= 0) and bucket[-1] == REL_BUCKETS - 1
    return [int(np.argmax(bucket >= j)) for j in range(1, REL_BUCKETS)]


T5_THRESHOLDS = _t5_thresholds()


def _t5_bias(dist, tab_ref, heads):
    ge = [dist >= thr for thr in T5_THRESHOLDS]
    out = []
    for h in heads:
        b = jnp.full(dist.shape, tab_ref[0, h], F32)
        for j, m in enumerate(ge):
            b = jnp.where(m, tab_ref[j + 1, h], b)
        out.append(b)
    return out


def _dot(a, b):
    return jnp.dot(a, b, preferred_element_type=F32)


def _dot_nt(a, b):
    return lax.dot_general(a, b, (((1,), (1,)), ((), ())), preferred_element_type=F32)


def _dot_tn(a, b):
    return lax.dot_general(a, b, (((0,), (0,)), ((), ())), preferred_element_type=F32)


def _cparams(sem):
    return pltpu.CompilerParams(dimension_semantics=sem, vmem_limit_bytes=VMEM_LIMIT_BYTES)


def _rmsnorm_kernel(x_ref, g_ref, o_ref):
    x = x_ref[...]
    y = x * lax.rsqrt(jnp.mean(x * x, axis=-1, keepdims=True) + NORM_EPS)
    o_ref[...] = (y * g_ref[...]).astype(o_ref.dtype)


def _rmsnorm(x, g, out_dtype):
    m, d = x.shape
    tm = _pick_tile(m, 1024, SUBLANES)
    return pl.pallas_call(
        _rmsnorm_kernel,
        grid=(m // tm,),
        in_specs=[pl.BlockSpec((tm, d), lambda i: (i, 0)), pl.BlockSpec((1, d), lambda i: (0, 0))],
        out_specs=pl.BlockSpec((tm, d), lambda i: (i, 0)),
        out_shape=jax.ShapeDtypeStruct((m, d), out_dtype),
        compiler_params=_cparams(("parallel",)),
        name="rmsnorm",
    )(x, g.reshape(1, d))


def _mm_kernel(*refs, n_x, pairs, n_extra, epilogue, cast_w):
    xs = refs[:n_x]
    ws = refs[n_x:n_x + len(pairs)]
    extras = refs[n_x + len(pairs):n_x + len(pairs) + n_extra]
    o_ref = refs[n_x + len(pairs) + n_extra]
    wscr = refs[n_x + len(pairs) + n_extra + 1:]
    i = pl.program_id(1)

    if any(cast_w):
        @pl.when(i == 0)
        def _():
            s = 0
            for p in range(len(pairs)):
                if cast_w[p]:
                    wscr[s][...] = ws[p][...].astype(BF16)
                    s += 1

    xv = [x[...].astype(BF16) for x in xs]
    prods = []
    s = 0
    for p, xi in enumerate(pairs):
        if cast_w[p]:
            w = wscr[s][...]
            s += 1
        else:
            w = ws[p][...]
        prods.append(_dot(xv[xi], w))
    o_ref[...] = epilogue(prods, [e[...] for e in extras]).astype(o_ref.dtype)


def _fused_matmul(xs, pairs, ws, extras, epilogue, n_out, out_dtype, tm, tn, name):
    m = xs[0].shape[0]
    assert m % tm == 0 and n_out % tn == 0
    cast_w = tuple(w.dtype != BF16 for w, _ in ws)
    in_specs = []
    for x in xs:
        in_specs.append(pl.BlockSpec((tm, x.shape[1]), lambda j, i: (i, 0)))
    for p, (w, roff) in enumerate(ws):
        k = xs[pairs[p]].shape[1]
        in_specs.append(pl.BlockSpec((k, tn), functools.partial(lambda j, i, r: (r, j), r=roff)))
    for e, coff in extras:
        in_specs.append(pl.BlockSpec((tm, tn), functools.partial(lambda j, i, c: (i, c + j), c=coff)))
    scratch = [pltpu.VMEM((xs[pairs[p]].shape[1], tn), BF16) for p in range(len(ws)) if cast_w[p]]
    kern = functools.partial(_mm_kernel, n_x=len(xs), pairs=tuple(pairs), n_extra=len(extras),
                             epilogue=epilogue, cast_w=cast_w)
    return pl.pallas_call(
        kern,
        grid=(n_out // tn, m // tm),
        in_specs=in_specs,
        out_specs=pl.BlockSpec((tm, tn), lambda j, i: (i, j)),
        out_shape=jax.ShapeDtypeStruct((m, n_out), out_dtype),
        scratch_shapes=scratch,
        compiler_params=_cparams(("arbitrary", "arbitrary")),
        name=name,
    )(*xs, *[w for w, _ in ws], *[e for e, _ in extras])


def _compress_kernel(blk_ref, pe_ref, w1_ref, w2_ref, o_ref, x2d, w1b):
    @pl.when(pl.program_id(1) == 0)
    def _():
        w1b[...] = w1_ref[0].astype(BF16)

    for tau in range(CMP_BLOCK):
        x2d[:, tau * HEAD_DIM:(tau + 1) * HEAD_DIM] = (
            blk_ref[:, tau, :] + pe_ref[0, tau:tau + 1, :]).astype(BF16)
    h1 = _dot(x2d[...], w1b[...])
    h1 = h1 * jax.nn.sigmoid(h1)
    o_ref[...] = _dot(h1.astype(BF16), w2_ref[0].astype(BF16))


def _compress(blocks, row_block0, n_blocks, col_block0, pe, w1, w2, cb, name):
    n_rk = 2 * NSA_KV_HEADS
    return pl.pallas_call(
        _compress_kernel,
        grid=(n_rk, n_blocks // cb),
        in_specs=[
            pl.BlockSpec((cb, CMP_BLOCK, HEAD_DIM), lambda rk, t: (row_block0 + t, 0, col_block0 + rk)),
            pl.BlockSpec((1, CMP_BLOCK, HEAD_DIM), lambda rk, t: (rk // NSA_KV_HEADS, 0, 0)),
            pl.BlockSpec((1, CMP_BLOCK * HEAD_DIM, HEAD_DIM), lambda rk, t: (rk // NSA_KV_HEADS, 0, 0)),
            pl.BlockSpec((1, HEAD_DIM, HEAD_DIM), lambda rk, t: (rk // NSA_KV_HEADS, 0, 0)),
        ],
        out_specs=pl.BlockSpec((cb, HEAD_DIM), lambda rk, t: (t, rk)),
        out_shape=jax.ShapeDtypeStruct((n_blocks, n_rk * HEAD_DIM), F32),
        scratch_shapes=[pltpu.VMEM((cb, CMP_BLOCK * HEAD_DIM), BF16),
                        pltpu.VMEM((CMP_BLOCK * HEAD_DIM, HEAD_DIM), BF16)],
        compiler_params=_cparams(("arbitrary", "arbitrary")),
        name=name,
    )(blocks, pe, w1, w2)


def _masked_softmax_unnorm(s, mask):
    s = jnp.where(mask, s, MASK_VALUE)
    m = jnp.max(s, axis=-1, keepdims=True)
    e = jnp.where(mask, jnp.exp(s - m), 0.0)
    den = jnp.maximum(jnp.sum(e, axis=-1, keepdims=True), 1e-30)
    return e, den


def _select_blocks(score_t_ref, n_pos, idx_col, idx_of_pos):
    sc = score_t_ref[...]

    def body(i, cnt):
        row = score_t_ref[pl.ds(i, 1), :]
        before = (row > sc) | ((row == sc) & (idx_of_pos(i) < idx_col))
        return cnt + before.astype(F32)

    cnt = lax.fori_loop(0, n_pos, body, jnp.zeros(sc.shape, F32))
    return ((cnt < N_SELECT) & (sc >= 0.0)).astype(F32)


def _nsa_prompt_kernel(tab_ref, q_ref, ks_ref, vs_ref, kw_ref, vw_ref, gt_ref, kc_ref, vc_ref, o_ref,
                       ksb, vsb, kwb, vwb, kcb, vcb, ebuf, boff, sct, *, seq):
    k = pl.program_id(1)
    qb = pl.program_id(2)
    nc = seq // CMP_BLOCK
    ncp = kcb.shape[0]
    n8 = sct.shape[0]
    g_heads = [k * NSA_GROUP + g for g in range(NSA_GROUP)]

    @pl.when(qb == 0)
    def _():
        ksb[...] = ks_ref[...].astype(BF16)
        vsb[...] = vs_ref[...].astype(BF16)
        kwb[0:WINDOW, :] = jnp.zeros((WINDOW, HEAD_DIM), BF16)
        kwb[WINDOW:, :] = kw_ref[...].astype(BF16)
        vwb[0:WINDOW, :] = jnp.zeros((WINDOW, HEAD_DIM), BF16)
        vwb[WINDOW:, :] = vw_ref[...].astype(BF16)
        kcb[...] = jnp.zeros(kcb.shape, BF16)
        vcb[...] = jnp.zeros(vcb.shape, BF16)
        kcb[0:nc, :] = kc_ref[...].astype(BF16)
        vcb[0:nc, :] = vc_ref[...].astype(BF16)
        blk = lax.broadcasted_iota(jnp.int32, ebuf.shape, 0)
        key = lax.broadcasted_iota(jnp.int32, ebuf.shape, 1)
        ebuf[...] = (lax.shift_right_logical(key, 6) == blk).astype(BF16)
        ti = lax.broadcasted_iota(jnp.int32, (Q_BLOCK, Q_BLOCK), 0)
        tj = lax.broadcasted_iota(jnp.int32, (Q_BLOCK, Q_BLOCK), 1)
        b0 = _t5_bias(jnp.maximum(ti - tj, 0), tab_ref, g_heads)
        b1 = _t5_bias(Q_BLOCK + ti - tj, tab_ref, g_heads)
        for g in range(NSA_GROUP):
            far = tab_ref[REL_BUCKETS - 1, g_heads[g]]
            boff[g, 0] = b0[g] - far
            boff[g, 1] = b1[g] - far

    t_col = lax.broadcasted_iota(jnp.int32, (Q_BLOCK, 1), 0)
    q_pos = qb * Q_BLOCK + t_col
    scale = HEAD_DIM ** -0.5
    qs = [(q_ref[:, g * HEAD_DIM:(g + 1) * HEAD_DIM] * scale).astype(BF16) for g in range(NSA_GROUP)]
    gates = jax.nn.sigmoid(gt_ref[...])

    cidx = lax.broadcasted_iota(jnp.int32, (Q_BLOCK, ncp), 1)
    dist_c = q_pos - (cidx * CMP_BLOCK + (CMP_BLOCK - 1))
    mask_c = (dist_c >= 0) & (cidx < nc)
    bias_c = _t5_bias(jnp.maximum(dist_c, 0), tab_ref, g_heads)
    imp = jnp.zeros((Q_BLOCK, ncp), F32)
    o_c = []
    for g in range(NSA_GROUP):
        s = _dot_nt(qs[g], kcb[...]) + bias_c[g]
        e, den = _masked_softmax_unnorm(s, mask_c)
        p = e / den
        imp = imp + p
        o_c.append(_dot(p.astype(BF16), vcb[...]))

    cur = lax.shift_right_logical(q_pos, 6)
    eligible = (cidx * CMP_BLOCK <= q_pos) & (cidx < nc)
    forced = (cidx == 0) | (cidx == cur) | (cidx == cur - 1)
    score = jnp.where(eligible, imp + jnp.where(forced, FORCE_SCORE, 0.0), -1.0)
    score = jnp.where(cidx < nc, score, -2.0)
    score_t = jnp.concatenate([score[:, c * LANES:(c + 1) * LANES].T for c in range(ncp // LANES)], axis=0)
    sct[...] = score_t[0:n8, :]
    idx_col = lax.broadcasted_iota(jnp.int32, (n8, Q_BLOCK), 0)
    sel_t = _select_blocks(sct, nc, idx_col, lambda i: i)
    if n8 < ncp:
        sel_t = jnp.concatenate([sel_t, jnp.zeros((ncp - n8, Q_BLOCK), F32)], axis=0)
    sel = jnp.concatenate([sel_t[c * LANES:(c + 1) * LANES, :].T for c in range(ncp // LANES)], axis=1)
    kpos = lax.broadcasted_iota(jnp.int32, (Q_BLOCK, seq), 1)
    mask_s = (_dot(sel.astype(BF16), ebuf[...]) > 0.5) & (kpos <= q_pos)

    wlen = WINDOW + Q_BLOCK
    jw = lax.broadcasted_iota(jnp.int32, (Q_BLOCK, wlen), 1)
    dist_w = WINDOW + t_col - jw
    mask_w = (dist_w >= 0) & (dist_w <= WINDOW) & (qb * Q_BLOCK - WINDOW + jw >= 0)
    w0 = pl.multiple_of(qb * Q_BLOCK, Q_BLOCK)
    kw = kwb[pl.ds(w0, wlen), :]
    vw = vwb[pl.ds(w0, wlen), :]
    n_far_tiles = wlen // Q_BLOCK - 2

    for g in range(NSA_GROUP):
        far = tab_ref[REL_BUCKETS - 1, g_heads[g]]
        s = _dot_nt(qs[g], ksb[...]) + far
        tile_id = lax.shift_right_logical(kpos, 7)
        b_same = jnp.concatenate([boff[g, 0]] * (seq // Q_BLOCK), axis=1)
        b_prev = jnp.concatenate([boff[g, 1]] * (seq // Q_BLOCK), axis=1)
        s = s + jnp.where(tile_id == qb, b_same, 0.0) + jnp.where(tile_id == qb - 1, b_prev, 0.0)
        e, den = _masked_softmax_unnorm(s, mask_s)
        o_s = _dot(e.astype(BF16), vsb[...]) / den
        bw = jnp.concatenate([jnp.zeros((Q_BLOCK, n_far_tiles * Q_BLOCK), F32), boff[g, 1], boff[g, 0]], axis=1)
        sw = _dot_nt(qs[g], kw) + far + bw
        ew, denw = _masked_softmax_unnorm(sw, mask_w)
        o_w = _dot(ew.astype(BF16), vw) / denw
        c0 = g * 3
        y = gates[:, c0:c0 + 1] * o_c[g] + gates[:, c0 + 1:c0 + 2] * o_s + gates[:, c0 + 2:c0 + 3] * o_w
        o_ref[:, g * HEAD_DIM:(g + 1) * HEAD_DIM] = y.astype(o_ref.dtype)


def _nsa_prompt(z, comp, rel_bias, n_batch, seq, cols):
    nqb = seq // Q_BLOCK
    nc = seq // CMP_BLOCK
    ncp = _round_up(nc, LANES)
    n8 = _round_up(nc, SUBLANES)
    kvc = cols["kv"] // HEAD_DIM
    winc = cols["win"] // HEAD_DIM
    kh = NSA_KV_HEADS
    gw = NSA_GROUP * HEAD_DIM

    def zspec(col_block_of_k):
        return pl.BlockSpec((seq, HEAD_DIM), lambda b, k, q: (b, col_block_of_k(k)))

    return pl.pallas_call(
        functools.partial(_nsa_prompt_kernel, seq=seq),
        grid=(n_batch, kh, nqb),
        in_specs=[
            pl.BlockSpec(memory_space=pltpu.SMEM),
            pl.BlockSpec((Q_BLOCK, gw), lambda b, k, q: (b * nqb + q, cols["q"] // gw + k)),
            zspec(lambda k: kvc + 2 * kh + k),
            zspec(lambda k: kvc + 3 * kh + k),
            zspec(lambda k: winc + k),
            zspec(lambda k: winc + kh + k),
            pl.BlockSpec((Q_BLOCK, LANES), lambda b, k, q: (b * nqb + q, cols["ng"] // LANES + k)),
            pl.BlockSpec((nc, HEAD_DIM), lambda b, k, q: (b, k)),
            pl.BlockSpec((nc, HEAD_DIM), lambda b, k, q: (b, kh + k)),
        ],
        out_specs=pl.BlockSpec((Q_BLOCK, gw), lambda b, k, q: (b * nqb + q, k)),
        out_shape=jax.ShapeDtypeStruct((n_batch * seq, NSA_WIDTH), F32),
        scratch_shapes=[
            pltpu.VMEM((seq, HEAD_DIM), BF16), pltpu.VMEM((seq, HEAD_DIM), BF16),
            pltpu.VMEM((seq + WINDOW, HEAD_DIM), BF16), pltpu.VMEM((seq + WINDOW, HEAD_DIM), BF16),
            pltpu.VMEM((ncp, HEAD_DIM), BF16), pltpu.VMEM((ncp, HEAD_DIM), BF16),
            pltpu.VMEM((ncp, seq), BF16),
            pltpu.VMEM((NSA_GROUP, 2, Q_BLOCK, Q_BLOCK), F32),
            pltpu.VMEM((n8, Q_BLOCK), F32),
        ],
        compiler_params=_cparams(("arbitrary", "arbitrary", "arbitrary")),
        name="nsa_prompt",
    )(rel_bias, z, z, z, z, z, z, comp, comp)


def _nsa_sample_kernel(pt_ref, tab_ref, q_ref, kvn_ref, wn_ref, gt_ref, cw_ref, pages_hbm, comp_hbm,
                       o_ref, wo_ref, pbuf, cbuf, ebuf, sct, psem, csem,
                       *, n_seq, n_pages, page0, tn):
    b = pl.program_id(0)
    slot = lax.rem(b, 2)
    past = n_pages * 2 * CMP_BLOCK
    ncs = 2 * n_pages
    ns = ncs + 1
    ncp = ebuf.shape[0]
    n8 = sct.shape[0]
    nkeys = past + LANES
    wb = cw_ref.shape[2]
    half_cols = 2 * NSA_KV_HEADS * HEAD_DIM

    def block_of(pos):
        half, page = _divmod_const(pos, n_pages)
        return jnp.where(pos < ncs, 2 * page + half, pos)

    def page_copies(seq_idx, dst_slot, j):
        pid = pt_ref[seq_idx * n_pages + j]
        return (
            pltpu.make_async_copy(pages_hbm.at[page0 + pid, :, pl.ds(half_cols, half_cols)],
                                  pbuf.at[dst_slot, j], psem.at[dst_slot]),
            pltpu.make_async_copy(comp_hbm.at[pid], cbuf.at[dst_slot, j], csem.at[dst_slot]),
        )

    def start_seq(seq_idx, dst_slot):
        def body(j, c):
            for cp in page_copies(seq_idx, dst_slot, j):
                cp.start()
            return c
        lax.fori_loop(0, n_pages, body, 0)

    def wait_seq(seq_idx, dst_slot):
        def body(j, c):
            for cp in page_copies(seq_idx, dst_slot, j):
                cp.wait()
            return c
        lax.fori_loop(0, n_pages, body, 0)

    @pl.when(b == 0)
    def _():
        start_seq(0, 0)
        pos = lax.broadcasted_iota(jnp.int32, ebuf.shape, 0)
        key = lax.broadcasted_iota(jnp.int32, ebuf.shape, 1)
        ebuf[...] = ((lax.shift_right_logical(key, 6) == block_of(pos)) & (pos <= ncs)).astype(BF16)

    @pl.when(b + 1 < n_seq)
    def _():
        start_seq(b + 1, 1 - slot)

    wait_seq(b, slot)

    t_col = lax.broadcasted_iota(jnp.int32, (tn, 1), 0)
    q_pos = past + t_col
    scale = HEAD_DIM ** -0.5
    rows = NSA_GROUP * tn

    def rep_g(x):
        return jnp.concatenate([x] * NSA_GROUP, axis=0)

    def pad_rows(x, n):
        if x.shape[0] == n:
            return x
        return jnp.concatenate([x, jnp.zeros((n - x.shape[0], x.shape[1]), x.dtype)], axis=0)

    pos1 = lax.broadcasted_iota(jnp.int32, (tn, ncp), 1)
    blk1 = block_of(pos1)
    dist_c = q_pos - (blk1 * CMP_BLOCK + (CMP_BLOCK - 1))
    mask_c = rep_g((dist_c >= 0) & (pos1 < ncs))
    cur = lax.shift_right_logical(q_pos, 6)
    eligible = (blk1 * CMP_BLOCK <= q_pos) & (pos1 <= ncs)
    forced = (blk1 == 0) | (blk1 == cur) | (blk1 == cur - 1)

    qk, o_cmp, scores = [], [], []
    for k in range(NSA_KV_HEADS):
        heads = [k * NSA_GROUP + g for g in range(NSA_GROUP)]
        q = jnp.concatenate(
            [q_ref[:, (k * NSA_GROUP + g) * HEAD_DIM:(k * NSA_GROUP + g + 1) * HEAD_DIM] for g in range(NSA_GROUP)],
            axis=0)
        q = (q * scale).astype(BF16)
        qk.append(q)
        kc = jnp.concatenate([cbuf[slot, :, k, :], cbuf[slot, :, 2 * NSA_KV_HEADS + k, :]], axis=0)
        vc = jnp.concatenate([cbuf[slot, :, NSA_KV_HEADS + k, :], cbuf[slot, :, 3 * NSA_KV_HEADS + k, :]], axis=0)
        kc = pad_rows(kc, ncp).astype(BF16)
        vc = pad_rows(vc, ncp).astype(BF16)
        bias_c = jnp.concatenate(_t5_bias(jnp.maximum(dist_c, 0), tab_ref, heads), axis=0)
        s = _dot_nt(q, kc) + bias_c
        e, den = _masked_softmax_unnorm(s, mask_c)
        p = e / den
        o_cmp.append(_dot(p.astype(BF16), vc))
        imp = p[0:tn]
        for g in range(1, NSA_GROUP):
            imp = imp + p[g * tn:(g + 1) * tn]
        sc = jnp.where(eligible, imp + jnp.where(forced, FORCE_SCORE, 0.0), -1.0)
        scores.append(jnp.where(pos1 <= ncs, sc, -2.0))

    score = pad_rows(jnp.concatenate(scores, axis=0), LANES)
    score_t = jnp.concatenate([score[:, c * LANES:(c + 1) * LANES].T for c in range(ncp // LANES)], axis=0)
    sct[...] = score_t[0:n8, :]
    posc = lax.broadcasted_iota(jnp.int32, (n8, LANES), 0)
    sel_t = _select_blocks(sct, ns, block_of(posc), block_of)
    if n8 < ncp:
        sel_t = jnp.concatenate([sel_t, jnp.zeros((ncp - n8, LANES), F32)], axis=0)
    sel = jnp.concatenate([sel_t[c * LANES:(c + 1) * LANES, :].T for c in range(ncp // LANES)], axis=1)

    kpos = lax.broadcasted_iota(jnp.int32, (tn, nkeys), 1)
    dist_s = q_pos - kpos
    causal_s = rep_g((dist_s >= 0) & (kpos < past + tn))
    near = 2 * LANES
    dist_near = jnp.maximum(dist_s[:, nkeys - near:], 0)
    jw = lax.broadcasted_iota(jnp.int32, (tn, wb + LANES), 1)
    dist_w = wb + t_col - jw
    mask_w = rep_g((dist_w >= 0) & (dist_w <= WINDOW) & (past - wb + jw >= 0) & (jw < wb + tn))
    dist_w_near = jnp.maximum(dist_w[:, wb + LANES - near:], 0)
    gates = jax.nn.sigmoid(gt_ref[...])

    for k in range(NSA_KV_HEADS):
        heads = [k * NSA_GROUP + g for g in range(NSA_GROUP)]
        far = jnp.concatenate([jnp.full((tn, 1), tab_ref[REL_BUCKETS - 1, h], F32) for h in heads], axis=0)
        q = qk[k]
        k_new = pad_rows(kvn_ref[:, (2 * NSA_KV_HEADS + k) * HEAD_DIM:(2 * NSA_KV_HEADS + k + 1) * HEAD_DIM], LANES)
        v_new = pad_rows(kvn_ref[:, (3 * NSA_KV_HEADS + k) * HEAD_DIM:(3 * NSA_KV_HEADS + k + 1) * HEAD_DIM], LANES)
        k_all = jnp.concatenate(
            [pbuf[slot, :, :, k * HEAD_DIM:(k + 1) * HEAD_DIM].reshape(past, HEAD_DIM), k_new], axis=0).astype(BF16)
        v_all = jnp.concatenate(
            [pbuf[slot, :, :, (NSA_KV_HEADS + k) * HEAD_DIM:(NSA_KV_HEADS + k + 1) * HEAD_DIM].reshape(past, HEAD_DIM),
             v_new], axis=0).astype(BF16)
        sel_k = rep_g(sel[k * tn:(k + 1) * tn, :]).astype(BF16)
        mask_s = (_dot(sel_k, ebuf[...]) > 0.5) & causal_s
        b_near = jnp.concatenate(_t5_bias(dist_near, tab_ref, heads), axis=0) - far
        bias_s = jnp.concatenate([jnp.zeros((rows, nkeys - near), F32), b_near], axis=1)
        s = _dot_nt(q, k_all) + far + bias_s
        e, den = _masked_softmax_unnorm(s, mask_s)
        o_s = _dot(e.astype(BF16), v_all) / den
        kw = jnp.concatenate([cw_ref[0, 0, :, k * HEAD_DIM:(k + 1) * HEAD_DIM],
                              pad_rows(wn_ref[:, k * HEAD_DIM:(k + 1) * HEAD_DIM], LANES)], axis=0).astype(BF16)
        vw = jnp.concatenate(
            [cw_ref[0, 0, :, (NSA_KV_HEADS + k) * HEAD_DIM:(NSA_KV_HEADS + k + 1) * HEAD_DIM],
             pad_rows(wn_ref[:, (NSA_KV_HEADS + k) * HEAD_DIM:(NSA_KV_HEADS + k + 1) * HEAD_DIM], LANES)],
            axis=0).astype(BF16)
        bw_near = jnp.concatenate(_t5_bias(dist_w_near, tab_ref, heads), axis=0) - far
        bias_w = jnp.concatenate([jnp.zeros((rows, wb + LANES - near), F32), bw_near], axis=1)
        sw = _dot_nt(q, kw) + far + bias_w
        ew, denw = _masked_softmax_unnorm(sw, mask_w)
        o_w = _dot(ew.astype(BF16), vw) / denw
        for g in range(NSA_GROUP):
            c0 = k * LANES + g * 3
            r = slice(g * tn, (g + 1) * tn)
            y = (gates[:, c0:c0 + 1] * o_cmp[k][r] + gates[:, c0 + 1:c0 + 2] * o_s[r]
                 + gates[:, c0 + 2:c0 + 3] * o_w[r])
            h = k * NSA_GROUP + g
            o_ref[:, h * HEAD_DIM:(h + 1) * HEAD_DIM] = y.astype(o_ref.dtype)

    wo_ref[0, 0:wb - tn, :] = cw_ref[0, 0, tn:wb, :]
    wo_ref[0, wb - tn:wb, :] = wn_ref[...]


def _nsa_sample(z, row0, n_seq, tn, pages, page0, comp_pages, cache_win_l, layer, page_table, rel_bias, cols):
    n_pages = page_table.shape[1]
    page_size = pages.shape[1]
    assert page_size == 2 * CMP_BLOCK and tn == SUBLANES and row0 % tn == 0
    past = n_pages * page_size
    ncs = 2 * n_pages
    ncp = _round_up(ncs + 1, LANES)
    n8 = _round_up(ncs + 1, SUBLANES)
    wb = cache_win_l.shape[2]
    assert wb == WINDOW and past >= WINDOW
    half_cols = 2 * NSA_KV_HEADS * HEAD_DIM
    rb = row0 // tn
    grid_spec = pltpu.PrefetchScalarGridSpec(
        num_scalar_prefetch=1,
        grid=(n_seq,),
        in_specs=[
            pl.BlockSpec(memory_space=pltpu.SMEM),
            pl.BlockSpec((tn, NSA_WIDTH), lambda b, pt: (rb + b, cols["q"] // NSA_WIDTH)),
            pl.BlockSpec((tn, 2 * half_cols), lambda b, pt: (rb + b, cols["kv"] // (2 * half_cols))),
            pl.BlockSpec((tn, half_cols), lambda b, pt: (rb + b, cols["win"] // half_cols)),
            pl.BlockSpec((tn, 2 * LANES), lambda b, pt: (rb + b, cols["ng"] // (2 * LANES))),
            pl.BlockSpec((1, 1, wb, half_cols), lambda b, pt: (layer, b, 0, 0)),
            pl.BlockSpec(memory_space=pl.ANY),
            pl.BlockSpec(memory_space=pl.ANY),
        ],
        out_specs=[
            pl.BlockSpec((tn, NSA_WIDTH), lambda b, pt: (b, 0)),
            pl.BlockSpec((1, wb, half_cols), lambda b, pt: (b, 0, 0)),
        ],
        scratch_shapes=[
            pltpu.VMEM((2, n_pages, page_size, half_cols), F32),
            pltpu.VMEM((2, n_pages, 2 * 2 * NSA_KV_HEADS, HEAD_DIM), F32),
            pltpu.VMEM((ncp, past + LANES), BF16),
            pltpu.VMEM((n8, LANES), F32),
            pltpu.SemaphoreType.DMA((2,)),
            pltpu.SemaphoreType.DMA((2,)),
        ],
    )
    return pl.pallas_call(
        functools.partial(_nsa_sample_kernel, n_seq=n_seq, n_pages=n_pages, page0=page0, tn=tn),
        grid_spec=grid_spec,
        out_shape=[jax.ShapeDtypeStruct((n_seq * tn, NSA_WIDTH), F32),
                   jax.ShapeDtypeStruct((n_seq, wb, half_cols), F32)],
        compiler_params=_cparams(("arbitrary",)),
        name="nsa_sample",
    )(page_table.reshape(-1), rel_bias, z, z, z, z, cache_win_l, pages, comp_pages)


def _cumsum8(x):
    row = lax.broadcasted_iota(jnp.int32, x.shape, 0)
    for sh in (1, 2, 4):
        x = x + jnp.where(row >= sh, pltpu.roll(x, sh, 0), 0.0)
    return x


def _mixer_kernel(zcb_ref, zcc_ref, zch_ref, zhg_ref, cw_ref, cs_ref, hs_ref, lb_ref, hn_ref,
                  yb_ref, yc_ref, cso_ref, hso_ref, uprev, st, qf_s, k_s, lf_s, *, layer, tt):
    ti = pl.program_id(1)
    nt = pl.num_programs(1)

    @pl.when(ti == 0)
    def _():
        uprev[...] = jnp.zeros(uprev.shape, F32)
        uprev[SUBLANES - (CONV_K - 1):SUBLANES, :] = cs_ref[0]
        for h in range(HG_HEADS):
            st[h] = hs_ref[0, h].T

    cb = zcb_ref[...]
    u = zcc_ref[...] * zch_ref[...]
    ext = jnp.concatenate([uprev[...], u], axis=0)
    y = cw_ref[CONV_K - 1:CONV_K, :] * u
    for j in range(CONV_K - 1):
        shift = CONV_K - 1 - j
        y = y + cw_ref[j:j + 1, :] * pltpu.roll(ext, shift, 0)[SUBLANES:, :]
    yb_ref[...] = (cb * y).astype(yb_ref.dtype)
    tail = ext[tt:tt + SUBLANES, :]
    uprev[...] = tail

    @pl.when(ti == nt - 1)
    def _():
        cso_ref[0] = pltpu.roll(tail, CONV_K - 1, 0)[0:CONV_K - 1, :]

    hw = HG_HEADS * HG_DK
    lg = lb_ref[...]
    mx = jnp.max(lg, axis=0, keepdims=True)
    ex = jnp.exp(lg - mx)
    sm = ex / jnp.sum(ex, axis=0, keepdims=True)
    cs = sm[0:1]
    for i in range(1, layer + 1):
        cs = cs + sm[i:i + 1]
    lb = cs - sm[0:1]
    fx = zhg_ref[:, hw:2 * hw]
    lb_pos = lb > 0.0
    log_lb = jnp.log(jnp.where(lb_pos, lb, 1.0))
    log_sig = jnp.minimum(fx, 0.0) - jnp.log1p(jnp.exp(-jnp.abs(fx)))
    a2 = jnp.log1p(-lb) + log_sig
    lae = jnp.maximum(log_lb, a2) + jnp.log1p(jnp.exp(-jnp.abs(log_lb - a2)))
    lf_s[...] = jnp.where(lb_pos, lae, log_sig)
    k_s[...] = (1.0 - lb) * jax.nn.sigmoid(-fx)
    hq = zhg_ref[:, 0:hw]
    qf_s[...] = hq * jax.nn.sigmoid(hq)
    norm_g = hn_ref[...]
    ts = lax.broadcasted_iota(jnp.int32, (HG_BLOCK, HG_BLOCK, 1), 0)
    ss = lax.broadcasted_iota(jnp.int32, (HG_BLOCK, HG_BLOCK, 1), 1)
    causal = ss <= ts

    def block(i, c):
        r0 = pl.multiple_of(i * HG_BLOCK, HG_BLOCK)
        rs = pl.ds(r0, HG_BLOCK)
        for h in range(HG_HEADS):
            kc = slice(h * HG_DK, (h + 1) * HG_DK)
            vcol = slice(2 * hw + h * HG_DV, 2 * hw + (h + 1) * HG_DV)
            gcol = slice(2 * hw + HG_HEADS * HG_DV + h * HG_DV, 2 * hw + HG_HEADS * HG_DV + (h + 1) * HG_DV)
            qf = qf_s[rs, kc]
            kk = k_s[rs, kc]
            g = _cumsum8(lf_s[rs, kc])
            v = zhg_ref[rs, vcol]
            gl = g[HG_BLOCK - 1:HG_BLOCK, :]
            st_h = st[h]
            o = _dot_nt((qf * jnp.exp(g)).astype(BF16), st_h.astype(BF16))
            diff = jnp.where(causal, g[:, None, :] - g[None, :, :], 0.0)
            x3 = jnp.where(causal, qf[:, None, :] * kk[None, :, :] * jnp.exp(diff), 0.0)
            a = jnp.sum(x3, axis=-1, keepdims=True)
            o = o + jnp.sum(a * v[None, :, :], axis=1)
            kd = kk * jnp.exp(gl - g)
            st[h] = st_h * jnp.exp(gl) + _dot_tn(v.astype(BF16), kd.astype(BF16))
            on = o * lax.rsqrt(jnp.mean(o * o, axis=-1, keepdims=True) + NORM_EPS) * norm_g
            gx = zhg_ref[rs, gcol]
            yc_ref[rs, h * HG_DV:(h + 1) * HG_DV] = (on * (gx * jax.nn.sigmoid(gx))).astype(yc_ref.dtype)
        return c

    lax.fori_loop(0, tt // HG_BLOCK, block, 0)

    @pl.when(ti == nt - 1)
    def _():
        for h in range(HG_HEADS):
            hso_ref[0, h] = st[h].T


def _mixer(z, row0, n_seq, seq, conv_w_l, conv_state, hg_state, lb_logits, hg_norm_l, layer, cols):
    cw = conv_w_l.shape[-1]
    hgw = 2 * HG_HEADS * HG_DK + 2 * HG_HEADS * HG_DV
    tt = _pick_tile(seq, 256, SUBLANES)
    nt = seq // tt
    rb = row0 // tt
    assert row0 % tt == 0 and cols["cv"] % cw == 0 and cols["hg"] % hgw == 0
    depth = lb_logits.shape[0]
    cvb = cols["cv"] // cw
    return pl.pallas_call(
        functools.partial(_mixer_kernel, layer=layer, tt=tt),
        grid=(n_seq, nt),
        in_specs=[
            pl.BlockSpec((tt, cw), lambda b, t: (rb + b * nt + t, cvb)),
            pl.BlockSpec((tt, cw), lambda b, t: (rb + b * nt + t, cvb + 1)),
            pl.BlockSpec((tt, cw), lambda b, t: (rb + b * nt + t, cvb + 2)),
            pl.BlockSpec((tt, hgw), lambda b, t: (rb + b * nt + t, cols["hg"] // hgw)),
            pl.BlockSpec((CONV_K, cw), lambda b, t: (0, 0)),
            pl.BlockSpec((1, CONV_K - 1, cw), lambda b, t: (b, 0, 0)),
            pl.BlockSpec((1, HG_HEADS, HG_DK, HG_DV), lambda b, t: (b, 0, 0, 0)),
            pl.BlockSpec((depth, HG_HEADS * HG_DK), lambda b, t: (0, 0)),
            pl.BlockSpec((1, HG_DV), lambda b, t: (0, 0)),
        ],
        out_specs=[
            pl.BlockSpec((tt, cw), lambda b, t: (b * nt + t, 0)),
            pl.BlockSpec((tt, HG_HEADS * HG_DV), lambda b, t: (b * nt + t, 0)),
            pl.BlockSpec((1, CONV_K - 1, cw), lambda b, t: (b, 0, 0)),
            pl.BlockSpec((1, HG_HEADS, HG_DK, HG_DV), lambda b, t: (b, 0, 0, 0)),
        ],
        out_shape=[
            jax.ShapeDtypeStruct((n_seq * seq, cw), F32),
            jax.ShapeDtypeStruct((n_seq * seq, HG_HEADS * HG_DV), F32),
            jax.ShapeDtypeStruct((n_seq, CONV_K - 1, cw), F32),
            jax.ShapeDtypeStruct((n_seq, HG_HEADS, HG_DK, HG_DV), F32),
        ],
        scratch_shapes=[
            pltpu.VMEM((SUBLANES, cw), F32),
            pltpu.VMEM((HG_HEADS, HG_DV, HG_DK), F32),
            pltpu.VMEM((tt, HG_HEADS * HG_DK), F32),
            pltpu.VMEM((tt, HG_HEADS * HG_DK), F32),
            pltpu.VMEM((tt, HG_HEADS * HG_DK), F32),
        ],
        compiler_params=_cparams(("arbitrary", "arbitrary")),
        name="mixer_conv_hgrn",
    )(z, z, z, z, conv_w_l, conv_state, hg_state, lb_logits, hg_norm_l.reshape(1, HG_DV))


def _ep_identity(prods, extras):
    return prods[0]


def _ep_mix(prods, extras):
    ga, gb, gc = (jax.nn.sigmoid(e) for e in extras)
    return ga * prods[0] + gb * prods[1] + gc * prods[2]


def _ep_residual(prods, extras):
    return extras[0] + prods[0]


def _ep_swiglu(prods, extras):
    return prods[0] * jax.nn.sigmoid(prods[0]) * prods[1]


def _ep_ple(prods, extras):
    return extras[0] + jax.nn.sigmoid(prods[0]) * prods[1]


def kernel(x_prompt, x_sample, p_prompt, p_sample, cache_kv, cache_win, state_conv, state_hgrn, page_table,
           rel_bias, g_mix, w_in, phi_pe, phi_w1, phi_w2, conv_w, hg_lb_logits, hg_norm, w_pa, w_pb, w_pc, w_o,
           g_ffn, w_gate, w_up, w_down, w_pg, w_pp, g_final):
    bp, seq, d = x_prompt.shape
    bs, tn, _ = x_sample.shape
    depth = w_in.shape[0]
    n_in = w_in.shape[-1]
    cw = conv_w.shape[-1]
    d_ff = w_gate.shape[-1]
    ple = w_pp.shape[1]
    n_pool, page_size = cache_kv.shape[1], cache_kv.shape[2]
    mp, ms = bp * seq, bs * tn
    m = mp + ms
    kvw = 6 * NSA_KV_HEADS * HEAD_DIM
    hgw = 2 * HG_HEADS * HG_DK + 2 * HG_HEADS * HG_DV
    assert n_in == NSA_WIDTH + kvw + N_GATES + 3 * cw + hgw + 3 * d
    assert seq % Q_BLOCK == 0 and seq >= WINDOW

    o_ng = NSA_WIDTH + kvw
    cols = {"q": 0, "kv": NSA_WIDTH, "win": NSA_WIDTH + N_KV_ROWS * NSA_KV_HEADS * HEAD_DIM,
            "cv": o_ng, "hg": o_ng + 3 * cw, "mg": o_ng + 3 * cw + hgw, "ng": n_in - N_GATES}
    nz = cols["ng"] + GATE_PAD
    gk = N_GATES // NSA_KV_HEADS

    def relayout_w_in(w):
        gate_cols = w[:, o_ng:o_ng + N_GATES]
        parts = [w[:, :o_ng], w[:, o_ng + N_GATES:]]
        for k in range(NSA_KV_HEADS):
            parts += [gate_cols[:, k * gk:(k + 1) * gk], jnp.zeros((d, LANES - gk), w.dtype)]
        parts.append(jnp.zeros((d, GATE_PAD - NSA_KV_HEADS * LANES), w.dtype))
        return jnp.concatenate(parts, axis=1).astype(BF16)

    tm = _pick_tile(m, 1024, 256) if m % 256 == 0 else _pick_tile(m, 1024, SUBLANES)
    h = jnp.concatenate([x_prompt.reshape(mp, d), x_sample.reshape(ms, d)], axis=0)
    pages = cache_kv.reshape(depth * n_pool, page_size, N_KV_ROWS * NSA_KV_HEADS * HEAD_DIM)
    pool_blocks = cache_kv.reshape(depth * n_pool * (page_size // CMP_BLOCK), CMP_BLOCK,
                                   N_KV_ROWS * NSA_KV_HEADS * HEAD_DIM)
    nb_pool = n_pool * (page_size // CMP_BLOCK)
    cache_win_r = cache_win.reshape(depth, bs, cache_win.shape[2], 2 * NSA_KV_HEADS * HEAD_DIM)
    zeros_conv = jnp.zeros((bp, CONV_K - 1, cw), F32)
    zeros_hg = jnp.zeros((bp, HG_HEADS, HG_DK, HG_DV), F32)

    def w2d(w):
        return w.reshape(w.shape[0] * w.shape[1], w.shape[2])

    kv_p, kv_s, win_p, win_s, conv_p, conv_s, hg_p, hg_s = [], [], [], [], [], [], [], []
    for l in range(depth):
        xn = _rmsnorm(h, g_mix[l], BF16)
        z = _fused_matmul([xn], [0], [(relayout_w_in(w_in[l]), 0)], [], _ep_identity, nz, F32,
                          tm, _pick_tile(nz, 512, LANES), "in_proj")

        zb = z.reshape(m // CMP_BLOCK, CMP_BLOCK, nz)
        comp_prompt = _compress(zb, 0, mp // CMP_BLOCK, cols["kv"] // HEAD_DIM, phi_pe[l], phi_w1[l], phi_w2[l],
                                _pick_tile(mp // CMP_BLOCK, 256, SUBLANES), "compress_prompt")
        cb_pool = _pick_tile(nb_pool, 256, SUBLANES)
        comp_pool = _compress(pool_blocks, l * nb_pool // cb_pool, nb_pool, 0, phi_pe[l], phi_w1[l], phi_w2[l],
                              cb_pool, "compress_pool")
        comp_pages = comp_pool.reshape(n_pool, (page_size // CMP_BLOCK) * 2 * NSA_KV_HEADS, HEAD_DIM)
        ya_p = _nsa_prompt(z, comp_prompt, rel_bias, bp, seq, cols)
        ya_s, win_new = _nsa_sample(z, mp, bs, tn, pages, l * n_pool, comp_pages, cache_win_r, l, page_table,
                                    rel_bias, cols)
        yb_p, yc_p, cs_p, hs_p = _mixer(z, 0, bp, seq, conv_w[l], zeros_conv, zeros_hg, hg_lb_logits,
                                        hg_norm[l], l, cols)
        yb_s, yc_s, cs_s, hs_s = _mixer(z, mp, bs, tn, conv_w[l], state_conv[l], state_hgrn[l], hg_lb_logits,
                                        hg_norm[l], l, cols)
        ya = jnp.concatenate([ya_p, ya_s], axis=0)
        yb = jnp.concatenate([yb_p, yb_s], axis=0)
        yc = jnp.concatenate([yc_p, yc_s], axis=0)

        tn_d = _pick_tile(d, 512, LANES)
        mgb = cols["mg"] // tn_d
        mixed = _fused_matmul([ya, yb, yc], [0, 1, 2], [(w2d(w_pa), l), (w2d(w_pb), l), (w2d(w_pc), l)],
                              [(z, mgb), (z, mgb + d // tn_d), (z, mgb + 2 * (d // tn_d))],
                              _ep_mix, d, BF16, tm, tn_d, "mix_proj")
        h = _fused_matmul([mixed], [0], [(w2d(w_o), l)], [(h, 0)], _ep_residual, d, F32, tm, tn_d, "out_proj")
        xf = _rmsnorm(h, g_ffn[l], BF16)
        act = _fused_matmul([xf], [0, 0], [(w2d(w_gate), l), (w2d(w_up), l)], [], _ep_swiglu, d_ff, BF16,
                            tm, _pick_tile(d_ff, 512, LANES), "ffn_up")
        h = _fused_matmul([act], [0], [(w2d(w_down), l)], [(h, 0)], _ep_residual, d, F32,
                          tm, _pick_tile(d, 256, LANES), "ffn_down")
        p_all = jnp.concatenate([p_prompt[l].reshape(mp, ple), p_sample[l].reshape(ms, ple)], axis=0)
        h = _fused_matmul([h, p_all], [0, 1], [(w2d(w_pg), l), (w2d(w_pp), l)], [(h, 0)], _ep_ple, d, F32,
                          tm, tn_d, "ple_gate")

        kvr = z[:, cols["kv"]:cols["win"]]
        kv_p.append(kvr[:mp].reshape(bp, seq, N_KV_ROWS, NSA_KV_HEADS, HEAD_DIM))
        kv_s.append(kvr[mp:].reshape(bs, tn, N_KV_ROWS, NSA_KV_HEADS, HEAD_DIM))
        keep = min(WINDOW, seq)
        zw = z[:mp, cols["win"]:cols["cv"]].reshape(bp, seq, 2, NSA_KV_HEADS, HEAD_DIM)
        win_p.append(zw[:, seq - keep:])
        win_s.append(win_new.reshape(bs, win_new.shape[1], 2, NSA_KV_HEADS, HEAD_DIM))
        conv_p.append(cs_p); conv_s.append(cs_s); hg_p.append(hs_p); hg_s.append(hs_s)

    y = _rmsnorm(h, g_final, F32)
    return (y[:mp].reshape(bp, seq, d), y[mp:].reshape(bs, tn, d), jnp.stack(kv_p), jnp.stack(kv_s),
            jnp.stack(win_p), jnp.stack(win_s), jnp.stack(conv_p), jnp.stack(conv_s),
            jnp.stack(hg_p), jnp.stack(hg_s))
```

```python
import functools
import math

import numpy as np
import jax
import jax.numpy as jnp
from jax import lax
from jax.experimental import pallas as pl
from jax.experimental.pallas import tpu as pltpu

F32 = jnp.float32
BF16 = jnp.bfloat16

LANES = 128
SUBLANES = 8
VMEM_LIMIT_BYTES = 56 * 1024 * 1024

NSA_HEADS = 8
NSA_KV_HEADS = 2
NSA_GROUP = NSA_HEADS // NSA_KV_HEADS
HEAD_DIM = 128
NSA_WIDTH = NSA_HEADS * HEAD_DIM
CMP_BLOCK = 64
N_SELECT = 16
WINDOW = 512
Q_BLOCK = 128
FORCE_SCORE = 1.0e4
MASK_VALUE = -1.0e30
N_KV_ROWS = 4
CONV_K = 3
HG_HEADS = 4
HG_DK = 128
HG_DV = 128
HG_BLOCK = SUBLANES
REL_BUCKETS = 32
REL_MAX_DIST = 128
NORM_EPS = 1e-6
N_GATES = 3 * NSA_HEADS
GATE_PAD = 4 * LANES


def _round_up(x, m):
    return -(-x // m) * m


def _pick_tile(n, pref, unit):
    if n <= pref:
        return n
    best = None
    for t in range(unit, pref + 1, unit):
        if n % t == 0:
            best = t
    assert best is not None, (n, pref, unit)
    return best


def _divmod_const(x, n):
    if n & (n - 1) == 0:
        return lax.shift_right_logical(x, n.bit_length() - 1), x & (n - 1)
    return x // n, lax.rem(x, n)


def _t5_thresholds():
    n = np.arange(0, REL_MAX_DIST + 1)
    max_exact = REL_BUCKETS // 2
    nf = np.maximum(n, 1).astype(np.float32)
    ratio = np.log(nf / np.float32(max_exact)) / np.float32(math.log(REL_MAX_DIST / max_exact))
    large = max_exact + (ratio * np.float32(REL_BUCKETS - max_exact)).astype(np.int32)
    large = np.minimum(large, REL_BUCKETS - 1)
    bucket = np.where(n < max_exact, n, large)
    assert np.all(np.diff(bucket) >= 0) and bucket[-1] == REL_BUCKETS - 1
    return [int(np.argmax(bucket >= j)) for j in range(1, REL_BUCKETS)]


T5_THRESHOLDS = _t5_thresholds()


def _t5_bias(dist, tab_ref, heads):
    ge = [dist >= thr for thr in T5_THRESHOLDS]
    out = []
    for h in heads:
        b = jnp.full(dist.shape, tab_ref[0, h], F32)
        for j, m in enumerate(ge):
            b = jnp.where(m, tab_ref[j + 1, h], b)
        out.append(b)
    return out


def _dot(a, b):
    return jnp.dot(a, b, preferred_element_type=F32)


def _dot_nt(a, b):
    return lax.dot_general(a, b, (((1,), (1,)), ((), ())), preferred_element_type=F32)


def _dot_tn(a, b):
    return lax.dot_general(a, b, (((0,), (0,)), ((), ())), preferred_element_type=F32)


def _cparams(sem):
    return pltpu.CompilerParams(dimension_semantics=sem, vmem_limit_bytes=VMEM_LIMIT_BYTES)


def _rmsnorm_kernel(x_ref, g_ref, o_ref):
    x = x_ref[...]
    y = x * lax.rsqrt(jnp.mean(x * x, axis=-1, keepdims=True) + NORM_EPS)
    o_ref[...] = (y * g_ref[...]).astype(o_ref.dtype)


def _rmsnorm(x, g, out_dtype):
    m, d = x.shape
    tm = _pick_tile(m, 1024, SUBLANES)
    return pl.pallas_call(
        _rmsnorm_kernel,
        grid=(m // tm,),
        in_specs=[pl.BlockSpec((tm, d), lambda i: (i, 0)), pl.BlockSpec((1, d), lambda i: (0, 0))],
        out_specs=pl.BlockSpec((tm, d), lambda i: (i, 0)),
        out_shape=jax.ShapeDtypeStruct((m, d), out_dtype),
        compiler_params=_cparams(("parallel",)),
        name="rmsnorm",
    )(x, g.reshape(1, d))


def _mm_kernel(*refs, n_x, pairs, n_extra, epilogue, cast_w):
    xs = refs[:n_x]
    ws = refs[n_x:n_x + len(pairs)]
    extras = refs[n_x + len(pairs):n_x + len(pairs) + n_extra]
    o_ref = refs[n_x + len(pairs) + n_extra]
    wscr = refs[n_x + len(pairs) + n_extra + 1:]
    i = pl.program_id(1)

    if any(cast_w):
        @pl.when(i == 0)
        def _():
            s = 0
            for p in range(len(pairs)):
                if cast_w[p]:
                    wscr[s][...] = ws[p][...].astype(BF16)
                    s += 1

    xv = [x[...].astype(BF16) for x in xs]
    prods = []
    s = 0
    for p, xi in enumerate(pairs):
        if cast_w[p]:
            w = wscr[s][...]
            s += 1
        else:
            w = ws[p][...]
        prods.append(_dot(xv[xi], w))
    o_ref[...] = epilogue(prods, [e[...] for e in extras]).astype(o_ref.dtype)


def _fused_matmul(xs, pairs, ws, extras, epilogue, n_out, out_dtype, tm, tn, name):
    m = xs[0].shape[0]
    assert m % tm == 0 and n_out % tn == 0
    cast_w = tuple(w.dtype != BF16 for w, _ in ws)
    in_specs = []
    for x in xs:
        in_specs.append(pl.BlockSpec((tm, x.shape[1]), lambda j, i: (i, 0)))
    for p, (w, roff) in enumerate(ws):
        k = xs[pairs[p]].shape[1]
        in_specs.append(pl.BlockSpec((k, tn), functools.partial(lambda j, i, r: (r, j), r=roff)))
    for e, coff in extras:
        in_specs.append(pl.BlockSpec((tm, tn), functools.partial(lambda j, i, c: (i, c + j), c=coff)))
    scratch = [pltpu.VMEM((xs[pairs[p]].shape[1], tn), BF16) for p in range(len(ws)) if cast_w[p]]
    kern = functools.partial(_mm_kernel, n_x=len(xs), pairs=tuple(pairs), n_extra=len(extras),
                             epilogue=epilogue, cast_w=cast_w)
    return pl.pallas_call(
        kern,
        grid=(n_out // tn, m // tm),
        in_specs=in_specs,
        out_specs=pl.BlockSpec((tm, tn), lambda j, i: (i, j)),
        out_shape=jax.ShapeDtypeStruct((m, n_out), out_dtype),
        scratch_shapes=scratch,
        compiler_params=_cparams(("arbitrary", "arbitrary")),
        name=name,
    )(*xs, *[w for w, _ in ws], *[e for e, _ in extras])


def _compress_kernel(blk_ref, pe_ref, w1_ref, w2_ref, o_ref, x2d, w1b):
    @pl.when(pl.program_id(1) == 0)
    def _():
        w1b[...] = w1_ref[0].astype(BF16)

    cb = x2d.shape[0]
    for tau in range(CMP_BLOCK):
        x2d[:, tau * HEAD_DIM:(tau + 1) * HEAD_DIM] = (
            blk_ref[pl.ds(tau, cb, stride=CMP_BLOCK), :] + pe_ref[0, tau:tau + 1, :]).astype(BF16)
    h1 = _dot(x2d[...], w1b[...])
    h1 = h1 * jax.nn.sigmoid(h1)
    o_ref[...] = _dot(h1.astype(BF16), w2_ref[0].astype(BF16))


def _compress(tokens, n_blocks, col_block0, pe, w1, w2, cb, name):
    n_rk = 2 * NSA_KV_HEADS
    return pl.pallas_call(
        _compress_kernel,
        grid=(n_rk, n_blocks // cb),
        in_specs=[
            pl.BlockSpec((cb * CMP_BLOCK, HEAD_DIM), lambda rk, t: (t, col_block0 + rk)),
            pl.BlockSpec((1, CMP_BLOCK, HEAD_DIM), lambda rk, t: (rk // NSA_KV_HEADS, 0, 0)),
            pl.BlockSpec((1, CMP_BLOCK * HEAD_DIM, HEAD_DIM), lambda rk, t: (rk // NSA_KV_HEADS, 0, 0)),
            pl.BlockSpec((1, HEAD_DIM, HEAD_DIM), lambda rk, t: (rk // NSA_KV_HEADS, 0, 0)),
        ],
        out_specs=pl.BlockSpec((cb, HEAD_DIM), lambda rk, t: (t, rk)),
        out_shape=jax.ShapeDtypeStruct((n_blocks, n_rk * HEAD_DIM), F32),
        scratch_shapes=[pltpu.VMEM((cb, CMP_BLOCK * HEAD_DIM), BF16),
                        pltpu.VMEM((CMP_BLOCK * HEAD_DIM, HEAD_DIM), BF16)],
        compiler_params=_cparams(("arbitrary", "arbitrary")),
        name=name,
    )(tokens, pe, w1, w2)


def _compress_rows_kernel(tok_ref, pe_ref, w1_ref, w2_ref, o_ref, x2d, *, n_rows):
    cb = o_ref.shape[0]
    blk_stride = CMP_BLOCK * n_rows
    for r in range(2):
        for k in range(NSA_KV_HEADS):
            rk = r * NSA_KV_HEADS + k
            for tau in range(CMP_BLOCK):
                x2d[k * cb:(k + 1) * cb, tau * HEAD_DIM:(tau + 1) * HEAD_DIM] = (
                    tok_ref[pl.ds(tau * n_rows + rk, cb, stride=blk_stride), :]
                    + pe_ref[r, tau:tau + 1, :]).astype(BF16)
        h1 = _dot(x2d[...], w1_ref[r])
        h1 = h1 * jax.nn.sigmoid(h1)
        h2 = _dot(h1.astype(BF16), w2_ref[r].astype(BF16))
        for k in range(NSA_KV_HEADS):
            rk = r * NSA_KV_HEADS + k
            o_ref[:, rk * HEAD_DIM:(rk + 1) * HEAD_DIM] = h2[k * cb:(k + 1) * cb]


def _compress_rows(rows2d, row_block0, n_blocks, n_rows, pe, w1_bf16, w2, cb, name):
    n_rk = 2 * NSA_KV_HEADS
    blk = cb * CMP_BLOCK * n_rows
    return pl.pallas_call(
        functools.partial(_compress_rows_kernel, n_rows=n_rows),
        grid=(n_blocks // cb,),
        in_specs=[
            pl.BlockSpec((blk, HEAD_DIM), lambda t: (row_block0 + t, 0)),
            pl.BlockSpec((2, CMP_BLOCK, HEAD_DIM), lambda t: (0, 0, 0)),
            pl.BlockSpec((2, CMP_BLOCK * HEAD_DIM, HEAD_DIM), lambda t: (0, 0, 0)),
            pl.BlockSpec((2, HEAD_DIM, HEAD_DIM), lambda t: (0, 0, 0)),
        ],
        out_specs=pl.BlockSpec((cb, n_rk * HEAD_DIM), lambda t: (t, 0)),
        out_shape=jax.ShapeDtypeStruct((n_blocks, n_rk * HEAD_DIM), F32),
        scratch_shapes=[pltpu.VMEM((NSA_KV_HEADS * cb, CMP_BLOCK * HEAD_DIM), BF16)],
        compiler_params=_cparams(("arbitrary",)),
        name=name,
    )(rows2d, pe, w1_bf16, w2)


def _masked_softmax_unnorm(s, mask):
    s = jnp.where(mask, s, MASK_VALUE)
    m = jnp.max(s, axis=-1, keepdims=True)
    e = jnp.where(mask, jnp.exp(s - m), 0.0)
    den = jnp.maximum(jnp.sum(e, axis=-1, keepdims=True), 1e-30)
    return e, den


def _select_blocks(score_t_ref, n_pos, idx_col, idx_of_pos):
    sc = score_t_ref[...]

    def body(i, cnt):
        row = score_t_ref[pl.ds(i, 1), :]
        before = (row > sc) | ((row == sc) & (idx_of_pos(i) < idx_col))
        return cnt + before.astype(F32)

    cnt = lax.fori_loop(0, n_pos, body, jnp.zeros(sc.shape, F32))
    return ((cnt < N_SELECT) & (sc >= 0.0)).astype(F32)


def _nsa_prompt_kernel(tab_ref, q_ref, ks_ref, vs_ref, kw_ref, vw_ref, gt_ref, kc_ref, vc_ref, o_ref,
                       ksb, vsb, kwb, vwb, kcb, vcb, ebuf, boff, sct, *, seq):
    k = pl.program_id(1)
    qb = pl.program_id(2)
    nc = seq // CMP_BLOCK
    ncp = kcb.shape[0]
    n8 = sct.shape[0]
    g_heads = [k * NSA_GROUP + g for g in range(NSA_GROUP)]

    @pl.when(qb == 0)
    def _():
        ksb[...] = ks_ref[...].astype(BF16)
        vsb[...] = vs_ref[...].astype(BF16)
        kwb[0:WINDOW, :] = jnp.zeros((WINDOW, HEAD_DIM), BF16)
        kwb[WINDOW:, :] = kw_ref[...].astype(BF16)
        vwb[0:WINDOW, :] = jnp.zeros((WINDOW, HEAD_DIM), BF16)
        vwb[WINDOW:, :] = vw_ref[...].astype(BF16)
        kcb[...] = jnp.zeros(kcb.shape, BF16)
        vcb[...] = jnp.zeros(vcb.shape, BF16)
        kcb[0:nc, :] = kc_ref[...].astype(BF16)
        vcb[0:nc, :] = vc_ref[...].astype(BF16)
        blk = lax.broadcasted_iota(jnp.int32, ebuf.shape, 0)
        key = lax.broadcasted_iota(jnp.int32, ebuf.shape, 1)
        ebuf[...] = (lax.shift_right_logical(key, 6) == blk).astype(BF16)
        ti = lax.broadcasted_iota(jnp.int32, (Q_BLOCK, Q_BLOCK), 0)
        tj = lax.broadcasted_iota(jnp.int32, (Q_BLOCK, Q_BLOCK), 1)
        b0 = _t5_bias(jnp.maximum(ti - tj, 0), tab_ref, g_heads)
        b1 = _t5_bias(Q_BLOCK + ti - tj, tab_ref, g_heads)
        for g in range(NSA_GROUP):
            far = tab_ref[REL_BUCKETS - 1, g_heads[g]]
            boff[g, 0] = b0[g] - far
            boff[g, 1] = b1[g] - far

    t_col = lax.broadcasted_iota(jnp.int32, (Q_BLOCK, 1), 0)
    q_pos = qb * Q_BLOCK + t_col
    scale = HEAD_DIM ** -0.5
    qs = [(q_ref[:, g * HEAD_DIM:(g + 1) * HEAD_DIM] * scale).astype(BF16) for g in range(NSA_GROUP)]
    gates = jax.nn.sigmoid(gt_ref[...])

    cidx = lax.broadcasted_iota(jnp.int32, (Q_BLOCK, ncp), 1)
    dist_c = q_pos - (cidx * CMP_BLOCK + (CMP_BLOCK - 1))
    mask_c = (dist_c >= 0) & (cidx < nc)
    bias_c = _t5_bias(jnp.maximum(dist_c, 0), tab_ref, g_heads)
    imp = jnp.zeros((Q_BLOCK, ncp), F32)
    o_c = []
    for g in range(NSA_GROUP):
        s = _dot_nt(qs[g], kcb[...]) + bias_c[g]
        e, den = _masked_softmax_unnorm(s, mask_c)
        p = e / den
        imp = imp + p
        o_c.append(_dot(p.astype(BF16), vcb[...]))

    cur = lax.shift_right_logical(q_pos, 6)
    eligible = (cidx * CMP_BLOCK <= q_pos) & (cidx < nc)
    forced = (cidx == 0) | (cidx == cur) | (cidx == cur - 1)
    score = jnp.where(eligible, imp + jnp.where(forced, FORCE_SCORE, 0.0), -1.0)
    score = jnp.where(cidx < nc, score, -2.0)
    score_t = jnp.concatenate([score[:, c * LANES:(c + 1) * LANES].T for c in range(ncp // LANES)], axis=0)
    sct[...] = score_t[0:n8, :]
    idx_col = lax.broadcasted_iota(jnp.int32, (n8, Q_BLOCK), 0)
    sel_t = _select_blocks(sct, nc, idx_col, lambda i: i)
    if n8 < ncp:
        sel_t = jnp.concatenate([sel_t, jnp.zeros((ncp - n8, Q_BLOCK), F32)], axis=0)
    sel = jnp.concatenate([sel_t[c * LANES:(c + 1) * LANES, :].T for c in range(ncp // LANES)], axis=1)
    kpos = lax.broadcasted_iota(jnp.int32, (Q_BLOCK, seq), 1)
    mask_s = (_dot(sel.astype(BF16), ebuf[...]) > 0.5) & (kpos <= q_pos)

    wlen = WINDOW + Q_BLOCK
    jw = lax.broadcasted_iota(jnp.int32, (Q_BLOCK, wlen), 1)
    dist_w = WINDOW + t_col - jw
    mask_w = (dist_w >= 0) & (dist_w <= WINDOW) & (qb * Q_BLOCK - WINDOW + jw >= 0)
    w0 = pl.multiple_of(qb * Q_BLOCK, Q_BLOCK)
    kw = kwb[pl.ds(w0, wlen), :]
    vw = vwb[pl.ds(w0, wlen), :]
    n_far_tiles = wlen // Q_BLOCK - 2

    for g in range(NSA_GROUP):
        far = tab_ref[REL_BUCKETS - 1, g_heads[g]]
        s = _dot_nt(qs[g], ksb[...]) + far
        tile_id = lax.shift_right_logical(kpos, 7)
        b_same = jnp.concatenate([boff[g, 0]] * (seq // Q_BLOCK), axis=1)
        b_prev = jnp.concatenate([boff[g, 1]] * (seq // Q_BLOCK), axis=1)
        s = s + jnp.where(tile_id == qb, b_same, 0.0) + jnp.where(tile_id == qb - 1, b_prev, 0.0)
        e, den = _masked_softmax_unnorm(s, mask_s)
        o_s = _dot(e.astype(BF16), vsb[...]) / den
        bw = jnp.concatenate([jnp.zeros((Q_BLOCK, n_far_tiles * Q_BLOCK), F32), boff[g, 1], boff[g, 0]], axis=1)
        sw = _dot_nt(qs[g], kw) + far + bw
        ew, denw = _masked_softmax_unnorm(sw, mask_w)
        o_w = _dot(ew.astype(BF16), vw) / denw
        c0 = g * 3
        y = gates[:, c0:c0 + 1] * o_c[g] + gates[:, c0 + 1:c0 + 2] * o_s + gates[:, c0 + 2:c0 + 3] * o_w
        o_ref[:, g * HEAD_DIM:(g + 1) * HEAD_DIM] = y.astype(o_ref.dtype)


def _nsa_prompt(z, comp, rel_bias, n_batch, seq, cols):
    nqb = seq // Q_BLOCK
    nc = seq // CMP_BLOCK
    ncp = _round_up(nc, LANES)
    n8 = _round_up(nc, SUBLANES)
    kvc = cols["kv"] // HEAD_DIM
    winc = cols["win"] // HEAD_DIM
    kh = NSA_KV_HEADS
    gw = NSA_GROUP * HEAD_DIM

    def zspec(col_block_of_k):
        return pl.BlockSpec((seq, HEAD_DIM), lambda b, k, q: (b, col_block_of_k(k)))

    return pl.pallas_call(
        functools.partial(_nsa_prompt_kernel, seq=seq),
        grid=(n_batch, kh, nqb),
        in_specs=[
            pl.BlockSpec(memory_space=pltpu.SMEM),
            pl.BlockSpec((Q_BLOCK, gw), lambda b, k, q: (b * nqb + q, cols["q"] // gw + k)),
            zspec(lambda k: kvc + 2 * kh + k),
            zspec(lambda k: kvc + 3 * kh + k),
            zspec(lambda k: winc + k),
            zspec(lambda k: winc + kh + k),
            pl.BlockSpec((Q_BLOCK, LANES), lambda b, k, q: (b * nqb + q, cols["ng"] // LANES + k)),
            pl.BlockSpec((nc, HEAD_DIM), lambda b, k, q: (b, k)),
            pl.BlockSpec((nc, HEAD_DIM), lambda b, k, q: (b, kh + k)),
        ],
        out_specs=pl.BlockSpec((Q_BLOCK, gw), lambda b, k, q: (b * nqb + q, k)),
        out_shape=jax.ShapeDtypeStruct((n_batch * seq, NSA_WIDTH), F32),
        scratch_shapes=[
            pltpu.VMEM((seq, HEAD_DIM), BF16), pltpu.VMEM((seq, HEAD_DIM), BF16),
            pltpu.VMEM((seq + WINDOW, HEAD_DIM), BF16), pltpu.VMEM((seq + WINDOW, HEAD_DIM), BF16),
            pltpu.VMEM((ncp, HEAD_DIM), BF16), pltpu.VMEM((ncp, HEAD_DIM), BF16),
            pltpu.VMEM((ncp, seq), BF16),
            pltpu.VMEM((NSA_GROUP, 2, Q_BLOCK, Q_BLOCK), F32),
            pltpu.VMEM((n8, Q_BLOCK), F32),
        ],
        compiler_params=_cparams(("arbitrary", "arbitrary", "arbitrary")),
        name="nsa_prompt",
    )(rel_bias, z, z, z, z, z, z, comp, comp)


def _nsa_sample_kernel(pt_ref, tab_ref, q_ref, kvn_ref, wn_ref, gt_ref, win_hbm, pages_hbm, comp_hbm,
                       o_ref, wo0_ref, wo1_ref, wo2_ref, wo3_ref,
                       pbuf, cbuf, wbuf, ebuf, sct, psem, csem, wsem, *, n_seq, n_pages, page0, layer, tn):
    b = pl.program_id(0)
    slot = lax.rem(b, 2)
    past = n_pages * 2 * CMP_BLOCK
    ncs = 2 * n_pages
    ns = ncs + 1
    ncp = ebuf.shape[0]
    n8 = sct.shape[0]
    nkeys = past + LANES
    wo_refs = (wo0_ref, wo1_ref, wo2_ref, wo3_ref)
    wb = wbuf.shape[2]
    n_slc = 2 * NSA_KV_HEADS

    def block_of(pos):
        half, page = _divmod_const(pos, n_pages)
        return jnp.where(pos < ncs, 2 * page + half, pos)

    def page_copies(seq_idx, dst_slot, j):
        pid = pt_ref[seq_idx * n_pages + j]
        slc = tuple(
            pltpu.make_async_copy(pages_hbm.at[page0 + pid, :, n_slc + i, :], pbuf.at[dst_slot, i, j],
                                  psem.at[dst_slot])
            for i in range(n_slc))
        return slc + (pltpu.make_async_copy(comp_hbm.at[pid], cbuf.at[dst_slot, j], csem.at[dst_slot]),)

    def window_copies(seq_idx, dst_slot):
        return tuple(
            pltpu.make_async_copy(win_hbm.at[layer, seq_idx, :, i, :], wbuf.at[dst_slot, i], wsem.at[dst_slot])
            for i in range(n_slc))

    def start_seq(seq_idx, dst_slot):
        def body(j, c):
            for cp in page_copies(seq_idx, dst_slot, j):
                cp.start()
            return c
        lax.fori_loop(0, n_pages, body, 0)
        for cp in window_copies(seq_idx, dst_slot):
            cp.start()

    def wait_seq(seq_idx, dst_slot):
        def body(j, c):
            for cp in page_copies(seq_idx, dst_slot, j):
                cp.wait()
            return c
        lax.fori_loop(0, n_pages, body, 0)
        for cp in window_copies(seq_idx, dst_slot):
            cp.wait()

    @pl.when(b == 0)
    def _():
        start_seq(0, 0)
        pos = lax.broadcasted_iota(jnp.int32, ebuf.shape, 0)
        key = lax.broadcasted_iota(jnp.int32, ebuf.shape, 1)
        ebuf[...] = ((lax.shift_right_logical(key, 6) == block_of(pos)) & (pos <= ncs)).astype(BF16)

    @pl.when(b + 1 < n_seq)
    def _():
        start_seq(b + 1, 1 - slot)

    wait_seq(b, slot)

    t_col = lax.broadcasted_iota(jnp.int32, (tn, 1), 0)
    q_pos = past + t_col
    scale = HEAD_DIM ** -0.5
    rows = NSA_GROUP * tn

    def rep_g(x):
        return jnp.concatenate([x] * NSA_GROUP, axis=0)

    def pad_rows(x, n):
        if x.shape[0] == n:
            return x
        return jnp.concatenate([x, jnp.zeros((n - x.shape[0], x.shape[1]), x.dtype)], axis=0)

    pos1 = lax.broadcasted_iota(jnp.int32, (tn, ncp), 1)
    blk1 = block_of(pos1)
    dist_c = q_pos - (blk1 * CMP_BLOCK + (CMP_BLOCK - 1))
    mask_c = rep_g((dist_c >= 0) & (pos1 < ncs))
    cur = lax.shift_right_logical(q_pos, 6)
    eligible = (blk1 * CMP_BLOCK <= q_pos) & (pos1 <= ncs)
    forced = (blk1 == 0) | (blk1 == cur) | (blk1 == cur - 1)

    qk, o_cmp, scores = [], [], []
    for k in range(NSA_KV_HEADS):
        heads = [k * NSA_GROUP + g for g in range(NSA_GROUP)]
        q = jnp.concatenate(
            [q_ref[:, (k * NSA_GROUP + g) * HEAD_DIM:(k * NSA_GROUP + g + 1) * HEAD_DIM] for g in range(NSA_GROUP)],
            axis=0)
        q = (q * scale).astype(BF16)
        qk.append(q)
        kc = jnp.concatenate([cbuf[slot, :, k, :], cbuf[slot, :, 2 * NSA_KV_HEADS + k, :]], axis=0)
        vc = jnp.concatenate([cbuf[slot, :, NSA_KV_HEADS + k, :], cbuf[slot, :, 3 * NSA_KV_HEADS + k, :]], axis=0)
        kc = pad_rows(kc, ncp).astype(BF16)
        vc = pad_rows(vc, ncp).astype(BF16)
        bias_c = jnp.concatenate(_t5_bias(jnp.maximum(dist_c, 0), tab_ref, heads), axis=0)
        s = _dot_nt(q, kc) + bias_c
        e, den = _masked_softmax_unnorm(s, mask_c)
        p = e / den
        o_cmp.append(_dot(p.astype(BF16), vc))
        imp = p[0:tn]
        for g in range(1, NSA_GROUP):
            imp = imp + p[g * tn:(g + 1) * tn]
        sc = jnp.where(eligible, imp + jnp.where(forced, FORCE_SCORE, 0.0), -1.0)
        scores.append(jnp.where(pos1 <= ncs, sc, -2.0))

    score = pad_rows(jnp.concatenate(scores, axis=0), LANES)
    score_t = jnp.concatenate([score[:, c * LANES:(c + 1) * LANES].T for c in range(ncp // LANES)], axis=0)
    sct[...] = score_t[0:n8, :]
    posc = lax.broadcasted_iota(jnp.int32, (n8, LANES), 0)
    sel_t = _select_blocks(sct, ns, block_of(posc), block_of)
    if n8 < ncp:
        sel_t = jnp.concatenate([sel_t, jnp.zeros((ncp - n8, LANES), F32)], axis=0)
    sel = jnp.concatenate([sel_t[c * LANES:(c + 1) * LANES, :].T for c in range(ncp // LANES)], axis=1)

    kpos = lax.broadcasted_iota(jnp.int32, (tn, nkeys), 1)
    dist_s = q_pos - kpos
    causal_s = rep_g((dist_s >= 0) & (kpos < past + tn))
    near = 2 * LANES
    dist_near = jnp.maximum(dist_s[:, nkeys - near:], 0)
    jw = lax.broadcasted_iota(jnp.int32, (tn, wb + LANES), 1)
    dist_w = wb + t_col - jw
    mask_w = rep_g((dist_w >= 0) & (dist_w <= WINDOW) & (past - wb + jw >= 0) & (jw < wb + tn))
    dist_w_near = jnp.maximum(dist_w[:, wb + LANES - near:], 0)
    gates = jax.nn.sigmoid(gt_ref[...])

    for k in range(NSA_KV_HEADS):
        heads = [k * NSA_GROUP + g for g in range(NSA_GROUP)]
        far = jnp.concatenate([jnp.full((tn, 1), tab_ref[REL_BUCKETS - 1, h], F32) for h in heads], axis=0)
        q = qk[k]
        k_new = pad_rows(kvn_ref[:, (2 * NSA_KV_HEADS + k) * HEAD_DIM:(2 * NSA_KV_HEADS + k + 1) * HEAD_DIM], LANES)
        v_new = pad_rows(kvn_ref[:, (3 * NSA_KV_HEADS + k) * HEAD_DIM:(3 * NSA_KV_HEADS + k + 1) * HEAD_DIM], LANES)
        k_all = jnp.concatenate([pbuf[slot, k].reshape(past, HEAD_DIM), k_new], axis=0).astype(BF16)
        v_all = jnp.concatenate([pbuf[slot, NSA_KV_HEADS + k].reshape(past, HEAD_DIM), v_new], axis=0).astype(BF16)
        sel_k = rep_g(sel[k * tn:(k + 1) * tn, :]).astype(BF16)
        mask_s = (_dot(sel_k, ebuf[...]) > 0.5) & causal_s
        b_near = jnp.concatenate(_t5_bias(dist_near, tab_ref, heads), axis=0) - far
        bias_s = jnp.concatenate([jnp.zeros((rows, nkeys - near), F32), b_near], axis=1)
        s = _dot_nt(q, k_all) + far + bias_s
        e, den = _masked_softmax_unnorm(s, mask_s)
        o_s = _dot(e.astype(BF16), v_all) / den
        kw = jnp.concatenate([wbuf[slot, k],
                              pad_rows(wn_ref[:, k * HEAD_DIM:(k + 1) * HEAD_DIM], LANES)], axis=0).astype(BF16)
        vw = jnp.concatenate(
            [wbuf[slot, NSA_KV_HEADS + k],
             pad_rows(wn_ref[:, (NSA_KV_HEADS + k) * HEAD_DIM:(NSA_KV_HEADS + k + 1) * HEAD_DIM], LANES)],
            axis=0).astype(BF16)
        bw_near = jnp.concatenate(_t5_bias(dist_w_near, tab_ref, heads), axis=0) - far
        bias_w = jnp.concatenate([jnp.zeros((rows, wb + LANES - near), F32), bw_near], axis=1)
        sw = _dot_nt(q, kw) + far + bias_w
        ew, denw = _masked_softmax_unnorm(sw, mask_w)
        o_w = _dot(ew.astype(BF16), vw) / denw
        for g in range(NSA_GROUP):
            c0 = k * LANES + g * 3
            r = slice(g * tn, (g + 1) * tn)
            y = (gates[:, c0:c0 + 1] * o_cmp[k][r] + gates[:, c0 + 1:c0 + 2] * o_s[r]
                 + gates[:, c0 + 2:c0 + 3] * o_w[r])
            h = k * NSA_GROUP + g
            o_ref[:, h * HEAD_DIM:(h + 1) * HEAD_DIM] = y.astype(o_ref.dtype)

    for i in range(n_slc):
        wo_refs[i][0, 0:wb - tn, :] = wbuf[slot, i, tn:wb, :]
        wo_refs[i][0, wb - tn:wb, :] = wn_ref[:, i * HEAD_DIM:(i + 1) * HEAD_DIM]


def _nsa_sample(z, row0, n_seq, tn, pages, page0, comp_pages, cache_win_l, layer, page_table, rel_bias, cols):
    n_pages = page_table.shape[1]
    page_size = pages.shape[1]
    assert page_size == 2 * CMP_BLOCK and tn == SUBLANES and row0 % tn == 0
    past = n_pages * page_size
    ncs = 2 * n_pages
    ncp = _round_up(ncs + 1, LANES)
    n8 = _round_up(ncs + 1, SUBLANES)
    wb = cache_win_l.shape[2]
    assert wb == WINDOW and past >= WINDOW
    half_cols = 2 * NSA_KV_HEADS * HEAD_DIM
    n_slc = 2 * NSA_KV_HEADS
    rb = row0 // tn
    grid_spec = pltpu.PrefetchScalarGridSpec(
        num_scalar_prefetch=1,
        grid=(n_seq,),
        in_specs=[
            pl.BlockSpec(memory_space=pltpu.SMEM),
            pl.BlockSpec((tn, NSA_WIDTH), lambda b, pt: (rb + b, cols["q"] // NSA_WIDTH)),
            pl.BlockSpec((tn, 2 * half_cols), lambda b, pt: (rb + b, cols["kv"] // (2 * half_cols))),
            pl.BlockSpec((tn, half_cols), lambda b, pt: (rb + b, cols["win"] // half_cols)),
            pl.BlockSpec((tn, 2 * LANES), lambda b, pt: (rb + b, cols["ng"] // (2 * LANES))),
            pl.BlockSpec(memory_space=pl.ANY),
            pl.BlockSpec(memory_space=pl.ANY),
            pl.BlockSpec(memory_space=pl.ANY),
        ],
        out_specs=[pl.BlockSpec((tn, NSA_WIDTH), lambda b, pt: (b, 0))]
        + [pl.BlockSpec((1, wb, HEAD_DIM), lambda b, pt: (b, 0, 0)) for _ in range(n_slc)],
        scratch_shapes=[
            pltpu.VMEM((2, n_slc, n_pages, page_size, HEAD_DIM), F32),
            pltpu.VMEM((2, n_pages, 2 * 2 * NSA_KV_HEADS, HEAD_DIM), F32),
            pltpu.VMEM((2, n_slc, wb, HEAD_DIM), F32),
            pltpu.VMEM((ncp, past + LANES), BF16),
            pltpu.VMEM((n8, LANES), F32),
            pltpu.SemaphoreType.DMA((2,)),
            pltpu.SemaphoreType.DMA((2,)),
            pltpu.SemaphoreType.DMA((2,)),
        ],
    )
    return pl.pallas_call(
        functools.partial(_nsa_sample_kernel, n_seq=n_seq, n_pages=n_pages, page0=page0, layer=layer, tn=tn),
        grid_spec=grid_spec,
        out_shape=[jax.ShapeDtypeStruct((n_seq * tn, NSA_WIDTH), F32)]
        + [jax.ShapeDtypeStruct((n_seq, wb, HEAD_DIM), F32) for _ in range(n_slc)],
        compiler_params=_cparams(("arbitrary",)),
        name="nsa_sample",
    )(page_table.reshape(-1), rel_bias, z, z, z, z, cache_win_l, pages, comp_pages)


def _cumsum8(x):
    row = lax.broadcasted_iota(jnp.int32, x.shape, 0)
    for sh in (1, 2, 4):
        x = x + jnp.where(row >= sh, pltpu.roll(x, sh, 0), 0.0)
    return x


def _mixer_kernel(zcb_ref, zcc_ref, zch_ref, zhg_ref, cw_ref, cs_ref, hs_ref, lb_ref, hn_ref,
                  yb_ref, yc_ref, cso_ref, hso_ref, uprev, st, qf_s, k_s, lf_s, *, layer, tt):
    ti = pl.program_id(1)
    nt = pl.num_programs(1)

    @pl.when(ti == 0)
    def _():
        uprev[...] = jnp.zeros(uprev.shape, F32)
        uprev[SUBLANES - (CONV_K - 1):SUBLANES, :] = cs_ref[0]
        for h in range(HG_HEADS):
            st[h] = hs_ref[0, h].T

    cb = zcb_ref[...]
    u = zcc_ref[...] * zch_ref[...]
    ext = jnp.concatenate([uprev[...], u], axis=0)
    y = cw_ref[CONV_K - 1:CONV_K, :] * u
    for j in range(CONV_K - 1):
        shift = CONV_K - 1 - j
        y = y + cw_ref[j:j + 1, :] * pltpu.roll(ext, shift, 0)[SUBLANES:, :]
    yb_ref[...] = (cb * y).astype(yb_ref.dtype)
    tail = ext[tt:tt + SUBLANES, :]
    uprev[...] = tail

    @pl.when(ti == nt - 1)
    def _():
        cso_ref[0] = pltpu.roll(tail, CONV_K - 1, 0)[0:CONV_K - 1, :]

    hw = HG_HEADS * HG_DK
    lg = lb_ref[...]
    mx = jnp.max(lg, axis=0, keepdims=True)
    ex = jnp.exp(lg - mx)
    sm = ex / jnp.sum(ex, axis=0, keepdims=True)
    cs = sm[0:1]
    for i in range(1, layer + 1):
        cs = cs + sm[i:i + 1]
    lb = cs - sm[0:1]
    fx = zhg_ref[:, hw:2 * hw]
    lb_pos = lb > 0.0
    log_lb = jnp.log(jnp.where(lb_pos, lb, 1.0))
    log_sig = jnp.minimum(fx, 0.0) - jnp.log1p(jnp.exp(-jnp.abs(fx)))
    a2 = jnp.log1p(-lb) + log_sig
    lae = jnp.maximum(log_lb, a2) + jnp.log1p(jnp.exp(-jnp.abs(log_lb - a2)))
    lf_s[...] = jnp.where(lb_pos, lae, log_sig)
    k_s[...] = (1.0 - lb) * jax.nn.sigmoid(-fx)
    hq = zhg_ref[:, 0:hw]
    qf_s[...] = hq * jax.nn.sigmoid(hq)
    norm_g = hn_ref[...]
    ts = lax.broadcasted_iota(jnp.int32, (HG_BLOCK, HG_BLOCK, 1), 0)
    ss = lax.broadcasted_iota(jnp.int32, (HG_BLOCK, HG_BLOCK, 1), 1)
    causal = ss <= ts

    def block(i, c):
        r0 = pl.multiple_of(i * HG_BLOCK, HG_BLOCK)
        rs = pl.ds(r0, HG_BLOCK)
        for h in range(HG_HEADS):
            kc = slice(h * HG_DK, (h + 1) * HG_DK)
            vcol = slice(2 * hw + h * HG_DV, 2 * hw + (h + 1) * HG_DV)
            gcol = slice(2 * hw + HG_HEADS * HG_DV + h * HG_DV, 2 * hw + HG_HEADS * HG_DV + (h + 1) * HG_DV)
            qf = qf_s[rs, kc]
            kk = k_s[rs, kc]
            g = _cumsum8(lf_s[rs, kc])
            v = zhg_ref[rs, vcol]
            gl = g[HG_BLOCK - 1:HG_BLOCK, :]
            st_h = st[h]
            o = _dot_nt((qf * jnp.exp(g)).astype(BF16), st_h.astype(BF16))
            diff = jnp.where(causal, g[:, None, :] - g[None, :, :], 0.0)
            x3 = jnp.where(causal, qf[:, None, :] * kk[None, :, :] * jnp.exp(diff), 0.0)
            a = jnp.sum(x3, axis=-1, keepdims=True)
            o = o + jnp.sum(a * v[None, :, :], axis=1)
            kd = kk * jnp.exp(gl - g)
            st[h] = st_h * jnp.exp(gl) + _dot_tn(v.astype(BF16), kd.astype(BF16))
            on = o * lax.rsqrt(jnp.mean(o * o, axis=-1, keepdims=True) + NORM_EPS) * norm_g
            gx = zhg_ref[rs, gcol]
            yc_ref[rs, h * HG_DV:(h + 1) * HG_DV] = (on * (gx * jax.nn.sigmoid(gx))).astype(yc_ref.dtype)
        return c

    lax.fori_loop(0, tt // HG_BLOCK, block, 0)

    @pl.when(ti == nt - 1)
    def _():
        for h in range(HG_HEADS):
            hso_ref[0, h] = st[h].T


def _mixer(z, row0, n_seq, seq, conv_w_l, conv_state, hg_state, lb_logits, hg_norm_l, layer, cols):
    cw = conv_w_l.shape[-1]
    hgw = 2 * HG_HEADS * HG_DK + 2 * HG_HEADS * HG_DV
    tt = _pick_tile(seq, 256, SUBLANES)
    nt = seq // tt
    rb = row0 // tt
    assert row0 % tt == 0 and cols["cv"] % cw == 0 and cols["hg"] % hgw == 0
    depth = lb_logits.shape[0]
    cvb = cols["cv"] // cw
    return pl.pallas_call(
        functools.partial(_mixer_kernel, layer=layer, tt=tt),
        grid=(n_seq, nt),
        in_specs=[
            pl.BlockSpec((tt, cw), lambda b, t: (rb + b * nt + t, cvb)),
            pl.BlockSpec((tt, cw), lambda b, t: (rb + b * nt + t, cvb + 1)),
            pl.BlockSpec((tt, cw), lambda b, t: (rb + b * nt + t, cvb + 2)),
            pl.BlockSpec((tt, hgw), lambda b, t: (rb + b * nt + t, cols["hg"] // hgw)),
            pl.BlockSpec((CONV_K, cw), lambda b, t: (0, 0)),
            pl.BlockSpec((1, CONV_K - 1, cw), lambda b, t: (b, 0, 0)),
            pl.BlockSpec((1, HG_HEADS, HG_DK, HG_DV), lambda b, t: (b, 0, 0, 0)),
            pl.BlockSpec((depth, HG_HEADS * HG_DK), lambda b, t: (0, 0)),
            pl.BlockSpec((1, HG_DV), lambda b, t: (0, 0)),
        ],
        out_specs=[
            pl.BlockSpec((tt, cw), lambda b, t: (b * nt + t, 0)),
            pl.BlockSpec((tt, HG_HEADS * HG_DV), lambda b, t: (b * nt + t, 0)),
            pl.BlockSpec((1, CONV_K - 1, cw), lambda b, t: (b, 0, 0)),
            pl.BlockSpec((1, HG_HEADS, HG_DK, HG_DV), lambda b, t: (b, 0, 0, 0)),
        ],
        out_shape=[
            jax.ShapeDtypeStruct((n_seq * seq, cw), F32),
            jax.ShapeDtypeStruct((n_seq * seq, HG_HEADS * HG_DV), F32),
            jax.ShapeDtypeStruct((n_seq, CONV_K - 1, cw), F32),
            jax.ShapeDtypeStruct((n_seq, HG_HEADS, HG_DK, HG_DV), F32),
        ],
        scratch_shapes=[
            pltpu.VMEM((SUBLANES, cw), F32),
            pltpu.VMEM((HG_HEADS, HG_DV, HG_DK), F32),
            pltpu.VMEM((tt, HG_HEADS * HG_DK), F32),
            pltpu.VMEM((tt, HG_HEADS * HG_DK), F32),
            pltpu.VMEM((tt, HG_HEADS * HG_DK), F32),
        ],
        compiler_params=_cparams(("arbitrary", "arbitrary")),
        name="mixer_conv_hgrn",
    )(z, z, z, z, conv_w_l, conv_state, hg_state, lb_logits, hg_norm_l.reshape(1, HG_DV))


def _ep_identity(prods, extras):
    return prods[0]


def _ep_mix(prods, extras):
    ga, gb, gc = (jax.nn.sigmoid(e) for e in extras)
    return ga * prods[0] + gb * prods[1] + gc * prods[2]


def _ep_residual(prods, extras):
    return extras[0] + prods[0]


def _ep_swiglu(prods, extras):
    return prods[0] * jax.nn.sigmoid(prods[0]) * prods[1]


def _ep_ple(prods, extras):
    return extras[0] + jax.nn.sigmoid(prods[0]) * prods[1]


def kernel(x_prompt, x_sample, p_prompt, p_sample, cache_kv, cache_win, state_conv, state_hgrn, page_table,
           rel_bias, g_mix, w_in, phi_pe, phi_w1, phi_w2, conv_w, hg_lb_logits, hg_norm, w_pa, w_pb, w_pc, w_o,
           g_ffn, w_gate, w_up, w_down, w_pg, w_pp, g_final):
    bp, seq, d = x_prompt.shape
    bs, tn, _ = x_sample.shape
    depth = w_in.shape[0]
    n_in = w_in.shape[-1]
    cw = conv_w.shape[-1]
    d_ff = w_gate.shape[-1]
    ple = w_pp.shape[1]
    n_pool, page_size = cache_kv.shape[1], cache_kv.shape[2]
    mp, ms = bp * seq, bs * tn
    m = mp + ms
    kvw = 6 * NSA_KV_HEADS * HEAD_DIM
    hgw = 2 * HG_HEADS * HG_DK + 2 * HG_HEADS * HG_DV
    assert n_in == NSA_WIDTH + kvw + N_GATES + 3 * cw + hgw + 3 * d
    assert seq % Q_BLOCK == 0 and seq >= WINDOW

    o_ng = NSA_WIDTH + kvw
    cols = {"q": 0, "kv": NSA_WIDTH, "win": NSA_WIDTH + N_KV_ROWS * NSA_KV_HEADS * HEAD_DIM,
            "cv": o_ng, "hg": o_ng + 3 * cw, "mg": o_ng + 3 * cw + hgw, "ng": n_in - N_GATES}
    nz = cols["ng"] + GATE_PAD
    gk = N_GATES // NSA_KV_HEADS

    def relayout_w_in(w):
        gate_cols = w[:, o_ng:o_ng + N_GATES]
        parts = [w[:, :o_ng], w[:, o_ng + N_GATES:]]
        for k in range(NSA_KV_HEADS):
            parts += [gate_cols[:, k * gk:(k + 1) * gk], jnp.zeros((d, LANES - gk), w.dtype)]
        parts.append(jnp.zeros((d, GATE_PAD - NSA_KV_HEADS * LANES), w.dtype))
        return jnp.concatenate(parts, axis=1).astype(BF16)

    tm = _pick_tile(m, 1024, 256) if m % 256 == 0 else _pick_tile(m, 1024, SUBLANES)
    h = jnp.concatenate([x_prompt.reshape(mp, d), x_sample.reshape(ms, d)], axis=0)
    n_rows = N_KV_ROWS * NSA_KV_HEADS
    pages = cache_kv.reshape(depth * n_pool, page_size, n_rows, HEAD_DIM)
    pool_rows = cache_kv.reshape(depth * n_pool * page_size * n_rows, HEAD_DIM)
    nb_pool = n_pool * (page_size // CMP_BLOCK)
    cache_win_r = cache_win.reshape(depth, bs, cache_win.shape[2], 2 * NSA_KV_HEADS, HEAD_DIM)
    zeros_conv = jnp.zeros((bp, CONV_K - 1, cw), F32)
    zeros_hg = jnp.zeros((bp, HG_HEADS, HG_DK, HG_DV), F32)

    def w2d(w):
        return w.reshape(w.shape[0] * w.shape[1], w.shape[2])

    kv_p, kv_s, win_p, win_s, conv_p, conv_s, hg_p, hg_s = [], [], [], [], [], [], [], []
    for l in range(depth):
        xn = _rmsnorm(h, g_mix[l], BF16)
        z = _fused_matmul([xn], [0], [(relayout_w_in(w_in[l]), 0)], [], _ep_identity, nz, F32,
                          tm, _pick_tile(nz, 2560, 2 * LANES), "in_proj")

        comp_prompt = _compress(z, mp // CMP_BLOCK, cols["kv"] // HEAD_DIM, phi_pe[l], phi_w1[l], phi_w2[l],
                                _pick_tile(mp // CMP_BLOCK, 256, SUBLANES), "compress_prompt")
        cb_pool = _pick_tile(nb_pool, 64, SUBLANES)
        comp_pool = _compress_rows(pool_rows, l * nb_pool // cb_pool, nb_pool, n_rows, phi_pe[l],
                                   phi_w1[l].astype(BF16), phi_w2[l], cb_pool, "compress_pool")
        comp_pages = comp_pool.reshape(n_pool, (page_size // CMP_BLOCK) * 2 * NSA_KV_HEADS, HEAD_DIM)
        ya_p = _nsa_prompt(z, comp_prompt, rel_bias, bp, seq, cols)
        ya_s, *win_new = _nsa_sample(z, mp, bs, tn, pages, l * n_pool, comp_pages, cache_win_r, l, page_table,
                                     rel_bias, cols)
        yb_p, yc_p, cs_p, hs_p = _mixer(z, 0, bp, seq, conv_w[l], zeros_conv, zeros_hg, hg_lb_logits,
                                        hg_norm[l], l, cols)
        yb_s, yc_s, cs_s, hs_s = _mixer(z, mp, bs, tn, conv_w[l], state_conv[l], state_hgrn[l], hg_lb_logits,
                                        hg_norm[l], l, cols)
        ya = jnp.concatenate([ya_p, ya_s], axis=0)
        yb = jnp.concatenate([yb_p, yb_s], axis=0)
        yc = jnp.concatenate([yc_p, yc_s], axis=0)

        tn_d = _pick_tile(d, 1024, LANES)
        tn_mix = _pick_tile(d, 512, LANES)
        mgb = cols["mg"] // tn_mix
        mixed = _fused_matmul([ya, yb, yc], [0, 1, 2], [(w2d(w_pa), l), (w2d(w_pb), l), (w2d(w_pc), l)],
                              [(z, mgb), (z, mgb + d // tn_mix), (z, mgb + 2 * (d // tn_mix))],
                              _ep_mix, d, BF16, tm, tn_mix, "mix_proj")
        h = _fused_matmul([mixed], [0], [(w2d(w_o), l)], [(h, 0)], _ep_residual, d, F32, tm, tn_d, "out_proj")
        xf = _rmsnorm(h, g_ffn[l], BF16)
        act = _fused_matmul([xf], [0, 0], [(w2d(w_gate), l), (w2d(w_up), l)], [], _ep_swiglu, d_ff, BF16,
                            tm, _pick_tile(d_ff, 512, LANES), "ffn_up")
        h = _fused_matmul([act], [0], [(w_down[l].astype(BF16), 0)], [(h, 0)], _ep_residual, d, F32,
                          tm, _pick_tile(d, 512, LANES), "ffn_down")
        p_all = jnp.concatenate([p_prompt[l].reshape(mp, ple), p_sample[l].reshape(ms, ple)], axis=0)
        h = _fused_matmul([h, p_all], [0, 1], [(w2d(w_pg), l), (w2d(w_pp), l)], [(h, 0)], _ep_ple, d, F32,
                          tm, tn_d, "ple_gate")

        kvr = z[:, cols["kv"]:cols["win"]]
        kv_p.append(kvr[:mp].reshape(bp, seq, N_KV_ROWS, NSA_KV_HEADS, HEAD_DIM))
        kv_s.append(kvr[mp:].reshape(bs, tn, N_KV_ROWS, NSA_KV_HEADS, HEAD_DIM))
        keep = min(WINDOW, seq)
        zw = z[:mp, cols["win"]:cols["cv"]].reshape(bp, seq, 2, NSA_KV_HEADS, HEAD_DIM)
        win_p.append(zw[:, seq - keep:])
        win_s.append(jnp.stack(win_new, axis=2).reshape(bs, win_new[0].shape[1], 2, NSA_KV_HEADS, HEAD_DIM))
        conv_p.append(cs_p); conv_s.append(cs_s); hg_p.append(hs_p); hg_s.append(hs_s)

    y = _rmsnorm(h, g_final, F32)
    return (y[:mp].reshape(bp, seq, d), y[mp:].reshape(bs, tn, d), jnp.stack(kv_p), jnp.stack(kv_s),
            jnp.stack(win_p), jnp.stack(win_s), jnp.stack(conv_p), jnp.stack(conv_s),
            jnp.stack(hg_p), jnp.stack(hg_s))
```

```python
import functools
import math

import numpy as np
import jax
import jax.numpy as jnp
from jax import lax
from jax.experimental import pallas as pl
from jax.experimental.pallas import tpu as pltpu

F32 = jnp.float32
BF16 = jnp.bfloat16

LANES = 128
SUBLANES = 8
VMEM_LIMIT_BYTES = 56 * 1024 * 1024

NSA_HEADS = 8
NSA_KV_HEADS = 2
NSA_GROUP = NSA_HEADS // NSA_KV_HEADS
HEAD_DIM = 128
NSA_WIDTH = NSA_HEADS * HEAD_DIM
CMP_BLOCK = 64
N_SELECT = 16
WINDOW = 512
Q_BLOCK = 128
FAR_CHUNK = 512
FORCE_SCORE = 1.0e4
MASK_VALUE = -1.0e30
N_KV_ROWS = 4
CONV_K = 3
HG_HEADS = 4
HG_DK = 128
HG_DV = 128
HG_BLOCK = SUBLANES
REL_BUCKETS = 32
REL_MAX_DIST = 128
NORM_EPS = 1e-6
N_GATES = 3 * NSA_HEADS
GATE_PAD = 4 * LANES


def _round_up(x, m):
    return -(-x // m) * m


def _pick_tile(n, pref, unit):
    if n <= pref:
        return n
    best = None
    for t in range(unit, pref + 1, unit):
        if n % t == 0:
            best = t
    assert best is not None, (n, pref, unit)
    return best


def _divmod_const(x, n):
    if n & (n - 1) == 0:
        return lax.shift_right_logical(x, n.bit_length() - 1), x & (n - 1)
    return x // n, lax.rem(x, n)


def _t5_thresholds():
    n = np.arange(0, REL_MAX_DIST + 1)
    max_exact = REL_BUCKETS // 2
    nf = np.maximum(n, 1).astype(np.float32)
    ratio = np.log(nf / np.float32(max_exact)) / np.float32(math.log(REL_MAX_DIST / max_exact))
    large = max_exact + (ratio * np.float32(REL_BUCKETS - max_exact)).astype(np.int32)
    large = np.minimum(large, REL_BUCKETS - 1)
    bucket = np.where(n < max_exact, n, large)
    assert np.all(np.diff(bucket) >= 0) and bucket[-1] == REL_BUCKETS - 1
    return [int(np.argmax(bucket >= j)) for j in range(1, REL_BUCKETS)]


T5_THRESHOLDS = _t5_thresholds()


def _t5_bias(dist, tab_ref, heads):
    ge = [dist >= thr for thr in T5_THRESHOLDS]
    out = []
    for h in heads:
        b = jnp.full(dist.shape, tab_ref[0, h], F32)
        for j, m in enumerate(ge):
            b = jnp.where(m, tab_ref[j + 1, h], b)
        out.append(b)
    return out


def _dot(a, b):
    return jnp.dot(a, b, preferred_element_type=F32)


def _dot_nt(a, b):
    return lax.dot_general(a, b, (((1,), (1,)), ((), ())), preferred_element_type=F32)


def _dot_tn(a, b):
    return lax.dot_general(a, b, (((0,), (0,)), ((), ())), preferred_element_type=F32)


def _cparams(sem):
    return pltpu.CompilerParams(dimension_semantics=sem, vmem_limit_bytes=VMEM_LIMIT_BYTES)


def _rmsnorm_kernel(x_ref, g_ref, o_ref):
    x = x_ref[...]
    y = x * lax.rsqrt(jnp.mean(x * x, axis=-1, keepdims=True) + NORM_EPS)
    o_ref[...] = (y * g_ref[...]).astype(o_ref.dtype)


def _rmsnorm(x, g, out_dtype):
    m, d = x.shape
    tm = _pick_tile(m, 1024, SUBLANES)
    return pl.pallas_call(
        _rmsnorm_kernel,
        grid=(m // tm,),
        in_specs=[pl.BlockSpec((tm, d), lambda i: (i, 0)), pl.BlockSpec((1, d), lambda i: (0, 0))],
        out_specs=pl.BlockSpec((tm, d), lambda i: (i, 0)),
        out_shape=jax.ShapeDtypeStruct((m, d), out_dtype),
        compiler_params=_cparams(("parallel",)),
        name="rmsnorm",
    )(x, g.reshape(1, d))


def _mm_kernel(*refs, n_x, pairs, n_extra, epilogue, cast_w):
    xs = refs[:n_x]
    ws = refs[n_x:n_x + len(pairs)]
    extras = refs[n_x + len(pairs):n_x + len(pairs) + n_extra]
    o_ref = refs[n_x + len(pairs) + n_extra]
    wscr = refs[n_x + len(pairs) + n_extra + 1:]
    i = pl.program_id(1)

    if any(cast_w):
        @pl.when(i == 0)
        def _():
            s = 0
            for p in range(len(pairs)):
                if cast_w[p]:
                    wscr[s][...] = ws[p][...].astype(BF16)
                    s += 1

    xv = [x[...].astype(BF16) for x in xs]
    prods = []
    s = 0
    for p, xi in enumerate(pairs):
        if cast_w[p]:
            w = wscr[s][...]
            s += 1
        else:
            w = ws[p][...]
        prods.append(_dot(xv[xi], w))
    o_ref[...] = epilogue(prods, [e[...] for e in extras]).astype(o_ref.dtype)


def _fused_matmul(xs, pairs, ws, extras, epilogue, n_out, out_dtype, tm, tn, name):
    m = xs[0].shape[0]
    assert m % tm == 0 and n_out % tn == 0
    cast_w = tuple(w.dtype != BF16 for w, _ in ws)
    in_specs = []
    for x in xs:
        in_specs.append(pl.BlockSpec((tm, x.shape[1]), lambda j, i: (i, 0)))
    for p, (w, roff) in enumerate(ws):
        k = xs[pairs[p]].shape[1]
        in_specs.append(pl.BlockSpec((k, tn), functools.partial(lambda j, i, r: (r, j), r=roff)))
    for e, coff in extras:
        in_specs.append(pl.BlockSpec((tm, tn), functools.partial(lambda j, i, c: (i, c + j), c=coff)))
    scratch = [pltpu.VMEM((xs[pairs[p]].shape[1], tn), BF16) for p in range(len(ws)) if cast_w[p]]
    kern = functools.partial(_mm_kernel, n_x=len(xs), pairs=tuple(pairs), n_extra=len(extras),
                             epilogue=epilogue, cast_w=cast_w)
    return pl.pallas_call(
        kern,
        grid=(n_out // tn, m // tm),
        in_specs=in_specs,
        out_specs=pl.BlockSpec((tm, tn), lambda j, i: (i, j)),
        out_shape=jax.ShapeDtypeStruct((m, n_out), out_dtype),
        scratch_shapes=scratch,
        compiler_params=_cparams(("arbitrary", "arbitrary")),
        name=name,
    )(*xs, *[w for w, _ in ws], *[e for e, _ in extras])


def _compress_kernel(blk_ref, pe_ref, w1_ref, w2_ref, o_ref, x2d, w1b):
    @pl.when(pl.program_id(1) == 0)
    def _():
        w1b[...] = w1_ref[0].astype(BF16)

    cb = x2d.shape[0]
    for tau in range(CMP_BLOCK):
        x2d[:, tau * HEAD_DIM:(tau + 1) * HEAD_DIM] = (
            blk_ref[pl.ds(tau, cb, stride=CMP_BLOCK), :] + pe_ref[0, tau:tau + 1, :]).astype(BF16)
    h1 = _dot(x2d[...], w1b[...])
    h1 = h1 * jax.nn.sigmoid(h1)
    o_ref[...] = _dot(h1.astype(BF16), w2_ref[0].astype(BF16))


def _compress(tokens, n_blocks, col_block0, pe, w1, w2, cb, name):
    n_rk = 2 * NSA_KV_HEADS
    return pl.pallas_call(
        _compress_kernel,
        grid=(n_rk, n_blocks // cb),
        in_specs=[
            pl.BlockSpec((cb * CMP_BLOCK, HEAD_DIM), lambda rk, t: (t, col_block0 + rk)),
            pl.BlockSpec((1, CMP_BLOCK, HEAD_DIM), lambda rk, t: (rk // NSA_KV_HEADS, 0, 0)),
            pl.BlockSpec((1, CMP_BLOCK * HEAD_DIM, HEAD_DIM), lambda rk, t: (rk // NSA_KV_HEADS, 0, 0)),
            pl.BlockSpec((1, HEAD_DIM, HEAD_DIM), lambda rk, t: (rk // NSA_KV_HEADS, 0, 0)),
        ],
        out_specs=pl.BlockSpec((cb, HEAD_DIM), lambda rk, t: (t, rk)),
        out_shape=jax.ShapeDtypeStruct((n_blocks, n_rk * HEAD_DIM), F32),
        scratch_shapes=[pltpu.VMEM((cb, CMP_BLOCK * HEAD_DIM), BF16),
                        pltpu.VMEM((CMP_BLOCK * HEAD_DIM, HEAD_DIM), BF16)],
        compiler_params=_cparams(("arbitrary", "arbitrary")),
        name=name,
    )(tokens, pe, w1, w2)


def _compress_rows_kernel(tok_ref, pe_ref, w1_ref, w2_ref, o_ref, x2d, *, n_rows):
    cb = o_ref.shape[0]
    blk_stride = CMP_BLOCK * n_rows
    for r in range(2):
        for k in range(NSA_KV_HEADS):
            rk = r * NSA_KV_HEADS + k
            for tau in range(CMP_BLOCK):
                x2d[k * cb:(k + 1) * cb, tau * HEAD_DIM:(tau + 1) * HEAD_DIM] = (
                    tok_ref[pl.ds(tau * n_rows + rk, cb, stride=blk_stride), :]
                    + pe_ref[r, tau:tau + 1, :]).astype(BF16)
        h1 = _dot(x2d[...], w1_ref[r])
        h1 = h1 * jax.nn.sigmoid(h1)
        h2 = _dot(h1.astype(BF16), w2_ref[r].astype(BF16))
        for k in range(NSA_KV_HEADS):
            rk = r * NSA_KV_HEADS + k
            o_ref[:, rk * HEAD_DIM:(rk + 1) * HEAD_DIM] = h2[k * cb:(k + 1) * cb]


def _compress_rows(rows2d, row_block0, n_blocks, n_rows, pe, w1_bf16, w2, cb, name):
    n_rk = 2 * NSA_KV_HEADS
    blk = cb * CMP_BLOCK * n_rows
    return pl.pallas_call(
        functools.partial(_compress_rows_kernel, n_rows=n_rows),
        grid=(n_blocks // cb,),
        in_specs=[
            pl.BlockSpec((blk, HEAD_DIM), lambda t: (row_block0 + t, 0)),
            pl.BlockSpec((2, CMP_BLOCK, HEAD_DIM), lambda t: (0, 0, 0)),
            pl.BlockSpec((2, CMP_BLOCK * HEAD_DIM, HEAD_DIM), lambda t: (0, 0, 0)),
            pl.BlockSpec((2, HEAD_DIM, HEAD_DIM), lambda t: (0, 0, 0)),
        ],
        out_specs=pl.BlockSpec((cb, n_rk * HEAD_DIM), lambda t: (t, 0)),
        out_shape=jax.ShapeDtypeStruct((n_blocks, n_rk * HEAD_DIM), F32),
        scratch_shapes=[pltpu.VMEM((NSA_KV_HEADS * cb, CMP_BLOCK * HEAD_DIM), BF16)],
        compiler_params=_cparams(("arbitrary",)),
        name=name,
    )(rows2d, pe, w1_bf16, w2)


def _masked_softmax_unnorm(s, mask):
    s = jnp.where(mask, s, MASK_VALUE)
    m = jnp.max(s, axis=-1, keepdims=True)
    e = jnp.where(mask, jnp.exp(s - m), 0.0)
    den = jnp.maximum(jnp.sum(e, axis=-1, keepdims=True), 1e-30)
    return e, den


def _select_blocks(score_t_ref, n_blocks):
    sc = score_t_ref[...]
    idx = lax.broadcasted_iota(jnp.int32, sc.shape, 0)

    def body(i, cnt):
        row = score_t_ref[pl.ds(i, 1), :]
        return cnt + jnp.where(idx > i, (row >= sc).astype(F32), (row > sc).astype(F32))

    cnt = lax.fori_loop(0, n_blocks, body, jnp.zeros(sc.shape, F32), unroll=math.gcd(n_blocks, 8))
    return ((cnt < N_SELECT) & (sc >= 0.0)).astype(F32)


def _select_blocks_rows(score, blk_of_lane, n_pos, blk_of_pos):
    cnt = jnp.zeros(score.shape, F32)
    for p in range(n_pos):
        col = score[:, p:p + 1]
        cnt = cnt + jnp.where(blk_of_lane > blk_of_pos(p), (col >= score).astype(F32), (col > score).astype(F32))
    return ((cnt < N_SELECT) & (score >= 0.0)).astype(F32)


def _nsa_prompt_kernel(tab_ref, q_ref, ks_ref, vs_ref, kw_ref, vw_ref, gt_ref, kc_ref, vc_ref, _out_buf, o_ref,
                       ksb, vsb, kwb, vwb, kcb, vcb, ebuf, bnear, bcn, sct, sbuf, mrun, lrun, acc, *, seq):
    k = pl.program_id(1)
    qb = pl.program_id(2)
    nc = seq // CMP_BLOCK
    ncp = kcb.shape[0]
    n8 = sct.shape[0]
    g_heads = [k * NSA_GROUP + g for g in range(NSA_GROUP)]
    rows = NSA_GROUP * Q_BLOCK

    @pl.when(qb == 0)
    def _():
        ksb[0:Q_BLOCK, :] = jnp.zeros((Q_BLOCK, HEAD_DIM), BF16)
        ksb[Q_BLOCK:, :] = ks_ref[...].astype(BF16)
        vsb[0:Q_BLOCK, :] = jnp.zeros((Q_BLOCK, HEAD_DIM), BF16)
        vsb[Q_BLOCK:, :] = vs_ref[...].astype(BF16)
        kwb[0:WINDOW, :] = jnp.zeros((WINDOW, HEAD_DIM), BF16)
        kwb[WINDOW:, :] = kw_ref[...].astype(BF16)
        vwb[0:WINDOW, :] = jnp.zeros((WINDOW, HEAD_DIM), BF16)
        vwb[WINDOW:, :] = vw_ref[...].astype(BF16)
        kcb[...] = jnp.zeros(kcb.shape, BF16)
        vcb[...] = jnp.zeros(vcb.shape, BF16)
        kcb[0:nc, :] = kc_ref[...].astype(BF16)
        vcb[0:nc, :] = vc_ref[...].astype(BF16)
        blk = lax.broadcasted_iota(jnp.int32, ebuf.shape, 0)
        key = lax.broadcasted_iota(jnp.int32, ebuf.shape, 1) - Q_BLOCK
        ebuf[...] = ((key >= 0) & (lax.shift_right_logical(key, 6) == blk)).astype(BF16)
        ti = lax.broadcasted_iota(jnp.int32, (Q_BLOCK, Q_BLOCK), 0)
        tj = lax.broadcasted_iota(jnp.int32, (Q_BLOCK, Q_BLOCK), 1)
        b0 = _t5_bias(jnp.maximum(ti - tj, 0), tab_ref, g_heads)
        b1 = _t5_bias(Q_BLOCK + ti - tj, tab_ref, g_heads)
        tc = lax.broadcasted_iota(jnp.int32, (Q_BLOCK, ncp), 0)
        cc = lax.broadcasted_iota(jnp.int32, (Q_BLOCK, ncp), 1)
        dist_cn = jnp.where(cc < 4, tc + (CMP_BLOCK + 1) - CMP_BLOCK * cc, REL_MAX_DIST)
        bc = _t5_bias(jnp.maximum(dist_cn, 0), tab_ref, g_heads)
        for g in range(NSA_GROUP):
            far = tab_ref[REL_BUCKETS - 1, g_heads[g]]
            bnear[g * Q_BLOCK:(g + 1) * Q_BLOCK, 0:Q_BLOCK] = b1[g] - far
            bnear[g * Q_BLOCK:(g + 1) * Q_BLOCK, Q_BLOCK:] = b0[g] - far
            bcn[g * Q_BLOCK:(g + 1) * Q_BLOCK, :] = bc[g] - far

    def rep_g(x):
        return jnp.concatenate([x] * NSA_GROUP, axis=0)

    def fold_max(x):
        m = x[:, 0:LANES]
        for c in range(1, x.shape[1] // LANES):
            m = jnp.maximum(m, x[:, c * LANES:(c + 1) * LANES])
        return m

    def fold_sum(x):
        m = x[:, 0:LANES]
        for c in range(1, x.shape[1] // LANES):
            m = m + x[:, c * LANES:(c + 1) * LANES]
        return m

    t_col = lax.broadcasted_iota(jnp.int32, (Q_BLOCK, 1), 0)
    q_pos = qb * Q_BLOCK + t_col
    scale = HEAD_DIM ** -0.5
    q_all = jnp.concatenate(
        [(q_ref[:, g * HEAD_DIM:(g + 1) * HEAD_DIM] * scale).astype(BF16) for g in range(NSA_GROUP)], axis=0)
    gates = jax.nn.sigmoid(gt_ref[...])

    cidx = lax.broadcasted_iota(jnp.int32, (Q_BLOCK, ncp), 1)
    dist_c = q_pos - (cidx * CMP_BLOCK + (CMP_BLOCK - 1))
    mask_c = rep_g((dist_c >= 0) & (cidx < nc))
    bias_c = pltpu.roll(bcn[...], lax.rem(2 * qb - 2 + ncp, ncp), 1)
    e_c, den_c = _masked_softmax_unnorm(_dot_nt(q_all, kcb[...]) + bias_c, mask_c)
    p_c = e_c / den_c
    o_c = _dot(p_c.astype(BF16), vcb[...])
    imp = p_c[0:Q_BLOCK]
    for g in range(1, NSA_GROUP):
        imp = imp + p_c[g * Q_BLOCK:(g + 1) * Q_BLOCK]

    cur = lax.shift_right_logical(q_pos, 6)
    eligible = (cidx * CMP_BLOCK <= q_pos) & (cidx < nc)
    forced = (cidx == 0) | (cidx == cur) | (cidx == cur - 1)
    score = jnp.where(eligible, imp + jnp.where(forced, FORCE_SCORE, 0.0), -1.0)
    score = jnp.where(cidx < nc, score, -2.0)
    score_t = jnp.concatenate([score[:, c * LANES:(c + 1) * LANES].T for c in range(ncp // LANES)], axis=0)
    sct[...] = score_t[0:n8, :]
    sel_t = _select_blocks(sct, nc)
    if n8 < ncp:
        sel_t = jnp.concatenate([sel_t, jnp.zeros((ncp - n8, Q_BLOCK), F32)], axis=0)
    sel = jnp.concatenate([sel_t[c * LANES:(c + 1) * LANES, :].T for c in range(ncp // LANES)], axis=1)
    sel = sel.astype(BF16)

    near = 2 * Q_BLOCK
    n0 = pl.multiple_of(qb * Q_BLOCK, Q_BLOCK)
    jn = lax.broadcasted_iota(jnp.int32, (Q_BLOCK, near), 1)
    ok_near = (_dot(sel, ebuf[:, pl.ds(n0, near)]) > 0.5) & (jn - Q_BLOCK <= t_col)
    mb_near = rep_g(jnp.where(ok_near, 0.0, MASK_VALUE))
    s_near = _dot_nt(q_all, ksb[pl.ds(n0, near), :]) + bnear[...] + mb_near
    far_end = (qb - 1) * Q_BLOCK
    n_chunks = _divmod_const(jnp.maximum(far_end + (FAR_CHUNK - 1), 0), FAR_CHUNK)[0]
    mrun[...] = jnp.full(mrun.shape, MASK_VALUE, F32)

    def pass1(c, carry):
        c0 = pl.multiple_of(c * FAR_CHUNK, FAR_CHUNK)
        p0 = pl.multiple_of(c * FAR_CHUNK + Q_BLOCK, Q_BLOCK)
        pos = c * FAR_CHUNK + lax.broadcasted_iota(jnp.int32, (Q_BLOCK, FAR_CHUNK), 1)
        ok = (_dot(sel, ebuf[:, pl.ds(p0, FAR_CHUNK)]) > 0.5) & (pos < far_end)
        mb = jnp.where(ok, 0.0, MASK_VALUE)
        s = _dot_nt(q_all, ksb[pl.ds(p0, FAR_CHUNK), :])
        for g in range(NSA_GROUP):
            r = slice(g * Q_BLOCK, (g + 1) * Q_BLOCK)
            sg = s[r] + mb
            sbuf[r, pl.ds(c0, FAR_CHUNK)] = sg
            mrun[r, :] = jnp.maximum(mrun[r, :], fold_max(sg))
        return carry

    lax.fori_loop(0, n_chunks, pass1, 0)
    m_s = jnp.maximum(jnp.max(mrun[...], axis=-1, keepdims=True), jnp.max(s_near, axis=-1, keepdims=True))
    e_near = jnp.exp(s_near - m_s)
    mrun[...] = jnp.broadcast_to(m_s, mrun.shape)
    lrun[...] = fold_sum(e_near)
    acc[...] = _dot(e_near.astype(BF16), vsb[pl.ds(n0, near), :])

    def pass2(c, carry):
        c0 = pl.multiple_of(c * FAR_CHUNK, FAR_CHUNK)
        p0 = pl.multiple_of(c * FAR_CHUNK + Q_BLOCK, Q_BLOCK)
        m_full = mrun[...]
        e = jnp.exp(sbuf[:, pl.ds(c0, FAR_CHUNK)] - jnp.concatenate([m_full] * (FAR_CHUNK // LANES), axis=1))
        lrun[...] = lrun[...] + fold_sum(e)
        acc[...] = acc[...] + _dot(e.astype(BF16), vsb[pl.ds(p0, FAR_CHUNK), :])
        return carry

    lax.fori_loop(0, n_chunks, pass2, 0)
    o_s = acc[...] / jnp.sum(lrun[...], axis=-1, keepdims=True)

    wlen = WINDOW + Q_BLOCK
    jw = lax.broadcasted_iota(jnp.int32, (Q_BLOCK, wlen), 1)
    dist_w = WINDOW + t_col - jw
    ok_w = (dist_w >= 0) & (dist_w <= WINDOW) & (qb * Q_BLOCK - WINDOW + jw >= 0)
    mb_w = rep_g(jnp.where(ok_w, 0.0, MASK_VALUE))
    bias_w = jnp.concatenate([jnp.zeros((rows, wlen - near), F32), bnear[...]], axis=1)
    s_w = _dot_nt(q_all, kwb[pl.ds(n0, wlen), :]) + bias_w + mb_w
    e_w = jnp.exp(s_w - jnp.max(s_w, axis=-1, keepdims=True))
    o_w = _dot(e_w.astype(BF16), vwb[pl.ds(n0, wlen), :]) / jnp.sum(e_w, axis=-1, keepdims=True)

    for g in range(NSA_GROUP):
        r = slice(g * Q_BLOCK, (g + 1) * Q_BLOCK)
        c0 = g * 3
        y = gates[:, c0:c0 + 1] * o_c[r] + gates[:, c0 + 1:c0 + 2] * o_s[r] + gates[:, c0 + 2:c0 + 3] * o_w[r]
        o_ref[:, g * HEAD_DIM:(g + 1) * HEAD_DIM] = y.astype(o_ref.dtype)


def _nsa_prompt(z, comp, rel_bias, n_batch, seq, cols, out_buf):
    nqb = seq // Q_BLOCK
    nc = seq // CMP_BLOCK
    ncp = _round_up(nc, LANES)
    n8 = _round_up(nc, SUBLANES)
    kvc = cols["kv"] // HEAD_DIM
    winc = cols["win"] // HEAD_DIM
    kh = NSA_KV_HEADS
    gw = NSA_GROUP * HEAD_DIM
    rows = NSA_GROUP * Q_BLOCK
    assert seq % FAR_CHUNK == 0

    def zspec(col_block_of_k):
        return pl.BlockSpec((seq, HEAD_DIM), lambda b, k, q: (b, col_block_of_k(k)))

    return pl.pallas_call(
        functools.partial(_nsa_prompt_kernel, seq=seq),
        grid=(n_batch, kh, nqb),
        in_specs=[
            pl.BlockSpec(memory_space=pltpu.SMEM),
            pl.BlockSpec((Q_BLOCK, gw), lambda b, k, q: (b * nqb + q, cols["q"] // gw + k)),
            zspec(lambda k: kvc + 2 * kh + k),
            zspec(lambda k: kvc + 3 * kh + k),
            zspec(lambda k: winc + k),
            zspec(lambda k: winc + kh + k),
            pl.BlockSpec((Q_BLOCK, LANES), lambda b, k, q: (b * nqb + q, cols["ng"] // LANES + k)),
            pl.BlockSpec((nc, HEAD_DIM), lambda b, k, q: (b, k)),
            pl.BlockSpec((nc, HEAD_DIM), lambda b, k, q: (b, kh + k)),
            pl.BlockSpec(memory_space=pl.ANY),
        ],
        out_specs=pl.BlockSpec((Q_BLOCK, gw), lambda b, k, q: (b * nqb + q, k)),
        out_shape=jax.ShapeDtypeStruct(out_buf.shape, out_buf.dtype),
        input_output_aliases={9: 0},
        scratch_shapes=[
            pltpu.VMEM((seq + Q_BLOCK, HEAD_DIM), BF16), pltpu.VMEM((seq + Q_BLOCK, HEAD_DIM), BF16),
            pltpu.VMEM((seq + WINDOW, HEAD_DIM), BF16), pltpu.VMEM((seq + WINDOW, HEAD_DIM), BF16),
            pltpu.VMEM((ncp, HEAD_DIM), BF16), pltpu.VMEM((ncp, HEAD_DIM), BF16),
            pltpu.VMEM((ncp, seq + Q_BLOCK), BF16),
            pltpu.VMEM((rows, 2 * Q_BLOCK), F32),
            pltpu.VMEM((rows, ncp), F32),
            pltpu.VMEM((n8, Q_BLOCK), F32),
            pltpu.VMEM((rows, seq), F32),
            pltpu.VMEM((rows, LANES), F32), pltpu.VMEM((rows, LANES), F32), pltpu.VMEM((rows, LANES), F32),
        ],
        compiler_params=_cparams(("arbitrary", "arbitrary", "arbitrary")),
        name="nsa_prompt",
    )(rel_bias, z, z, z, z, z, z, comp, comp, out_buf)


def _nsa_sample_kernel(pt_ref, tab_ref, q_ref, kvn_ref, wn_ref, gt_ref, win_hbm, pages_hbm, comp_hbm, _out_buf,
                       o_ref, wo0_ref, wo1_ref, wo2_ref, wo3_ref,
                       pbuf, cbuf, wbuf, ebuf, psem, csem, wsem, *, n_seq, n_pages, page0, layer, tn):
    b = pl.program_id(0)
    slot = lax.rem(b, 2)
    past = n_pages * 2 * CMP_BLOCK
    ncs = 2 * n_pages
    ns = ncs + 1
    ncp = ebuf.shape[0]
    nkeys = past + LANES
    wo_refs = (wo0_ref, wo1_ref, wo2_ref, wo3_ref)
    wb = wbuf.shape[2]
    n_slc = 2 * NSA_KV_HEADS

    def block_of(pos):
        half, page = _divmod_const(pos, n_pages)
        return jnp.where(pos < ncs, 2 * page + half, pos)

    def page_copies(seq_idx, dst_slot, j):
        pid = pt_ref[seq_idx * n_pages + j]
        slc = tuple(
            pltpu.make_async_copy(pages_hbm.at[page0 + pid, :, n_slc + i, :], pbuf.at[dst_slot, i, j],
                                  psem.at[dst_slot])
            for i in range(n_slc))
        return slc + (pltpu.make_async_copy(comp_hbm.at[pid], cbuf.at[dst_slot, j], csem.at[dst_slot]),)

    def window_copies(seq_idx, dst_slot):
        return tuple(
            pltpu.make_async_copy(win_hbm.at[layer, seq_idx, :, i, :], wbuf.at[dst_slot, i], wsem.at[dst_slot])
            for i in range(n_slc))

    def start_seq(seq_idx, dst_slot):
        def body(j, c):
            for cp in page_copies(seq_idx, dst_slot, j):
                cp.start()
            return c
        lax.fori_loop(0, n_pages, body, 0)
        for cp in window_copies(seq_idx, dst_slot):
            cp.start()

    def wait_seq(seq_idx, dst_slot):
        def body(j, c):
            for cp in page_copies(seq_idx, dst_slot, j):
                cp.wait()
            return c
        lax.fori_loop(0, n_pages, body, 0)
        for cp in window_copies(seq_idx, dst_slot):
            cp.wait()

    @pl.when(b == 0)
    def _():
        start_seq(0, 0)
        pos = lax.broadcasted_iota(jnp.int32, ebuf.shape, 0)
        key = lax.broadcasted_iota(jnp.int32, ebuf.shape, 1)
        ebuf[...] = ((lax.shift_right_logical(key, 6) == block_of(pos)) & (pos <= ncs)).astype(BF16)

    @pl.when(b + 1 < n_seq)
    def _():
        start_seq(b + 1, 1 - slot)

    wait_seq(b, slot)

    t_col = lax.broadcasted_iota(jnp.int32, (tn, 1), 0)
    q_pos = past + t_col
    scale = HEAD_DIM ** -0.5
    rows = NSA_GROUP * tn

    def rep_g(x):
        return jnp.concatenate([x] * NSA_GROUP, axis=0)

    def pad_rows(x, n):
        if x.shape[0] == n:
            return x
        return jnp.concatenate([x, jnp.zeros((n - x.shape[0], x.shape[1]), x.dtype)], axis=0)

    pos1 = lax.broadcasted_iota(jnp.int32, (tn, ncp), 1)
    blk1 = block_of(pos1)
    dist_c = q_pos - (blk1 * CMP_BLOCK + (CMP_BLOCK - 1))
    mask_c = rep_g((dist_c >= 0) & (pos1 < ncs))
    cur = lax.shift_right_logical(q_pos, 6)
    eligible = (blk1 * CMP_BLOCK <= q_pos) & (pos1 <= ncs)
    forced = (blk1 == 0) | (blk1 == cur) | (blk1 == cur - 1)

    qk, o_cmp, scores = [], [], []
    for k in range(NSA_KV_HEADS):
        heads = [k * NSA_GROUP + g for g in range(NSA_GROUP)]
        q = jnp.concatenate(
            [q_ref[:, (k * NSA_GROUP + g) * HEAD_DIM:(k * NSA_GROUP + g + 1) * HEAD_DIM] for g in range(NSA_GROUP)],
            axis=0)
        q = (q * scale).astype(BF16)
        qk.append(q)
        kc = jnp.concatenate([cbuf[slot, :, k, :], cbuf[slot, :, 2 * NSA_KV_HEADS + k, :]], axis=0)
        vc = jnp.concatenate([cbuf[slot, :, NSA_KV_HEADS + k, :], cbuf[slot, :, 3 * NSA_KV_HEADS + k, :]], axis=0)
        kc = pad_rows(kc, ncp).astype(BF16)
        vc = pad_rows(vc, ncp).astype(BF16)
        bias_c = jnp.concatenate(_t5_bias(jnp.maximum(dist_c, 0), tab_ref, heads), axis=0)
        s = _dot_nt(q, kc) + bias_c
        e, den = _masked_softmax_unnorm(s, mask_c)
        p = e / den
        o_cmp.append(_dot(p.astype(BF16), vc))
        imp = p[0:tn]
        for g in range(1, NSA_GROUP):
            imp = imp + p[g * tn:(g + 1) * tn]
        sc = jnp.where(eligible, imp + jnp.where(forced, FORCE_SCORE, 0.0), -1.0)
        scores.append(jnp.where(pos1 <= ncs, sc, -2.0))

    def block_of_static(p):
        return 2 * (p % n_pages) + p // n_pages if p < ncs else p

    blk_lane = block_of(lax.broadcasted_iota(jnp.int32, (1, ncp), 1))
    sel = _select_blocks_rows(jnp.concatenate(scores, axis=0), blk_lane, ns, block_of_static)

    kpos = lax.broadcasted_iota(jnp.int32, (tn, nkeys), 1)
    dist_s = q_pos - kpos
    causal_s = rep_g((dist_s >= 0) & (kpos < past + tn))
    near = 2 * LANES
    dist_near = jnp.maximum(dist_s[:, nkeys - near:], 0)
    jw = lax.broadcasted_iota(jnp.int32, (tn, wb + LANES), 1)
    dist_w = wb + t_col - jw
    mask_w = rep_g((dist_w >= 0) & (dist_w <= WINDOW) & (past - wb + jw >= 0) & (jw < wb + tn))
    dist_w_near = jnp.maximum(dist_w[:, wb + LANES - near:], 0)
    gates = jax.nn.sigmoid(gt_ref[...])

    for k in range(NSA_KV_HEADS):
        heads = [k * NSA_GROUP + g for g in range(NSA_GROUP)]
        far = jnp.concatenate([jnp.full((tn, 1), tab_ref[REL_BUCKETS - 1, h], F32) for h in heads], axis=0)
        q = qk[k]
        k_new = pad_rows(kvn_ref[:, (2 * NSA_KV_HEADS + k) * HEAD_DIM:(2 * NSA_KV_HEADS + k + 1) * HEAD_DIM], LANES)
        v_new = pad_rows(kvn_ref[:, (3 * NSA_KV_HEADS + k) * HEAD_DIM:(3 * NSA_KV_HEADS + k + 1) * HEAD_DIM], LANES)
        k_all = jnp.concatenate([pbuf[slot, k].reshape(past, HEAD_DIM), k_new], axis=0).astype(BF16)
        v_all = jnp.concatenate([pbuf[slot, NSA_KV_HEADS + k].reshape(past, HEAD_DIM), v_new], axis=0).astype(BF16)
        sel_k = rep_g(sel[k * tn:(k + 1) * tn, :]).astype(BF16)
        mask_s = (_dot(sel_k, ebuf[...]) > 0.5) & causal_s
        b_near = jnp.concatenate(_t5_bias(dist_near, tab_ref, heads), axis=0) - far
        bias_s = jnp.concatenate([jnp.zeros((rows, nkeys - near), F32), b_near], axis=1)
        s = _dot_nt(q, k_all) + far + bias_s
        e, den = _masked_softmax_unnorm(s, mask_s)
        o_s = _dot(e.astype(BF16), v_all) / den
        kw = jnp.concatenate([wbuf[slot, k],
                              pad_rows(wn_ref[:, k * HEAD_DIM:(k + 1) * HEAD_DIM], LANES)], axis=0).astype(BF16)
        vw = jnp.concatenate(
            [wbuf[slot, NSA_KV_HEADS + k],
             pad_rows(wn_ref[:, (NSA_KV_HEADS + k) * HEAD_DIM:(NSA_KV_HEADS + k + 1) * HEAD_DIM], LANES)],
            axis=0).astype(BF16)
        bw_near = jnp.concatenate(_t5_bias(dist_w_near, tab_ref, heads), axis=0) - far
        bias_w = jnp.concatenate([jnp.zeros((rows, wb + LANES - near), F32), bw_near], axis=1)
        sw = _dot_nt(q, kw) + far + bias_w
        ew, denw = _masked_softmax_unnorm(sw, mask_w)
        o_w = _dot(ew.astype(BF16), vw) / denw
        for g in range(NSA_GROUP):
            c0 = k * LANES + g * 3
            r = slice(g * tn, (g + 1) * tn)
            y = (gates[:, c0:c0 + 1] * o_cmp[k][r] + gates[:, c0 + 1:c0 + 2] * o_s[r]
                 + gates[:, c0 + 2:c0 + 3] * o_w[r])
            h = k * NSA_GROUP + g
            o_ref[:, h * HEAD_DIM:(h + 1) * HEAD_DIM] = y.astype(o_ref.dtype)

    for i in range(n_slc):
        wo_refs[i][0, 0:wb - tn, :] = wbuf[slot, i, tn:wb, :]
        wo_refs[i][0, wb - tn:wb, :] = wn_ref[:, i * HEAD_DIM:(i + 1) * HEAD_DIM]


def _nsa_sample(z, row0, n_seq, tn, pages, page0, comp_pages, cache_win_l, layer, page_table, rel_bias, cols,
                out_buf):
    n_pages = page_table.shape[1]
    page_size = pages.shape[1]
    assert page_size == 2 * CMP_BLOCK and tn == SUBLANES and row0 % tn == 0
    past = n_pages * page_size
    ncs = 2 * n_pages
    ncp = _round_up(ncs + 1, LANES)
    wb = cache_win_l.shape[2]
    assert wb == WINDOW and past >= WINDOW
    half_cols = 2 * NSA_KV_HEADS * HEAD_DIM
    n_slc = 2 * NSA_KV_HEADS
    rb = row0 // tn
    grid_spec = pltpu.PrefetchScalarGridSpec(
        num_scalar_prefetch=1,
        grid=(n_seq,),
        in_specs=[
            pl.BlockSpec(memory_space=pltpu.SMEM),
            pl.BlockSpec((tn, NSA_WIDTH), lambda b, pt: (rb + b, cols["q"] // NSA_WIDTH)),
            pl.BlockSpec((tn, 2 * half_cols), lambda b, pt: (rb + b, cols["kv"] // (2 * half_cols))),
            pl.BlockSpec((tn, half_cols), lambda b, pt: (rb + b, cols["win"] // half_cols)),
            pl.BlockSpec((tn, 2 * LANES), lambda b, pt: (rb + b, cols["ng"] // (2 * LANES))),
            pl.BlockSpec(memory_space=pl.ANY),
            pl.BlockSpec(memory_space=pl.ANY),
            pl.BlockSpec(memory_space=pl.ANY),
            pl.BlockSpec(memory_space=pl.ANY),
        ],
        out_specs=[pl.BlockSpec((tn, NSA_WIDTH), lambda b, pt: (rb + b, 0))]
        + [pl.BlockSpec((1, wb, HEAD_DIM), lambda b, pt: (b, 0, 0)) for _ in range(n_slc)],
        scratch_shapes=[
            pltpu.VMEM((2, n_slc, n_pages, page_size, HEAD_DIM), F32),
            pltpu.VMEM((2, n_pages, 2 * 2 * NSA_KV_HEADS, HEAD_DIM), F32),
            pltpu.VMEM((2, n_slc, wb, HEAD_DIM), F32),
            pltpu.VMEM((ncp, past + LANES), BF16),
            pltpu.SemaphoreType.DMA((2,)),
            pltpu.SemaphoreType.DMA((2,)),
            pltpu.SemaphoreType.DMA((2,)),
        ],
    )
    return pl.pallas_call(
        functools.partial(_nsa_sample_kernel, n_seq=n_seq, n_pages=n_pages, page0=page0, layer=layer, tn=tn),
        grid_spec=grid_spec,
        out_shape=[jax.ShapeDtypeStruct(out_buf.shape, out_buf.dtype)]
        + [jax.ShapeDtypeStruct((n_seq, wb, HEAD_DIM), F32) for _ in range(n_slc)],
        input_output_aliases={9: 0},
        compiler_params=_cparams(("arbitrary",)),
        name="nsa_sample",
    )(page_table.reshape(-1), rel_bias, z, z, z, z, cache_win_l, pages, comp_pages, out_buf)


def _cumsum8(x):
    row = lax.broadcasted_iota(jnp.int32, x.shape, 0)
    for sh in (1, 2, 4):
        x = x + jnp.where(row >= sh, pltpu.roll(x, sh, 0), 0.0)
    return x


def _mixer_kernel(zcb_ref, zcc_ref, zch_ref, zhg_ref, cw_ref, cs_ref, hs_ref, lb_ref, hn_ref, _yb_buf, _yc_buf,
                  yb_ref, yc_ref, cso_ref, hso_ref, uprev, st, qf_s, k_s, lf_s, *, layer, tt):
    ti = pl.program_id(1)
    nt = pl.num_programs(1)

    @pl.when(ti == 0)
    def _():
        uprev[...] = jnp.zeros(uprev.shape, F32)
        uprev[SUBLANES - (CONV_K - 1):SUBLANES, :] = cs_ref[0]
        for h in range(HG_HEADS):
            st[h] = hs_ref[0, h].T

    cb = zcb_ref[...]
    u = zcc_ref[...] * zch_ref[...]
    ext = jnp.concatenate([uprev[...], u], axis=0)
    y = cw_ref[CONV_K - 1:CONV_K, :] * u
    for j in range(CONV_K - 1):
        shift = CONV_K - 1 - j
        y = y + cw_ref[j:j + 1, :] * pltpu.roll(ext, shift, 0)[SUBLANES:, :]
    yb_ref[...] = (cb * y).astype(yb_ref.dtype)
    tail = ext[tt:tt + SUBLANES, :]
    uprev[...] = tail

    @pl.when(ti == nt - 1)
    def _():
        cso_ref[0] = pltpu.roll(tail, CONV_K - 1, 0)[0:CONV_K - 1, :]

    hw = HG_HEADS * HG_DK
    lg = lb_ref[...]
    mx = jnp.max(lg, axis=0, keepdims=True)
    ex = jnp.exp(lg - mx)
    sm = ex / jnp.sum(ex, axis=0, keepdims=True)
    cs = sm[0:1]
    for i in range(1, layer + 1):
        cs = cs + sm[i:i + 1]
    lb = cs - sm[0:1]
    fx = zhg_ref[:, hw:2 * hw]
    lb_pos = lb > 0.0
    log_lb = jnp.log(jnp.where(lb_pos, lb, 1.0))
    log_sig = jnp.minimum(fx, 0.0) - jnp.log1p(jnp.exp(-jnp.abs(fx)))
    a2 = jnp.log1p(-lb) + log_sig
    lae = jnp.maximum(log_lb, a2) + jnp.log1p(jnp.exp(-jnp.abs(log_lb - a2)))
    lf_s[...] = jnp.where(lb_pos, lae, log_sig)
    k_s[...] = (1.0 - lb) * jax.nn.sigmoid(-fx)
    hq = zhg_ref[:, 0:hw]
    qf_s[...] = hq * jax.nn.sigmoid(hq)
    norm_g = hn_ref[...]
    ts = lax.broadcasted_iota(jnp.int32, (HG_BLOCK, HG_BLOCK, 1), 0)
    ss = lax.broadcasted_iota(jnp.int32, (HG_BLOCK, HG_BLOCK, 1), 1)
    causal = ss <= ts

    def block(i, c):
        r0 = pl.multiple_of(i * HG_BLOCK, HG_BLOCK)
        rs = pl.ds(r0, HG_BLOCK)
        for h in range(HG_HEADS):
            kc = slice(h * HG_DK, (h + 1) * HG_DK)
            vcol = slice(2 * hw + h * HG_DV, 2 * hw + (h + 1) * HG_DV)
            gcol = slice(2 * hw + HG_HEADS * HG_DV + h * HG_DV, 2 * hw + HG_HEADS * HG_DV + (h + 1) * HG_DV)
            qf = qf_s[rs, kc]
            kk = k_s[rs, kc]
            g = _cumsum8(lf_s[rs, kc])
            v = zhg_ref[rs, vcol]
            gl = g[HG_BLOCK - 1:HG_BLOCK, :]
            st_h = st[h]
            o = _dot_nt((qf * jnp.exp(g)).astype(BF16), st_h.astype(BF16))
            diff = jnp.where(causal, g[:, None, :] - g[None, :, :], 0.0)
            x3 = jnp.where(causal, qf[:, None, :] * kk[None, :, :] * jnp.exp(diff), 0.0)
            a = jnp.sum(x3, axis=-1, keepdims=True)
            o = o + jnp.sum(a * v[None, :, :], axis=1)
            kd = kk * jnp.exp(gl - g)
            st[h] = st_h * jnp.exp(gl) + _dot_tn(v.astype(BF16), kd.astype(BF16))
            on = o * lax.rsqrt(jnp.mean(o * o, axis=-1, keepdims=True) + NORM_EPS) * norm_g
            gx = zhg_ref[rs, gcol]
            yc_ref[rs, h * HG_DV:(h + 1) * HG_DV] = (on * (gx * jax.nn.sigmoid(gx))).astype(yc_ref.dtype)
        return c

    n_blk = tt // HG_BLOCK
    lax.fori_loop(0, n_blk, block, 0, unroll=min(4, n_blk))

    @pl.when(ti == nt - 1)
    def _():
        for h in range(HG_HEADS):
            hso_ref[0, h] = st[h].T


def _mixer(z, row0, n_seq, seq, conv_w_l, conv_state, hg_state, lb_logits, hg_norm_l, layer, cols, yb_buf, yc_buf):
    cw = conv_w_l.shape[-1]
    hgw = 2 * HG_HEADS * HG_DK + 2 * HG_HEADS * HG_DV
    tt = _pick_tile(seq, 256, SUBLANES)
    nt = seq // tt
    rb = row0 // tt
    assert row0 % tt == 0 and cols["cv"] % cw == 0 and cols["hg"] % hgw == 0
    depth = lb_logits.shape[0]
    cvb = cols["cv"] // cw
    return pl.pallas_call(
        functools.partial(_mixer_kernel, layer=layer, tt=tt),
        grid=(n_seq, nt),
        in_specs=[
            pl.BlockSpec((tt, cw), lambda b, t: (rb + b * nt + t, cvb)),
            pl.BlockSpec((tt, cw), lambda b, t: (rb + b * nt + t, cvb + 1)),
            pl.BlockSpec((tt, cw), lambda b, t: (rb + b * nt + t, cvb + 2)),
            pl.BlockSpec((tt, hgw), lambda b, t: (rb + b * nt + t, cols["hg"] // hgw)),
            pl.BlockSpec((CONV_K, cw), lambda b, t: (0, 0)),
            pl.BlockSpec((1, CONV_K - 1, cw), lambda b, t: (b, 0, 0)),
            pl.BlockSpec((1, HG_HEADS, HG_DK, HG_DV), lambda b, t: (b, 0, 0, 0)),
            pl.BlockSpec((depth, HG_HEADS * HG_DK), lambda b, t: (0, 0)),
            pl.BlockSpec((1, HG_DV), lambda b, t: (0, 0)),
            pl.BlockSpec(memory_space=pl.ANY),
            pl.BlockSpec(memory_space=pl.ANY),
        ],
        out_specs=[
            pl.BlockSpec((tt, cw), lambda b, t: (rb + b * nt + t, 0)),
            pl.BlockSpec((tt, HG_HEADS * HG_DV), lambda b, t: (rb + b * nt + t, 0)),
            pl.BlockSpec((1, CONV_K - 1, cw), lambda b, t: (b, 0, 0)),
            pl.BlockSpec((1, HG_HEADS, HG_DK, HG_DV), lambda b, t: (b, 0, 0, 0)),
        ],
        out_shape=[
            jax.ShapeDtypeStruct(yb_buf.shape, yb_buf.dtype),
            jax.ShapeDtypeStruct(yc_buf.shape, yc_buf.dtype),
            jax.ShapeDtypeStruct((n_seq, CONV_K - 1, cw), F32),
            jax.ShapeDtypeStruct((n_seq, HG_HEADS, HG_DK, HG_DV), F32),
        ],
        input_output_aliases={9: 0, 10: 1},
        scratch_shapes=[
            pltpu.VMEM((SUBLANES, cw), F32),
            pltpu.VMEM((HG_HEADS, HG_DV, HG_DK), F32),
            pltpu.VMEM((tt, HG_HEADS * HG_DK), F32),
            pltpu.VMEM((tt, HG_HEADS * HG_DK), F32),
            pltpu.VMEM((tt, HG_HEADS * HG_DK), F32),
        ],
        compiler_params=_cparams(("arbitrary", "arbitrary")),
        name="mixer_conv_hgrn",
    )(z, z, z, z, conv_w_l, conv_state, hg_state, lb_logits, hg_norm_l.reshape(1, HG_DV), yb_buf, yc_buf)


def _ep_identity(prods, extras):
    return prods[0]


def _ep_mix(prods, extras):
    ga, gb, gc = (jax.nn.sigmoid(e) for e in extras)
    return ga * prods[0] + gb * prods[1] + gc * prods[2]


def _ep_residual(prods, extras):
    return extras[0] + prods[0]


def _ep_swiglu(prods, extras):
    return prods[0] * jax.nn.sigmoid(prods[0]) * prods[1]


def _ep_ple(prods, extras):
    return extras[0] + jax.nn.sigmoid(prods[0]) * prods[1]


def kernel(x_prompt, x_sample, p_prompt, p_sample, cache_kv, cache_win, state_conv, state_hgrn, page_table,
           rel_bias, g_mix, w_in, phi_pe, phi_w1, phi_w2, conv_w, hg_lb_logits, hg_norm, w_pa, w_pb, w_pc, w_o,
           g_ffn, w_gate, w_up, w_down, w_pg, w_pp, g_final):
    bp, seq, d = x_prompt.shape
    bs, tn, _ = x_sample.shape
    depth = w_in.shape[0]
    n_in = w_in.shape[-1]
    cw = conv_w.shape[-1]
    d_ff = w_gate.shape[-1]
    ple = w_pp.shape[1]
    n_pool, page_size = cache_kv.shape[1], cache_kv.shape[2]
    mp, ms = bp * seq, bs * tn
    m = mp + ms
    kvw = 6 * NSA_KV_HEADS * HEAD_DIM
    hgw = 2 * HG_HEADS * HG_DK + 2 * HG_HEADS * HG_DV
    assert n_in == NSA_WIDTH + kvw + N_GATES + 3 * cw + hgw + 3 * d
    assert seq % Q_BLOCK == 0 and seq >= WINDOW

    o_ng = NSA_WIDTH + kvw
    cols = {"q": 0, "kv": NSA_WIDTH, "win": NSA_WIDTH + N_KV_ROWS * NSA_KV_HEADS * HEAD_DIM,
            "cv": o_ng, "hg": o_ng + 3 * cw, "mg": o_ng + 3 * cw + hgw, "ng": n_in - N_GATES}
    nz = cols["ng"] + GATE_PAD
    gk = N_GATES // NSA_KV_HEADS

    def relayout_w_in(w):
        gate_cols = w[:, o_ng:o_ng + N_GATES]
        parts = [w[:, :o_ng], w[:, o_ng + N_GATES:]]
        for k in range(NSA_KV_HEADS):
            parts += [gate_cols[:, k * gk:(k + 1) * gk], jnp.zeros((d, LANES - gk), w.dtype)]
        parts.append(jnp.zeros((d, GATE_PAD - NSA_KV_HEADS * LANES), w.dtype))
        return jnp.concatenate(parts, axis=1).astype(BF16)

    tm = _pick_tile(m, 1024, 256) if m % 256 == 0 else _pick_tile(m, 1024, SUBLANES)
    h = jnp.concatenate([x_prompt.reshape(mp, d), x_sample.reshape(ms, d)], axis=0)
    n_rows = N_KV_ROWS * NSA_KV_HEADS
    pages = cache_kv.reshape(depth * n_pool, page_size, n_rows, HEAD_DIM)
    pool_rows = cache_kv.reshape(depth * n_pool * page_size * n_rows, HEAD_DIM)
    nb_pool = n_pool * (page_size // CMP_BLOCK)
    cache_win_r = cache_win.reshape(depth, bs, cache_win.shape[2], 2 * NSA_KV_HEADS, HEAD_DIM)
    zeros_conv = jnp.zeros((bp, CONV_K - 1, cw), F32)
    zeros_hg = jnp.zeros((bp, HG_HEADS, HG_DK, HG_DV), F32)

    def w2d(w):
        return w.reshape(w.shape[0] * w.shape[1], w.shape[2])

    kv_p, kv_s, win_p, win_s, conv_p, conv_s, hg_p, hg_s = [], [], [], [], [], [], [], []
    for l in range(depth):
        xn = _rmsnorm(h, g_mix[l], BF16)
        z = _fused_matmul([xn], [0], [(relayout_w_in(w_in[l]), 0)], [], _ep_identity, nz, F32,
                          tm, _pick_tile(nz, 2560, 2 * LANES), "in_proj")

        comp_prompt = _compress(z, mp // CMP_BLOCK, cols["kv"] // HEAD_DIM, phi_pe[l], phi_w1[l], phi_w2[l],
                                _pick_tile(mp // CMP_BLOCK, 256, SUBLANES), "compress_prompt")
        cb_pool = _pick_tile(nb_pool, 64, SUBLANES)
        comp_pool = _compress_rows(pool_rows, l * nb_pool // cb_pool, nb_pool, n_rows, phi_pe[l],
                                   phi_w1[l].astype(BF16), phi_w2[l], cb_pool, "compress_pool")
        comp_pages = comp_pool.reshape(n_pool, (page_size // CMP_BLOCK) * 2 * NSA_KV_HEADS, HEAD_DIM)
        ya = _nsa_prompt(z, comp_prompt, rel_bias, bp, seq, cols, jnp.zeros((m, NSA_WIDTH), F32))
        ya, *win_new = _nsa_sample(z, mp, bs, tn, pages, l * n_pool, comp_pages, cache_win_r, l, page_table,
                                   rel_bias, cols, ya)
        yb, yc, cs_p, hs_p = _mixer(z, 0, bp, seq, conv_w[l], zeros_conv, zeros_hg, hg_lb_logits,
                                    hg_norm[l], l, cols, jnp.zeros((m, cw), F32),
                                    jnp.zeros((m, HG_HEADS * HG_DV), F32))
        yb, yc, cs_s, hs_s = _mixer(z, mp, bs, tn, conv_w[l], state_conv[l], state_hgrn[l], hg_lb_logits,
                                    hg_norm[l], l, cols, yb, yc)

        tn_d = _pick_tile(d, 1024, LANES)
        tn_mix = _pick_tile(d, 512, LANES)
        mgb = cols["mg"] // tn_mix
        mixed = _fused_matmul([ya, yb, yc], [0, 1, 2], [(w2d(w_pa), l), (w2d(w_pb), l), (w2d(w_pc), l)],
                              [(z, mgb), (z, mgb + d // tn_mix), (z, mgb + 2 * (d // tn_mix))],
                              _ep_mix, d, BF16, tm, tn_mix, "mix_proj")
        h = _fused_matmul([mixed], [0], [(w2d(w_o), l)], [(h, 0)], _ep_residual, d, F32, tm, tn_d, "out_proj")
        xf = _rmsnorm(h, g_ffn[l], BF16)
        act = _fused_matmul([xf], [0, 0], [(w2d(w_gate), l), (w2d(w_up), l)], [], _ep_swiglu, d_ff, BF16,
                            tm, _pick_tile(d_ff, 512, LANES), "ffn_up")
        h = _fused_matmul([act], [0], [(w_down[l].astype(BF16), 0)], [(h, 0)], _ep_residual, d, F32,
                          tm, _pick_tile(d, 512, LANES), "ffn_down")
        p_all = jnp.concatenate([p_prompt[l].reshape(mp, ple), p_sample[l].reshape(ms, ple)], axis=0)
        h = _fused_matmul([h, p_all], [0, 1], [(w2d(w_pg), l), (w2d(w_pp), l)], [(h, 0)], _ep_ple, d, F32,
                          tm, tn_d, "ple_gate")

        kvr = z[:, cols["kv"]:cols["win"]]
        kv_p.append(kvr[:mp].reshape(bp, seq, N_KV_ROWS, NSA_KV_HEADS, HEAD_DIM))
        kv_s.append(kvr[mp:].reshape(bs, tn, N_KV_ROWS, NSA_KV_HEADS, HEAD_DIM))
        keep = min(WINDOW, seq)
        zw = z[:mp, cols["win"]:cols["cv"]].reshape(bp, seq, 2, NSA_KV_HEADS, HEAD_DIM)
        win_p.append(zw[:, seq - keep:])
        win_s.append(jnp.stack(win_new, axis=2).reshape(bs, win_new[0].shape[1], 2, NSA_KV_HEADS, HEAD_DIM))
        conv_p.append(cs_p); conv_s.append(cs_s); hg_p.append(hs_p); hg_s.append(hs_s)

    y = _rmsnorm(h, g_final, F32)
    return (y[:mp].reshape(bp, seq, d), y[mp:].reshape(bs, tn, d), jnp.stack(kv_p), jnp.stack(kv_s),
            jnp.stack(win_p), jnp.stack(win_s), jnp.stack(conv_p), jnp.stack(conv_s),
            jnp.stack(hg_p), jnp.stack(hg_s))
```

```python
import functools
import math

import numpy as np
import jax
import jax.numpy as jnp
from jax import lax
from jax.experimental import pallas as pl
from jax.experimental.pallas import tpu as pltpu

F32 = jnp.float32
BF16 = jnp.bfloat16

LANES = 128
SUBLANES = 8
VMEM_LIMIT_BYTES = 56 * 1024 * 1024

NSA_HEADS = 8
NSA_KV_HEADS = 2
NSA_GROUP = NSA_HEADS // NSA_KV_HEADS
HEAD_DIM = 128
NSA_WIDTH = NSA_HEADS * HEAD_DIM
CMP_BLOCK = 64
N_SELECT = 16
WINDOW = 512
Q_BLOCK = 128
FAR_CHUNK = 512
FORCE_SCORE = 1.0e4
MASK_VALUE = -1.0e30
N_KV_ROWS = 4
CONV_K = 3
HG_HEADS = 4
HG_DK = 128
HG_DV = 128
HG_BLOCK = SUBLANES
REL_BUCKETS = 32
REL_MAX_DIST = 128
NORM_EPS = 1e-6
N_GATES = 3 * NSA_HEADS
GATE_PAD = 4 * LANES


def _round_up(x, m):
    return -(-x // m) * m


def _pick_tile(n, pref, unit):
    if n <= pref:
        return n
    best = None
    for t in range(unit, pref + 1, unit):
        if n % t == 0:
            best = t
    assert best is not None, (n, pref, unit)
    return best


def _divmod_const(x, n):
    if n & (n - 1) == 0:
        return lax.shift_right_logical(x, n.bit_length() - 1), x & (n - 1)
    return x // n, lax.rem(x, n)


def _t5_thresholds():
    n = np.arange(0, REL_MAX_DIST + 1)
    max_exact = REL_BUCKETS // 2
    nf = np.maximum(n, 1).astype(np.float32)
    ratio = np.log(nf / np.float32(max_exact)) / np.float32(math.log(REL_MAX_DIST / max_exact))
    large = max_exact + (ratio * np.float32(REL_BUCKETS - max_exact)).astype(np.int32)
    large = np.minimum(large, REL_BUCKETS - 1)
    bucket = np.where(n < max_exact, n, large)
    assert np.all(np.diff(bucket) >= 0) and bucket[-1] == REL_BUCKETS - 1
    return [int(np.argmax(bucket >= j)) for j in range(1, REL_BUCKETS)]


T5_THRESHOLDS = _t5_thresholds()


def _t5_bias(dist, tab_ref, heads):
    ge = [dist >= thr for thr in T5_THRESHOLDS]
    out = []
    for h in heads:
        b = jnp.full(dist.shape, tab_ref[0, h], F32)
        for j, m in enumerate(ge):
            b = jnp.where(m, tab_ref[j + 1, h], b)
        out.append(b)
    return out


def _dot(a, b):
    return jnp.dot(a, b, preferred_element_type=F32)


def _dot_nt(a, b):
    return lax.dot_general(a, b, (((1,), (1,)), ((), ())), preferred_element_type=F32)


def _dot_tn(a, b):
    return lax.dot_general(a, b, (((0,), (0,)), ((), ())), preferred_element_type=F32)


def _cparams(sem):
    return pltpu.CompilerParams(dimension_semantics=sem, vmem_limit_bytes=VMEM_LIMIT_BYTES)


def _rmsnorm_kernel(x_ref, g_ref, o_ref):
    x = x_ref[...]
    y = x * lax.rsqrt(jnp.mean(x * x, axis=-1, keepdims=True) + NORM_EPS)
    o_ref[...] = (y * g_ref[...]).astype(o_ref.dtype)


def _rmsnorm(x, g, out_dtype):
    m, d = x.shape
    tm = _pick_tile(m, 1024, SUBLANES)
    return pl.pallas_call(
        _rmsnorm_kernel,
        grid=(m // tm,),
        in_specs=[pl.BlockSpec((tm, d), lambda i: (i, 0)), pl.BlockSpec((1, d), lambda i: (0, 0))],
        out_specs=pl.BlockSpec((tm, d), lambda i: (i, 0)),
        out_shape=jax.ShapeDtypeStruct((m, d), out_dtype),
        compiler_params=_cparams(("parallel",)),
        name="rmsnorm",
    )(x, g.reshape(1, d))


def _mm_kernel(*refs, n_x, pairs, n_extra, epilogue, cast_w):
    xs = refs[:n_x]
    ws = refs[n_x:n_x + len(pairs)]
    extras = refs[n_x + len(pairs):n_x + len(pairs) + n_extra]
    o_ref = refs[n_x + len(pairs) + n_extra]
    wscr = refs[n_x + len(pairs) + n_extra + 1:]
    i = pl.program_id(1)

    if any(cast_w):
        @pl.when(i == 0)
        def _():
            s = 0
            for p in range(len(pairs)):
                if cast_w[p]:
                    wscr[s][...] = ws[p][...].astype(BF16)
                    s += 1

    xv = [x[...].astype(BF16) for x in xs]
    prods = []
    s = 0
    for p, xi in enumerate(pairs):
        if cast_w[p]:
            w = wscr[s][...]
            s += 1
        else:
            w = ws[p][...]
        prods.append(_dot(xv[xi], w))
    o_ref[...] = epilogue(prods, [e[...] for e in extras]).astype(o_ref.dtype)


def _fused_matmul(xs, pairs, ws, extras, epilogue, n_out, out_dtype, tm, tn, name):
    m = xs[0].shape[0]
    assert m % tm == 0 and n_out % tn == 0
    cast_w = tuple(w.dtype != BF16 for w, _ in ws)
    in_specs = []
    for x in xs:
        in_specs.append(pl.BlockSpec((tm, x.shape[1]), lambda j, i: (i, 0)))
    for p, (w, roff) in enumerate(ws):
        k = xs[pairs[p]].shape[1]
        in_specs.append(pl.BlockSpec((k, tn), functools.partial(lambda j, i, r: (r, j), r=roff)))
    for e, coff in extras:
        in_specs.append(pl.BlockSpec((tm, tn), functools.partial(lambda j, i, c: (i, c + j), c=coff)))
    scratch = [pltpu.VMEM((xs[pairs[p]].shape[1], tn), BF16) for p in range(len(ws)) if cast_w[p]]
    kern = functools.partial(_mm_kernel, n_x=len(xs), pairs=tuple(pairs), n_extra=len(extras),
                             epilogue=epilogue, cast_w=cast_w)
    return pl.pallas_call(
        kern,
        grid=(n_out // tn, m // tm),
        in_specs=in_specs,
        out_specs=pl.BlockSpec((tm, tn), lambda j, i: (i, j)),
        out_shape=jax.ShapeDtypeStruct((m, n_out), out_dtype),
        scratch_shapes=scratch,
        compiler_params=_cparams(("arbitrary", "arbitrary")),
        name=name,
    )(*xs, *[w for w, _ in ws], *[e for e, _ in extras])


def _w_in_relayout_kernel(w_ref, o_ref, *, o_ng, n_in):
    gk = N_GATES // NSA_KV_HEADS
    ng0 = n_in - N_GATES
    tr = o_ref.shape[0]
    o_ref[:, 0:o_ng] = w_ref[0, :, 0:o_ng].astype(BF16)
    o_ref[:, o_ng:ng0] = w_ref[0, :, o_ng + N_GATES:n_in].astype(BF16)
    gates = w_ref[0, :, o_ng:o_ng + LANES]
    lane = lax.broadcasted_iota(jnp.int32, (tr, LANES), 1)
    for k in range(NSA_KV_HEADS):
        g = gates if k == 0 else pltpu.roll(gates, LANES - k * gk, 1)
        o_ref[:, ng0 + k * LANES:ng0 + (k + 1) * LANES] = jnp.where(lane < gk, g, 0.0).astype(BF16)
    o_ref[:, ng0 + NSA_KV_HEADS * LANES:] = jnp.zeros((tr, GATE_PAD - NSA_KV_HEADS * LANES), BF16)


def _w_in_relayout(w_in, layer, o_ng, nz):
    _, d, n_in = w_in.shape
    tr = _pick_tile(d, 256, SUBLANES)
    return pl.pallas_call(
        functools.partial(_w_in_relayout_kernel, o_ng=o_ng, n_in=n_in),
        grid=(d // tr,),
        in_specs=[pl.BlockSpec((1, tr, n_in), lambda i: (layer, i, 0))],
        out_specs=pl.BlockSpec((tr, nz), lambda i: (i, 0)),
        out_shape=jax.ShapeDtypeStruct((d, nz), BF16),
        compiler_params=_cparams(("parallel",)),
        name="w_in_relayout",
    )(w_in)


def _rows_out_kernel(z_ref, _buf, o_ref, *, n_kinds):
    tm = z_ref.shape[0]
    for r in range(n_kinds):
        o_ref[pl.ds(r, tm, stride=n_kinds), :] = z_ref[:, r * HEAD_DIM:(r + 1) * HEAD_DIM]


def _rows_out(z, col0, n_kinds, n_seq, rows_per_seq, first_row, buf, layer, name):
    width = n_kinds * HEAD_DIM
    tm = _pick_tile(rows_per_seq, 512, SUBLANES)
    nt = rows_per_seq // tm
    assert col0 % width == 0 and all(first_row(s) % tm == 0 for s in range(n_seq))
    stride = (first_row(1) - first_row(0)) // tm if n_seq > 1 else 0
    rb0 = first_row(0) // tm
    return pl.pallas_call(
        functools.partial(_rows_out_kernel, n_kinds=n_kinds),
        grid=(n_seq, nt),
        in_specs=[pl.BlockSpec((tm, width), lambda s, i: (rb0 + s * stride + i, col0 // width)),
                  pl.BlockSpec(memory_space=pl.ANY)],
        out_specs=pl.BlockSpec((tm * n_kinds, HEAD_DIM), lambda s, i: ((layer * n_seq + s) * nt + i, 0)),
        out_shape=jax.ShapeDtypeStruct(buf.shape, buf.dtype),
        input_output_aliases={1: 0},
        compiler_params=_cparams(("arbitrary", "arbitrary")),
        name=name,
    )(z, buf)


def _compress_kernel(blk_ref, pe_ref, w1_ref, w2_ref, o_ref, x2d, w1b):
    @pl.when(pl.program_id(1) == 0)
    def _():
        w1b[...] = w1_ref[0].astype(BF16)

    cb = x2d.shape[0]
    for tau in range(CMP_BLOCK):
        x2d[:, tau * HEAD_DIM:(tau + 1) * HEAD_DIM] = (
            blk_ref[pl.ds(tau, cb, stride=CMP_BLOCK), :] + pe_ref[0, tau:tau + 1, :]).astype(BF16)
    h1 = _dot(x2d[...], w1b[...])
    h1 = h1 * jax.nn.sigmoid(h1)
    o_ref[...] = _dot(h1.astype(BF16), w2_ref[0].astype(BF16))


def _compress(tokens, n_blocks, col_block0, pe, w1, w2, cb, name):
    n_rk = 2 * NSA_KV_HEADS
    return pl.pallas_call(
        _compress_kernel,
        grid=(n_rk, n_blocks // cb),
        in_specs=[
            pl.BlockSpec((cb * CMP_BLOCK, HEAD_DIM), lambda rk, t: (t, col_block0 + rk)),
            pl.BlockSpec((1, CMP_BLOCK, HEAD_DIM), lambda rk, t: (rk // NSA_KV_HEADS, 0, 0)),
            pl.BlockSpec((1, CMP_BLOCK * HEAD_DIM, HEAD_DIM), lambda rk, t: (rk // NSA_KV_HEADS, 0, 0)),
            pl.BlockSpec((1, HEAD_DIM, HEAD_DIM), lambda rk, t: (rk // NSA_KV_HEADS, 0, 0)),
        ],
        out_specs=pl.BlockSpec((cb, HEAD_DIM), lambda rk, t: (t, rk)),
        out_shape=jax.ShapeDtypeStruct((n_blocks, n_rk * HEAD_DIM), F32),
        scratch_shapes=[pltpu.VMEM((cb, CMP_BLOCK * HEAD_DIM), BF16),
                        pltpu.VMEM((CMP_BLOCK * HEAD_DIM, HEAD_DIM), BF16)],
        compiler_params=_cparams(("arbitrary", "arbitrary")),
        name=name,
    )(tokens, pe, w1, w2)


def _compress_rows_kernel(tok_ref, pe_ref, w1_ref, w2_ref, o_ref, x2d):
    cb = tok_ref.shape[0]
    for tau in range(CMP_BLOCK):
        tok = tok_ref[:, tau * SUBLANES:(tau + 1) * SUBLANES, :] + pe_ref[tau]
        x2d[:, tau * HEAD_DIM:(tau + 1) * HEAD_DIM] = tok.reshape(cb * SUBLANES, HEAD_DIM).astype(BF16)
    h1 = _dot(x2d[...], w1_ref[...])
    h1 = h1 * jax.nn.sigmoid(h1)
    o_ref[...] = _dot(h1.astype(BF16), w2_ref[...])


def _compress_rows(blocks3d, row_block0, n_blocks, pe, w1, w2, cb, name):
    n_kinds = blocks3d.shape[1] // CMP_BLOCK
    assert n_kinds == SUBLANES
    kinds_k = NSA_KV_HEADS
    zero = jnp.zeros((CMP_BLOCK, n_kinds - 2 * kinds_k, HEAD_DIM), F32)
    pe_rows = jnp.concatenate([jnp.repeat(pe[0][:, None, :], kinds_k, axis=1),
                               jnp.repeat(pe[1][:, None, :], kinds_k, axis=1), zero], axis=1)
    w1_both = jnp.concatenate([w1[0], w1[1]], axis=1).astype(BF16)
    zw = jnp.zeros((HEAD_DIM, HEAD_DIM), F32)
    w2_both = jnp.concatenate([jnp.concatenate([w2[0], zw], axis=1),
                               jnp.concatenate([zw, w2[1]], axis=1)], axis=0).astype(BF16)
    return pl.pallas_call(
        _compress_rows_kernel,
        grid=(n_blocks // cb,),
        in_specs=[
            pl.BlockSpec((cb, CMP_BLOCK * n_kinds, HEAD_DIM), lambda t: (row_block0 + t, 0, 0)),
            pl.BlockSpec((CMP_BLOCK, n_kinds, HEAD_DIM), lambda t: (0, 0, 0)),
            pl.BlockSpec((CMP_BLOCK * HEAD_DIM, 2 * HEAD_DIM), lambda t: (0, 0)),
            pl.BlockSpec((2 * HEAD_DIM, 2 * HEAD_DIM), lambda t: (0, 0)),
        ],
        out_specs=pl.BlockSpec((cb * n_kinds, 2 * HEAD_DIM), lambda t: (t, 0)),
        out_shape=jax.ShapeDtypeStruct((n_blocks * n_kinds, 2 * HEAD_DIM), F32),
        scratch_shapes=[pltpu.VMEM((cb * n_kinds, CMP_BLOCK * HEAD_DIM), BF16)],
        compiler_params=_cparams(("arbitrary",)),
        name=name,
    )(blocks3d, pe_rows, w1_both, w2_both)


def _masked_softmax_unnorm(s, mask):
    s = jnp.where(mask, s, MASK_VALUE)
    m = jnp.max(s, axis=-1, keepdims=True)
    e = jnp.where(mask, jnp.exp(s - m), 0.0)
    den = jnp.maximum(jnp.sum(e, axis=-1, keepdims=True), 1e-30)
    return e, den


def _select_blocks(score_t_ref, n_blocks):
    sc = score_t_ref[...]
    idx = lax.broadcasted_iota(jnp.int32, sc.shape, 0)

    def body(i, cnt):
        row = score_t_ref[pl.ds(i, 1), :]
        return cnt + jnp.where(idx > i, (row >= sc).astype(F32), (row > sc).astype(F32))

    cnt = lax.fori_loop(0, n_blocks, body, jnp.zeros(sc.shape, F32), unroll=math.gcd(n_blocks, 8))
    return ((cnt < N_SELECT) & (sc >= 0.0)).astype(F32)


def _select_blocks_rows(score, blk_of_lane, n_pos, blk_of_pos):
    cnt = jnp.zeros(score.shape, F32)
    for p in range(n_pos):
        col = score[:, p:p + 1]
        cnt = cnt + jnp.where(blk_of_lane > blk_of_pos(p), (col >= score).astype(F32), (col > score).astype(F32))
    return ((cnt < N_SELECT) & (score >= 0.0)).astype(F32)


def _nsa_prompt_kernel(tab_ref, q_ref, ks_ref, vs_ref, kw_ref, vw_ref, gt_ref, kc_ref, vc_ref, _out_buf, o_ref,
                       ksb, vsb, kwb, vwb, kcb, vcb, ebuf, bnear, bcn, sct, sbuf, mrun, lrun, acc, *, seq):
    k = pl.program_id(1)
    qb = pl.program_id(2)
    nc = seq // CMP_BLOCK
    ncp = kcb.shape[0]
    n8 = sct.shape[0]
    g_heads = [k * NSA_GROUP + g for g in range(NSA_GROUP)]
    rows = NSA_GROUP * Q_BLOCK

    @pl.when(qb == 0)
    def _():
        ksb[0:Q_BLOCK, :] = jnp.zeros((Q_BLOCK, HEAD_DIM), BF16)
        ksb[Q_BLOCK:, :] = ks_ref[...].astype(BF16)
        vsb[0:Q_BLOCK, :] = jnp.zeros((Q_BLOCK, HEAD_DIM), BF16)
        vsb[Q_BLOCK:, :] = vs_ref[...].astype(BF16)
        kwb[0:WINDOW, :] = jnp.zeros((WINDOW, HEAD_DIM), BF16)
        kwb[WINDOW:, :] = kw_ref[...].astype(BF16)
        vwb[0:WINDOW, :] = jnp.zeros((WINDOW, HEAD_DIM), BF16)
        vwb[WINDOW:, :] = vw_ref[...].astype(BF16)
        kcb[...] = jnp.zeros(kcb.shape, BF16)
        vcb[...] = jnp.zeros(vcb.shape, BF16)
        kcb[0:nc, :] = kc_ref[...].astype(BF16)
        vcb[0:nc, :] = vc_ref[...].astype(BF16)
        blk = lax.broadcasted_iota(jnp.int32, ebuf.shape, 0)
        key = lax.broadcasted_iota(jnp.int32, ebuf.shape, 1) - Q_BLOCK
        ebuf[...] = ((key >= 0) & (lax.shift_right_logical(key, 6) == blk)).astype(BF16)
        ti = lax.broadcasted_iota(jnp.int32, (Q_BLOCK, Q_BLOCK), 0)
        tj = lax.broadcasted_iota(jnp.int32, (Q_BLOCK, Q_BLOCK), 1)
        b0 = _t5_bias(jnp.maximum(ti - tj, 0), tab_ref, g_heads)
        b1 = _t5_bias(Q_BLOCK + ti - tj, tab_ref, g_heads)
        tc = lax.broadcasted_iota(jnp.int32, (Q_BLOCK, ncp), 0)
        cc = lax.broadcasted_iota(jnp.int32, (Q_BLOCK, ncp), 1)
        dist_cn = jnp.where(cc < 4, tc + (CMP_BLOCK + 1) - CMP_BLOCK * cc, REL_MAX_DIST)
        bc = _t5_bias(jnp.maximum(dist_cn, 0), tab_ref, g_heads)
        for g in range(NSA_GROUP):
            far = tab_ref[REL_BUCKETS - 1, g_heads[g]]
            bnear[g * Q_BLOCK:(g + 1) * Q_BLOCK, 0:Q_BLOCK] = b1[g] - far
            bnear[g * Q_BLOCK:(g + 1) * Q_BLOCK, Q_BLOCK:] = b0[g] - far
            bcn[g * Q_BLOCK:(g + 1) * Q_BLOCK, :] = bc[g] - far

    def rep_g(x):
        return jnp.concatenate([x] * NSA_GROUP, axis=0)

    def fold_max(x):
        m = x[:, 0:LANES]
        for c in range(1, x.shape[1] // LANES):
            m = jnp.maximum(m, x[:, c * LANES:(c + 1) * LANES])
        return m

    def fold_sum(x):
        m = x[:, 0:LANES]
        for c in range(1, x.shape[1] // LANES):
            m = m + x[:, c * LANES:(c + 1) * LANES]
        return m

    t_col = lax.broadcasted_iota(jnp.int32, (Q_BLOCK, 1), 0)
    q_pos = qb * Q_BLOCK + t_col
    scale = HEAD_DIM ** -0.5
    q_all = jnp.concatenate(
        [(q_ref[:, g * HEAD_DIM:(g + 1) * HEAD_DIM] * scale).astype(BF16) for g in range(NSA_GROUP)], axis=0)
    gates = jax.nn.sigmoid(gt_ref[...])

    cidx = lax.broadcasted_iota(jnp.int32, (Q_BLOCK, ncp), 1)
    dist_c = q_pos - (cidx * CMP_BLOCK + (CMP_BLOCK - 1))
    mask_c = rep_g((dist_c >= 0) & (cidx < nc))
    bias_c = pltpu.roll(bcn[...], lax.rem(2 * qb - 2 + ncp, ncp), 1)
    e_c, den_c = _masked_softmax_unnorm(_dot_nt(q_all, kcb[...]) + bias_c, mask_c)
    p_c = e_c / den_c
    o_c = _dot(p_c.astype(BF16), vcb[...])
    imp = p_c[0:Q_BLOCK]
    for g in range(1, NSA_GROUP):
        imp = imp + p_c[g * Q_BLOCK:(g + 1) * Q_BLOCK]

    cur = lax.shift_right_logical(q_pos, 6)
    eligible = (cidx * CMP_BLOCK <= q_pos) & (cidx < nc)
    forced = (cidx == 0) | (cidx == cur) | (cidx == cur - 1)
    score = jnp.where(eligible, imp + jnp.where(forced, FORCE_SCORE, 0.0), -1.0)
    score = jnp.where(cidx < nc, score, -2.0)
    score_t = jnp.concatenate([score[:, c * LANES:(c + 1) * LANES].T for c in range(ncp // LANES)], axis=0)
    sct[...] = score_t[0:n8, :]
    sel_t = _select_blocks(sct, nc)
    if n8 < ncp:
        sel_t = jnp.concatenate([sel_t, jnp.zeros((ncp - n8, Q_BLOCK), F32)], axis=0)
    sel = jnp.concatenate([sel_t[c * LANES:(c + 1) * LANES, :].T for c in range(ncp // LANES)], axis=1)
    sel = sel.astype(BF16)

    near = 2 * Q_BLOCK
    n0 = pl.multiple_of(qb * Q_BLOCK, Q_BLOCK)
    jn = lax.broadcasted_iota(jnp.int32, (Q_BLOCK, near), 1)
    ok_near = (_dot(sel, ebuf[:, pl.ds(n0, near)]) > 0.5) & (jn - Q_BLOCK <= t_col)
    mb_near = rep_g(jnp.where(ok_near, 0.0, MASK_VALUE))
    s_near = _dot_nt(q_all, ksb[pl.ds(n0, near), :]) + bnear[...] + mb_near
    far_end = (qb - 1) * Q_BLOCK
    n_chunks = _divmod_const(jnp.maximum(far_end + (FAR_CHUNK - 1), 0), FAR_CHUNK)[0]
    mrun[...] = jnp.full(mrun.shape, MASK_VALUE, F32)

    def pass1(c, carry):
        c0 = pl.multiple_of(c * FAR_CHUNK, FAR_CHUNK)
        p0 = pl.multiple_of(c * FAR_CHUNK + Q_BLOCK, Q_BLOCK)
        pos = c * FAR_CHUNK + lax.broadcasted_iota(jnp.int32, (Q_BLOCK, FAR_CHUNK), 1)
        ok = (_dot(sel, ebuf[:, pl.ds(p0, FAR_CHUNK)]) > 0.5) & (pos < far_end)
        mb = jnp.where(ok, 0.0, MASK_VALUE)
        s = _dot_nt(q_all, ksb[pl.ds(p0, FAR_CHUNK), :])
        for g in range(NSA_GROUP):
            r = slice(g * Q_BLOCK, (g + 1) * Q_BLOCK)
            sg = s[r] + mb
            sbuf[r, pl.ds(c0, FAR_CHUNK)] = sg
            mrun[r, :] = jnp.maximum(mrun[r, :], fold_max(sg))
        return carry

    lax.fori_loop(0, n_chunks, pass1, 0)
    m_s = jnp.maximum(jnp.max(mrun[...], axis=-1, keepdims=True), jnp.max(s_near, axis=-1, keepdims=True))
    e_near = jnp.exp(s_near - m_s)
    mrun[...] = jnp.broadcast_to(m_s, mrun.shape)
    lrun[...] = fold_sum(e_near)
    acc[...] = _dot(e_near.astype(BF16), vsb[pl.ds(n0, near), :])

    def pass2(c, carry):
        c0 = pl.multiple_of(c * FAR_CHUNK, FAR_CHUNK)
        p0 = pl.multiple_of(c * FAR_CHUNK + Q_BLOCK, Q_BLOCK)
        m_full = mrun[...]
        e = jnp.exp(sbuf[:, pl.ds(c0, FAR_CHUNK)] - jnp.concatenate([m_full] * (FAR_CHUNK // LANES), axis=1))
        lrun[...] = lrun[...] + fold_sum(e)
        acc[...] = acc[...] + _dot(e.astype(BF16), vsb[pl.ds(p0, FAR_CHUNK), :])
        return carry

    lax.fori_loop(0, n_chunks, pass2, 0)
    o_s = acc[...] / jnp.sum(lrun[...], axis=-1, keepdims=True)

    wlen = WINDOW + Q_BLOCK
    jw = lax.broadcasted_iota(jnp.int32, (Q_BLOCK, wlen), 1)
    dist_w = WINDOW + t_col - jw
    ok_w = (dist_w >= 0) & (dist_w <= WINDOW) & (qb * Q_BLOCK - WINDOW + jw >= 0)
    mb_w = rep_g(jnp.where(ok_w, 0.0, MASK_VALUE))
    bias_w = jnp.concatenate([jnp.zeros((rows, wlen - near), F32), bnear[...]], axis=1)
    s_w = _dot_nt(q_all, kwb[pl.ds(n0, wlen), :]) + bias_w + mb_w
    e_w = jnp.exp(s_w - jnp.max(s_w, axis=-1, keepdims=True))
    o_w = _dot(e_w.astype(BF16), vwb[pl.ds(n0, wlen), :]) / jnp.sum(e_w, axis=-1, keepdims=True)

    for g in range(NSA_GROUP):
        r = slice(g * Q_BLOCK, (g + 1) * Q_BLOCK)
        c0 = g * 3
        y = gates[:, c0:c0 + 1] * o_c[r] + gates[:, c0 + 1:c0 + 2] * o_s[r] + gates[:, c0 + 2:c0 + 3] * o_w[r]
        o_ref[:, g * HEAD_DIM:(g + 1) * HEAD_DIM] = y.astype(o_ref.dtype)


def _nsa_prompt(z, comp, rel_bias, n_batch, seq, cols, out_buf):
    nqb = seq // Q_BLOCK
    nc = seq // CMP_BLOCK
    ncp = _round_up(nc, LANES)
    n8 = _round_up(nc, SUBLANES)
    kvc = cols["kv"] // HEAD_DIM
    winc = cols["win"] // HEAD_DIM
    kh = NSA_KV_HEADS
    gw = NSA_GROUP * HEAD_DIM
    rows = NSA_GROUP * Q_BLOCK
    assert seq % FAR_CHUNK == 0

    def zspec(col_block_of_k):
        return pl.BlockSpec((seq, HEAD_DIM), lambda b, k, q: (b, col_block_of_k(k)))

    return pl.pallas_call(
        functools.partial(_nsa_prompt_kernel, seq=seq),
        grid=(n_batch, kh, nqb),
        in_specs=[
            pl.BlockSpec(memory_space=pltpu.SMEM),
            pl.BlockSpec((Q_BLOCK, gw), lambda b, k, q: (b * nqb + q, cols["q"] // gw + k)),
            zspec(lambda k: kvc + 2 * kh + k),
            zspec(lambda k: kvc + 3 * kh + k),
            zspec(lambda k: winc + k),
            zspec(lambda k: winc + kh + k),
            pl.BlockSpec((Q_BLOCK, LANES), lambda b, k, q: (b * nqb + q, cols["ng"] // LANES + k)),
            pl.BlockSpec((nc, HEAD_DIM), lambda b, k, q: (b, k)),
            pl.BlockSpec((nc, HEAD_DIM), lambda b, k, q: (b, kh + k)),
            pl.BlockSpec(memory_space=pl.ANY),
        ],
        out_specs=pl.BlockSpec((Q_BLOCK, gw), lambda b, k, q: (b * nqb + q, k)),
        out_shape=jax.ShapeDtypeStruct(out_buf.shape, out_buf.dtype),
        input_output_aliases={9: 0},
        scratch_shapes=[
            pltpu.VMEM((seq + Q_BLOCK, HEAD_DIM), BF16), pltpu.VMEM((seq + Q_BLOCK, HEAD_DIM), BF16),
            pltpu.VMEM((seq + WINDOW, HEAD_DIM), BF16), pltpu.VMEM((seq + WINDOW, HEAD_DIM), BF16),
            pltpu.VMEM((ncp, HEAD_DIM), BF16), pltpu.VMEM((ncp, HEAD_DIM), BF16),
            pltpu.VMEM((ncp, seq + Q_BLOCK), BF16),
            pltpu.VMEM((rows, 2 * Q_BLOCK), F32),
            pltpu.VMEM((rows, ncp), F32),
            pltpu.VMEM((n8, Q_BLOCK), F32),
            pltpu.VMEM((rows, seq), F32),
            pltpu.VMEM((rows, LANES), F32), pltpu.VMEM((rows, LANES), F32), pltpu.VMEM((rows, LANES), F32),
        ],
        compiler_params=_cparams(("arbitrary", "arbitrary", "arbitrary")),
        name="nsa_prompt",
    )(rel_bias, z, z, z, z, z, z, comp, comp, out_buf)


def _nsa_sample_kernel(pt_ref, tab_ref, q_ref, kvn_ref, wn_ref, gt_ref, win_hbm, pages_hbm, comp_hbm, _out_buf,
                       _win_buf, o_ref, wo_ref,
                       pbuf, cbuf, wbuf, ebuf, psem, csem, wsem, *, n_seq, n_pages, page0, layer, tn):
    b = pl.program_id(0)
    slot = lax.rem(b, 2)
    past = n_pages * 2 * CMP_BLOCK
    ncs = 2 * n_pages
    ns = ncs + 1
    ncp = ebuf.shape[0]
    nkeys = past + LANES
    wb = wbuf.shape[2]
    n_slc = 2 * NSA_KV_HEADS

    def block_of(pos):
        half, page = _divmod_const(pos, n_pages)
        return jnp.where(pos < ncs, 2 * page + half, pos)

    def page_copies(seq_idx, dst_slot, j):
        pid = pt_ref[seq_idx * n_pages + j]
        slc = tuple(
            pltpu.make_async_copy(pages_hbm.at[page0 + pid, :, n_slc + i, :], pbuf.at[dst_slot, i, j],
                                  psem.at[dst_slot])
            for i in range(n_slc))
        return slc + (pltpu.make_async_copy(comp_hbm.at[pid], cbuf.at[dst_slot, j], csem.at[dst_slot]),)

    def window_copies(seq_idx, dst_slot):
        return tuple(
            pltpu.make_async_copy(win_hbm.at[layer, seq_idx, :, i, :], wbuf.at[dst_slot, i], wsem.at[dst_slot])
            for i in range(n_slc))

    def start_seq(seq_idx, dst_slot):
        def body(j, c):
            for cp in page_copies(seq_idx, dst_slot, j):
                cp.start()
            return c
        lax.fori_loop(0, n_pages, body, 0)
        for cp in window_copies(seq_idx, dst_slot):
            cp.start()

    def wait_seq(seq_idx, dst_slot):
        def body(j, c):
            for cp in page_copies(seq_idx, dst_slot, j):
                cp.wait()
            return c
        lax.fori_loop(0, n_pages, body, 0)
        for cp in window_copies(seq_idx, dst_slot):
            cp.wait()

    @pl.when(b == 0)
    def _():
        start_seq(0, 0)
        pos = lax.broadcasted_iota(jnp.int32, ebuf.shape, 0)
        key = lax.broadcasted_iota(jnp.int32, ebuf.shape, 1)
        ebuf[...] = ((lax.shift_right_logical(key, 6) == block_of(pos)) & (pos <= ncs)).astype(BF16)

    @pl.when(b + 1 < n_seq)
    def _():
        start_seq(b + 1, 1 - slot)

    wait_seq(b, slot)

    t_col = lax.broadcasted_iota(jnp.int32, (tn, 1), 0)
    q_pos = past + t_col
    scale = HEAD_DIM ** -0.5
    rows = NSA_GROUP * tn

    def rep_g(x):
        return jnp.concatenate([x] * NSA_GROUP, axis=0)

    def pad_rows(x, n):
        if x.shape[0] == n:
            return x
        return jnp.concatenate([x, jnp.zeros((n - x.shape[0], x.shape[1]), x.dtype)], axis=0)

    pos1 = lax.broadcasted_iota(jnp.int32, (tn, ncp), 1)
    blk1 = block_of(pos1)
    dist_c = q_pos - (blk1 * CMP_BLOCK + (CMP_BLOCK - 1))
    mask_c = rep_g((dist_c >= 0) & (pos1 < ncs))
    cur = lax.shift_right_logical(q_pos, 6)
    eligible = (blk1 * CMP_BLOCK <= q_pos) & (pos1 <= ncs)
    forced = (blk1 == 0) | (blk1 == cur) | (blk1 == cur - 1)

    qk, o_cmp, scores = [], [], []
    for k in range(NSA_KV_HEADS):
        heads = [k * NSA_GROUP + g for g in range(NSA_GROUP)]
        q = jnp.concatenate(
            [q_ref[:, (k * NSA_GROUP + g) * HEAD_DIM:(k * NSA_GROUP + g + 1) * HEAD_DIM] for g in range(NSA_GROUP)],
            axis=0)
        q = (q * scale).astype(BF16)
        qk.append(q)
        kc = jnp.concatenate([cbuf[slot, :, half * SUBLANES + k, 0:HEAD_DIM] for half in range(2)], axis=0)
        vc = jnp.concatenate([cbuf[slot, :, half * SUBLANES + NSA_KV_HEADS + k, HEAD_DIM:] for half in range(2)],
                             axis=0)
        kc = pad_rows(kc, ncp).astype(BF16)
        vc = pad_rows(vc, ncp).astype(BF16)
        bias_c = jnp.concatenate(_t5_bias(jnp.maximum(dist_c, 0), tab_ref, heads), axis=0)
        s = _dot_nt(q, kc) + bias_c
        e, den = _masked_softmax_unnorm(s, mask_c)
        p = e / den
        o_cmp.append(_dot(p.astype(BF16), vc))
        imp = p[0:tn]
        for g in range(1, NSA_GROUP):
            imp = imp + p[g * tn:(g + 1) * tn]
        sc = jnp.where(eligible, imp + jnp.where(forced, FORCE_SCORE, 0.0), -1.0)
        scores.append(jnp.where(pos1 <= ncs, sc, -2.0))

    def block_of_static(p):
        return 2 * (p % n_pages) + p // n_pages if p < ncs else p

    blk_lane = block_of(lax.broadcasted_iota(jnp.int32, (1, ncp), 1))
    sel = _select_blocks_rows(jnp.concatenate(scores, axis=0), blk_lane, ns, block_of_static)

    kpos = lax.broadcasted_iota(jnp.int32, (tn, nkeys), 1)
    dist_s = q_pos - kpos
    causal_s = rep_g((dist_s >= 0) & (kpos < past + tn))
    near = 2 * LANES
    dist_near = jnp.maximum(dist_s[:, nkeys - near:], 0)
    jw = lax.broadcasted_iota(jnp.int32, (tn, wb + LANES), 1)
    dist_w = wb + t_col - jw
    mask_w = rep_g((dist_w >= 0) & (dist_w <= WINDOW) & (past - wb + jw >= 0) & (jw < wb + tn))
    dist_w_near = jnp.maximum(dist_w[:, wb + LANES - near:], 0)
    gates = jax.nn.sigmoid(gt_ref[...])

    for k in range(NSA_KV_HEADS):
        heads = [k * NSA_GROUP + g for g in range(NSA_GROUP)]
        far = jnp.concatenate([jnp.full((tn, 1), tab_ref[REL_BUCKETS - 1, h], F32) for h in heads], axis=0)
        q = qk[k]
        k_new = pad_rows(kvn_ref[:, (2 * NSA_KV_HEADS + k) * HEAD_DIM:(2 * NSA_KV_HEADS + k + 1) * HEAD_DIM], LANES)
        v_new = pad_rows(kvn_ref[:, (3 * NSA_KV_HEADS + k) * HEAD_DIM:(3 * NSA_KV_HEADS + k + 1) * HEAD_DIM], LANES)
        k_all = jnp.concatenate([pbuf[slot, k].reshape(past, HEAD_DIM), k_new], axis=0).astype(BF16)
        v_all = jnp.concatenate([pbuf[slot, NSA_KV_HEADS + k].reshape(past, HEAD_DIM), v_new], axis=0).astype(BF16)
        sel_k = rep_g(sel[k * tn:(k + 1) * tn, :]).astype(BF16)
        mask_s = (_dot(sel_k, ebuf[...]) > 0.5) & causal_s
        b_near = jnp.concatenate(_t5_bias(dist_near, tab_ref, heads), axis=0) - far
        bias_s = jnp.concatenate([jnp.zeros((rows, nkeys - near), F32), b_near], axis=1)
        s = _dot_nt(q, k_all) + far + bias_s
        e, den = _masked_softmax_unnorm(s, mask_s)
        o_s = _dot(e.astype(BF16), v_all) / den
        kw = jnp.concatenate([wbuf[slot, k],
                              pad_rows(wn_ref[:, k * HEAD_DIM:(k + 1) * HEAD_DIM], LANES)], axis=0).astype(BF16)
        vw = jnp.concatenate(
            [wbuf[slot, NSA_KV_HEADS + k],
             pad_rows(wn_ref[:, (NSA_KV_HEADS + k) * HEAD_DIM:(NSA_KV_HEADS + k + 1) * HEAD_DIM], LANES)],
            axis=0).astype(BF16)
        bw_near = jnp.concatenate(_t5_bias(dist_w_near, tab_ref, heads), axis=0) - far
        bias_w = jnp.concatenate([jnp.zeros((rows, wb + LANES - near), F32), bw_near], axis=1)
        sw = _dot_nt(q, kw) + far + bias_w
        ew, denw = _masked_softmax_unnorm(sw, mask_w)
        o_w = _dot(ew.astype(BF16), vw) / denw
        for g in range(NSA_GROUP):
            c0 = k * LANES + g * 3
            r = slice(g * tn, (g + 1) * tn)
            y = (gates[:, c0:c0 + 1] * o_cmp[k][r] + gates[:, c0 + 1:c0 + 2] * o_s[r]
                 + gates[:, c0 + 2:c0 + 3] * o_w[r])
            h = k * NSA_GROUP + g
            o_ref[:, h * HEAD_DIM:(h + 1) * HEAD_DIM] = y.astype(o_ref.dtype)

    for i in range(n_slc):
        wo_ref[0, pl.ds(i, wb - tn, stride=n_slc), :] = wbuf[slot, i, tn:wb, :]
        wo_ref[0, pl.ds((wb - tn) * n_slc + i, tn, stride=n_slc), :] = wn_ref[:, i * HEAD_DIM:(i + 1) * HEAD_DIM]


def _nsa_sample(z, row0, n_seq, tn, pages, page0, comp_pages, cache_win_l, layer, page_table, rel_bias, cols,
                out_buf, win_buf):
    n_pages = page_table.shape[1]
    page_size = pages.shape[1]
    assert page_size == 2 * CMP_BLOCK and tn == SUBLANES and row0 % tn == 0
    past = n_pages * page_size
    ncs = 2 * n_pages
    ncp = _round_up(ncs + 1, LANES)
    wb = cache_win_l.shape[2]
    assert wb == WINDOW and past >= WINDOW
    half_cols = 2 * NSA_KV_HEADS * HEAD_DIM
    n_slc = 2 * NSA_KV_HEADS
    rb = row0 // tn
    grid_spec = pltpu.PrefetchScalarGridSpec(
        num_scalar_prefetch=1,
        grid=(n_seq,),
        in_specs=[
            pl.BlockSpec(memory_space=pltpu.SMEM),
            pl.BlockSpec((tn, NSA_WIDTH), lambda b, pt: (rb + b, cols["q"] // NSA_WIDTH)),
            pl.BlockSpec((tn, 2 * half_cols), lambda b, pt: (rb + b, cols["kv"] // (2 * half_cols))),
            pl.BlockSpec((tn, half_cols), lambda b, pt: (rb + b, cols["win"] // half_cols)),
            pl.BlockSpec((tn, 2 * LANES), lambda b, pt: (rb + b, cols["ng"] // (2 * LANES))),
            pl.BlockSpec(memory_space=pl.ANY),
            pl.BlockSpec(memory_space=pl.ANY),
            pl.BlockSpec(memory_space=pl.ANY),
            pl.BlockSpec(memory_space=pl.ANY),
            pl.BlockSpec(memory_space=pl.ANY),
        ],
        out_specs=[pl.BlockSpec((tn, NSA_WIDTH), lambda b, pt: (rb + b, 0)),
                   pl.BlockSpec((1, wb * n_slc, HEAD_DIM), lambda b, pt: (layer * n_seq + b, 0, 0))],
        scratch_shapes=[
            pltpu.VMEM((2, n_slc, n_pages, page_size, HEAD_DIM), F32),
            pltpu.VMEM((2, n_pages) + comp_pages.shape[1:], F32),
            pltpu.VMEM((2, n_slc, wb, HEAD_DIM), F32),
            pltpu.VMEM((ncp, past + LANES), BF16),
            pltpu.SemaphoreType.DMA((2,)),
            pltpu.SemaphoreType.DMA((2,)),
            pltpu.SemaphoreType.DMA((2,)),
        ],
    )
    return pl.pallas_call(
        functools.partial(_nsa_sample_kernel, n_seq=n_seq, n_pages=n_pages, page0=page0, layer=layer, tn=tn),
        grid_spec=grid_spec,
        out_shape=[jax.ShapeDtypeStruct(out_buf.shape, out_buf.dtype),
                   jax.ShapeDtypeStruct(win_buf.shape, win_buf.dtype)],
        input_output_aliases={9: 0, 10: 1},
        compiler_params=_cparams(("arbitrary",)),
        name="nsa_sample",
    )(page_table.reshape(-1), rel_bias, z, z, z, z, cache_win_l, pages, comp_pages, out_buf, win_buf)


def _cumsum8(x):
    row = lax.broadcasted_iota(jnp.int32, x.shape, 0)
    for sh in (1, 2, 4):
        x = x + jnp.where(row >= sh, pltpu.roll(x, sh, 0), 0.0)
    return x


def _mixer_kernel(zcb_ref, zcc_ref, zch_ref, zhg_ref, cw_ref, cs_ref, hs_ref, lb_ref, hn_ref, _yb_buf, _yc_buf,
                  yb_ref, yc_ref, cso_ref, hso_ref, uprev, st, qf_s, k_s, lf_s, *, layer, tt):
    ti = pl.program_id(1)
    nt = pl.num_programs(1)

    @pl.when(ti == 0)
    def _():
        uprev[...] = jnp.zeros(uprev.shape, F32)
        uprev[SUBLANES - (CONV_K - 1):SUBLANES, :] = cs_ref[0]
        for h in range(HG_HEADS):
            st[h] = hs_ref[0, h].T

    cb = zcb_ref[...]
    u = zcc_ref[...] * zch_ref[...]
    ext = jnp.concatenate([uprev[...], u], axis=0)
    y = cw_ref[CONV_K - 1:CONV_K, :] * u
    for j in range(CONV_K - 1):
        shift = CONV_K - 1 - j
        y = y + cw_ref[j:j + 1, :] * pltpu.roll(ext, shift, 0)[SUBLANES:, :]
    yb_ref[...] = (cb * y).astype(yb_ref.dtype)
    tail = ext[tt:tt + SUBLANES, :]
    uprev[...] = tail

    @pl.when(ti == nt - 1)
    def _():
        cso_ref[0] = pltpu.roll(tail, CONV_K - 1, 0)[0:CONV_K - 1, :]

    hw = HG_HEADS * HG_DK
    lg = lb_ref[...]
    mx = jnp.max(lg, axis=0, keepdims=True)
    ex = jnp.exp(lg - mx)
    sm = ex / jnp.sum(ex, axis=0, keepdims=True)
    cs = sm[0:1]
    for i in range(1, layer + 1):
        cs = cs + sm[i:i + 1]
    lb = cs - sm[0:1]
    fx = zhg_ref[:, hw:2 * hw]
    lb_pos = lb > 0.0
    log_lb = jnp.log(jnp.where(lb_pos, lb, 1.0))
    log_sig = jnp.minimum(fx, 0.0) - jnp.log1p(jnp.exp(-jnp.abs(fx)))
    a2 = jnp.log1p(-lb) + log_sig
    lae = jnp.maximum(log_lb, a2) + jnp.log1p(jnp.exp(-jnp.abs(log_lb - a2)))
    lf_s[...] = jnp.where(lb_pos, lae, log_sig)
    k_s[...] = (1.0 - lb) * jax.nn.sigmoid(-fx)
    hq = zhg_ref[:, 0:hw]
    qf_s[...] = hq * jax.nn.sigmoid(hq)
    norm_g = hn_ref[...]
    ts = lax.broadcasted_iota(jnp.int32, (HG_BLOCK, HG_BLOCK, 1), 0)
    ss = lax.broadcasted_iota(jnp.int32, (HG_BLOCK, HG_BLOCK, 1), 1)
    causal = ss <= ts

    def block(i, c):
        r0 = pl.multiple_of(i * HG_BLOCK, HG_BLOCK)
        rs = pl.ds(r0, HG_BLOCK)
        for h in range(HG_HEADS):
            kc = slice(h * HG_DK, (h + 1) * HG_DK)
            vcol = slice(2 * hw + h * HG_DV, 2 * hw + (h + 1) * HG_DV)
            gcol = slice(2 * hw + HG_HEADS * HG_DV + h * HG_DV, 2 * hw + HG_HEADS * HG_DV + (h + 1) * HG_DV)
            qf = qf_s[rs, kc]
            kk = k_s[rs, kc]
            g = _cumsum8(lf_s[rs, kc])
            v = zhg_ref[rs, vcol]
            gl = g[HG_BLOCK - 1:HG_BLOCK, :]
            st_h = st[h]
            o = _dot_nt((qf * jnp.exp(g)).astype(BF16), st_h.astype(BF16))
            diff = jnp.where(causal, g[:, None, :] - g[None, :, :], 0.0)
            x3 = jnp.where(causal, qf[:, None, :] * kk[None, :, :] * jnp.exp(diff), 0.0)
            a = jnp.sum(x3, axis=-1, keepdims=True)
            o = o + jnp.sum(a * v[None, :, :], axis=1)
            kd = kk * jnp.exp(gl - g)
            st[h] = st_h * jnp.exp(gl) + _dot_tn(v.astype(BF16), kd.astype(BF16))
            on = o * lax.rsqrt(jnp.mean(o * o, axis=-1, keepdims=True) + NORM_EPS) * norm_g
            gx = zhg_ref[rs, gcol]
            yc_ref[rs, h * HG_DV:(h + 1) * HG_DV] = (on * (gx * jax.nn.sigmoid(gx))).astype(yc_ref.dtype)
        return c

    n_blk = tt // HG_BLOCK
    lax.fori_loop(0, n_blk, block, 0, unroll=min(4, n_blk))

    @pl.when(ti == nt - 1)
    def _():
        for h in range(HG_HEADS):
            hso_ref[0, h] = st[h].T


def _mixer(z, row0, n_seq, seq, conv_w_l, conv_state, hg_state, lb_logits, hg_norm_l, layer, cols, yb_buf, yc_buf):
    cw = conv_w_l.shape[-1]
    hgw = 2 * HG_HEADS * HG_DK + 2 * HG_HEADS * HG_DV
    tt = _pick_tile(seq, 256, SUBLANES)
    nt = seq // tt
    rb = row0 // tt
    assert row0 % tt == 0 and cols["cv"] % cw == 0 and cols["hg"] % hgw == 0
    depth = lb_logits.shape[0]
    cvb = cols["cv"] // cw
    return pl.pallas_call(
        functools.partial(_mixer_kernel, layer=layer, tt=tt),
        grid=(n_seq, nt),
        in_specs=[
            pl.BlockSpec((tt, cw), lambda b, t: (rb + b * nt + t, cvb)),
            pl.BlockSpec((tt, cw), lambda b, t: (rb + b * nt + t, cvb + 1)),
            pl.BlockSpec((tt, cw), lambda b, t: (rb + b * nt + t, cvb + 2)),
            pl.BlockSpec((tt, hgw), lambda b, t: (rb + b * nt + t, cols["hg"] // hgw)),
            pl.BlockSpec((CONV_K, cw), lambda b, t: (0, 0)),
            pl.BlockSpec((1, CONV_K - 1, cw), lambda b, t: (b, 0, 0)),
            pl.BlockSpec((1, HG_HEADS, HG_DK, HG_DV), lambda b, t: (b, 0, 0, 0)),
            pl.BlockSpec((depth, HG_HEADS * HG_DK), lambda b, t: (0, 0)),
            pl.BlockSpec((1, HG_DV), lambda b, t: (0, 0)),
            pl.BlockSpec(memory_space=pl.ANY),
            pl.BlockSpec(memory_space=pl.ANY),
        ],
        out_specs=[
            pl.BlockSpec((tt, cw), lambda b, t: (rb + b * nt + t, 0)),
            pl.BlockSpec((tt, HG_HEADS * HG_DV), lambda b, t: (rb + b * nt + t, 0)),
            pl.BlockSpec((1, CONV_K - 1, cw), lambda b, t: (b, 0, 0)),
            pl.BlockSpec((1, HG_HEADS, HG_DK, HG_DV), lambda b, t: (b, 0, 0, 0)),
        ],
        out_shape=[
            jax.ShapeDtypeStruct(yb_buf.shape, yb_buf.dtype),
            jax.ShapeDtypeStruct(yc_buf.shape, yc_buf.dtype),
            jax.ShapeDtypeStruct((n_seq, CONV_K - 1, cw), F32),
            jax.ShapeDtypeStruct((n_seq, HG_HEADS, HG_DK, HG_DV), F32),
        ],
        input_output_aliases={9: 0, 10: 1},
        scratch_shapes=[
            pltpu.VMEM((SUBLANES, cw), F32),
            pltpu.VMEM((HG_HEADS, HG_DV, HG_DK), F32),
            pltpu.VMEM((tt, HG_HEADS * HG_DK), F32),
            pltpu.VMEM((tt, HG_HEADS * HG_DK), F32),
            pltpu.VMEM((tt, HG_HEADS * HG_DK), F32),
        ],
        compiler_params=_cparams(("arbitrary", "arbitrary")),
        name="mixer_conv_hgrn",
    )(z, z, z, z, conv_w_l, conv_state, hg_state, lb_logits, hg_norm_l.reshape(1, HG_DV), yb_buf, yc_buf)


def _ep_identity(prods, extras):
    return prods[0]


def _ep_mix(prods, extras):
    ga, gb, gc = (jax.nn.sigmoid(e) for e in extras)
    return ga * prods[0] + gb * prods[1] + gc * prods[2]


def _ep_residual(prods, extras):
    return extras[0] + prods[0]


def _ep_swiglu(prods, extras):
    return prods[0] * jax.nn.sigmoid(prods[0]) * prods[1]


def _ep_ple(prods, extras):
    return extras[0] + jax.nn.sigmoid(prods[0]) * prods[1]


def kernel(x_prompt, x_sample, p_prompt, p_sample, cache_kv, cache_win, state_conv, state_hgrn, page_table,
           rel_bias, g_mix, w_in, phi_pe, phi_w1, phi_w2, conv_w, hg_lb_logits, hg_norm, w_pa, w_pb, w_pc, w_o,
           g_ffn, w_gate, w_up, w_down, w_pg, w_pp, g_final):
    bp, seq, d = x_prompt.shape
    bs, tn, _ = x_sample.shape
    depth = w_in.shape[0]
    n_in = w_in.shape[-1]
    cw = conv_w.shape[-1]
    d_ff = w_gate.shape[-1]
    ple = w_pp.shape[1]
    n_pool, page_size = cache_kv.shape[1], cache_kv.shape[2]
    mp, ms = bp * seq, bs * tn
    m = mp + ms
    kvw = 6 * NSA_KV_HEADS * HEAD_DIM
    hgw = 2 * HG_HEADS * HG_DK + 2 * HG_HEADS * HG_DV
    assert n_in == NSA_WIDTH + kvw + N_GATES + 3 * cw + hgw + 3 * d
    assert seq % Q_BLOCK == 0 and seq >= WINDOW

    o_ng = NSA_WIDTH + kvw
    cols = {"q": 0, "kv": NSA_WIDTH, "win": NSA_WIDTH + N_KV_ROWS * NSA_KV_HEADS * HEAD_DIM,
            "cv": o_ng, "hg": o_ng + 3 * cw, "mg": o_ng + 3 * cw + hgw, "ng": n_in - N_GATES}
    nz = cols["ng"] + GATE_PAD

    tm = _pick_tile(m, 1024, 256) if m % 256 == 0 else _pick_tile(m, 1024, SUBLANES)
    h = jnp.concatenate([x_prompt.reshape(mp, d), x_sample.reshape(ms, d)], axis=0)
    n_rows = N_KV_ROWS * NSA_KV_HEADS
    pages = cache_kv.reshape(depth * n_pool, page_size, n_rows, HEAD_DIM)
    pool_blocks = cache_kv.reshape(depth * n_pool * (page_size // CMP_BLOCK), CMP_BLOCK * n_rows, HEAD_DIM)
    nb_pool = n_pool * (page_size // CMP_BLOCK)
    cache_win_r = cache_win.reshape(depth, bs, cache_win.shape[2], 2 * NSA_KV_HEADS, HEAD_DIM)
    zeros_conv = jnp.zeros((bp, CONV_K - 1, cw), F32)
    zeros_hg = jnp.zeros((bp, HG_HEADS, HG_DK, HG_DV), F32)

    def w2d(w):
        return w.reshape(w.shape[0] * w.shape[1], w.shape[2])

    keep = min(WINDOW, seq)
    kv_p = jnp.zeros((depth * mp * n_rows, HEAD_DIM), F32)
    kv_s = jnp.zeros((depth * ms * n_rows, HEAD_DIM), F32)
    win_p = jnp.zeros((depth * bp * keep * (n_rows // 2), HEAD_DIM), F32)
    wb = cache_win.shape[2]
    win_s = jnp.zeros((depth * bs, wb * (n_rows // 2), HEAD_DIM), F32)
    conv_p, conv_s, hg_p, hg_s = [], [], [], []
    for l in range(depth):
        xn = _rmsnorm(h, g_mix[l], BF16)
        z = _fused_matmul([xn], [0], [(_w_in_relayout(w_in, l, o_ng, nz), 0)], [], _ep_identity, nz, F32,
                          tm, _pick_tile(nz, 2560, 2 * LANES), "in_proj")

        comp_prompt = _compress(z, mp // CMP_BLOCK, cols["kv"] // HEAD_DIM, phi_pe[l], phi_w1[l], phi_w2[l],
                                _pick_tile(mp // CMP_BLOCK, 256, SUBLANES), "compress_prompt")
        cb_pool = _pick_tile(nb_pool, 64, SUBLANES)
        comp_pool = _compress_rows(pool_blocks, l * nb_pool // cb_pool, nb_pool, phi_pe[l], phi_w1[l], phi_w2[l],
                                   cb_pool, "compress_pool")
        comp_pages = comp_pool.reshape(n_pool, (page_size // CMP_BLOCK) * n_rows, 2 * HEAD_DIM)
        ya = _nsa_prompt(z, comp_prompt, rel_bias, bp, seq, cols, jnp.zeros((m, NSA_WIDTH), F32))
        ya, win_s = _nsa_sample(z, mp, bs, tn, pages, l * n_pool, comp_pages, cache_win_r, l, page_table,
                                rel_bias, cols, ya, win_s)
        yb, yc, cs_p, hs_p = _mixer(z, 0, bp, seq, conv_w[l], zeros_conv, zeros_hg, hg_lb_logits,
                                    hg_norm[l], l, cols, jnp.zeros((m, cw), F32),
                                    jnp.zeros((m, HG_HEADS * HG_DV), F32))
        yb, yc, cs_s, hs_s = _mixer(z, mp, bs, tn, conv_w[l], state_conv[l], state_hgrn[l], hg_lb_logits,
                                    hg_norm[l], l, cols, yb, yc)

        tn_d = _pick_tile(d, 1024, LANES)
        tn_mix = _pick_tile(d, 512, LANES)
        mgb = cols["mg"] // tn_mix
        mixed = _fused_matmul([ya, yb, yc], [0, 1, 2], [(w2d(w_pa), l), (w2d(w_pb), l), (w2d(w_pc), l)],
                              [(z, mgb), (z, mgb + d // tn_mix), (z, mgb + 2 * (d // tn_mix))],
                              _ep_mix, d, BF16, tm, tn_mix, "mix_proj")
        h = _fused_matmul([mixed], [0], [(w2d(w_o), l)], [(h, 0)], _ep_residual, d, F32, tm, tn_d, "out_proj")
        xf = _rmsnorm(h, g_ffn[l], BF16)
        act = _fused_matmul([xf], [0, 0], [(w2d(w_gate), l), (w2d(w_up), l)], [], _ep_swiglu, d_ff, BF16,
                            tm, _pick_tile(d_ff, 512, LANES), "ffn_up")
        h = _fused_matmul([act], [0], [(w_down[l].astype(BF16), 0)], [(h, 0)], _ep_residual, d, F32,
                          tm, _pick_tile(d, 512, LANES), "ffn_down")
        p_all = jnp.concatenate([p_prompt[l].reshape(mp, ple), p_sample[l].reshape(ms, ple)], axis=0)
        h = _fused_matmul([h, p_all], [0, 1], [(w2d(w_pg), l), (w2d(w_pp), l)], [(h, 0)], _ep_ple, d, F32,
                          tm, tn_d, "ple_gate")

        kv_p = _rows_out(z, cols["kv"], n_rows, 1, mp, lambda s: 0, kv_p, l, "kv_rows_prompt")
        kv_s = _rows_out(z, cols["kv"], n_rows, 1, ms, lambda s: mp, kv_s, l, "kv_rows_sample")
        win_p = _rows_out(z, cols["win"], n_rows // 2, bp, keep, lambda s: s * seq + seq - keep, win_p, l,
                          "win_rows_prompt")
        conv_p.append(cs_p); conv_s.append(cs_s); hg_p.append(hs_p); hg_s.append(hs_s)

    y = _rmsnorm(h, g_final, F32)
    return (y[:mp].reshape(bp, seq, d), y[mp:].reshape(bs, tn, d),
            kv_p.reshape(depth, bp, seq, N_KV_ROWS, NSA_KV_HEADS, HEAD_DIM),
            kv_s.reshape(depth, bs, tn, N_KV_ROWS, NSA_KV_HEADS, HEAD_DIM),
            win_p.reshape(depth, bp, keep, 2, NSA_KV_HEADS, HEAD_DIM),
            win_s.reshape(depth, bs, wb, 2, NSA_KV_HEADS, HEAD_DIM), jnp.stack(conv_p), jnp.stack(conv_s),
            jnp.stack(hg_p), jnp.stack(hg_s))
```

```python
import functools
import math

import numpy as np
import jax
import jax.numpy as jnp
from jax import lax
from jax.experimental import pallas as pl
from jax.experimental.pallas import tpu as pltpu

F32 = jnp.float32
BF16 = jnp.bfloat16

LANES = 128
SUBLANES = 8
VMEM_LIMIT_BYTES = 56 * 1024 * 1024

NSA_HEADS = 8
NSA_KV_HEADS = 2
NSA_GROUP = NSA_HEADS // NSA_KV_HEADS
HEAD_DIM = 128
NSA_WIDTH = NSA_HEADS * HEAD_DIM
CMP_BLOCK = 64
N_SELECT = 16
WINDOW = 512
Q_BLOCK = 128
FAR_CHUNK = 1024
FORCE_SCORE = 1.0e4
MASK_VALUE = -1.0e30
N_KV_ROWS = 4
CONV_K = 3
HG_HEADS = 4
HG_DK = 128
HG_DV = 128
HG_BLOCK = SUBLANES
REL_BUCKETS = 32
REL_MAX_DIST = 128
NORM_EPS = 1e-6
N_GATES = 3 * NSA_HEADS
GATE_PAD = 4 * LANES


def _round_up(x, m):
    return -(-x // m) * m


def _pick_tile(n, pref, unit):
    if n <= pref:
        return n
    best = None
    for t in range(unit, pref + 1, unit):
        if n % t == 0:
            best = t
    assert best is not None, (n, pref, unit)
    return best


def _divmod_const(x, n):
    if n & (n - 1) == 0:
        return lax.shift_right_logical(x, n.bit_length() - 1), x & (n - 1)
    return x // n, lax.rem(x, n)


def _t5_thresholds():
    n = np.arange(0, REL_MAX_DIST + 1)
    max_exact = REL_BUCKETS // 2
    nf = np.maximum(n, 1).astype(np.float32)
    ratio = np.log(nf / np.float32(max_exact)) / np.float32(math.log(REL_MAX_DIST / max_exact))
    large = max_exact + (ratio * np.float32(REL_BUCKETS - max_exact)).astype(np.int32)
    large = np.minimum(large, REL_BUCKETS - 1)
    bucket = np.where(n < max_exact, n, large)
    assert np.all(np.diff(bucket) >= 0) and bucket[-1] == REL_BUCKETS - 1
    return [int(np.argmax(bucket >= j)) for j in range(1, REL_BUCKETS)]


T5_THRESHOLDS = _t5_thresholds()


def _t5_bias(dist, tab_ref, heads):
    ge = [dist >= thr for thr in T5_THRESHOLDS]
    out = []
    for h in heads:
        b = jnp.full(dist.shape, tab_ref[0, h], F32)
        for j, m in enumerate(ge):
            b = jnp.where(m, tab_ref[j + 1, h], b)
        out.append(b)
    return out


def _dot(a, b):
    return jnp.dot(a, b, preferred_element_type=F32)


def _dot_nt(a, b):
    return lax.dot_general(a, b, (((1,), (1,)), ((), ())), preferred_element_type=F32)


def _dot_tn(a, b):
    return lax.dot_general(a, b, (((0,), (0,)), ((), ())), preferred_element_type=F32)


def _cparams(sem):
    return pltpu.CompilerParams(dimension_semantics=sem, vmem_limit_bytes=VMEM_LIMIT_BYTES)


def _rmsnorm_kernel(x_ref, g_ref, o_ref):
    x = x_ref[...]
    y = x * lax.rsqrt(jnp.mean(x * x, axis=-1, keepdims=True) + NORM_EPS)
    o_ref[...] = (y * g_ref[...]).astype(o_ref.dtype)


def _rmsnorm(x, g, out_dtype):
    m, d = x.shape
    tm = _pick_tile(m, 1024, SUBLANES)
    return pl.pallas_call(
        _rmsnorm_kernel,
        grid=(m // tm,),
        in_specs=[pl.BlockSpec((tm, d), lambda i: (i, 0)), pl.BlockSpec((1, d), lambda i: (0, 0))],
        out_specs=pl.BlockSpec((tm, d), lambda i: (i, 0)),
        out_shape=jax.ShapeDtypeStruct((m, d), out_dtype),
        compiler_params=_cparams(("parallel",)),
        name="rmsnorm",
    )(x, g.reshape(1, d))


def _mm_kernel(*refs, n_x, pairs, n_extra, epilogue, cast_w, w_rows_out):
    xs = refs[:n_x]
    ws = refs[n_x:n_x + len(pairs)]
    extras = refs[n_x + len(pairs):n_x + len(pairs) + n_extra]
    o_ref = refs[n_x + len(pairs) + n_extra]
    wscr = refs[n_x + len(pairs) + n_extra + 1:]
    i = pl.program_id(1)

    if any(cast_w):
        @pl.when(i == 0)
        def _():
            s = 0
            for p in range(len(pairs)):
                if cast_w[p]:
                    wscr[s][...] = ws[p][...].astype(BF16)
                    s += 1

    xv = [x[...].astype(BF16) for x in xs]
    prods = []
    s = 0
    for p, xi in enumerate(pairs):
        if cast_w[p]:
            w = wscr[s][...]
            s += 1
        else:
            w = ws[p][...]
        prods.append(_dot_nt(xv[xi], w) if w_rows_out else _dot(xv[xi], w))
    o_ref[...] = epilogue(prods, [e[...] for e in extras]).astype(o_ref.dtype)


def _fused_matmul(xs, pairs, ws, extras, epilogue, n_out, out_dtype, tm, tn, name, w_rows_out=False):
    m = xs[0].shape[0]
    assert m % tm == 0 and n_out % tn == 0
    cast_w = tuple(w.dtype != BF16 for w, _ in ws)
    assert not (w_rows_out and any(cast_w))
    in_specs = []
    for x in xs:
        in_specs.append(pl.BlockSpec((tm, x.shape[1]), lambda j, i: (i, 0)))
    for p, (w, roff) in enumerate(ws):
        k = xs[pairs[p]].shape[1]
        if w_rows_out:
            in_specs.append(pl.BlockSpec((tn, k), lambda j, i: (j, 0)))
            continue
        in_specs.append(pl.BlockSpec((k, tn), functools.partial(lambda j, i, r: (r, j), r=roff)))
    for e, coff in extras:
        in_specs.append(pl.BlockSpec((tm, tn), functools.partial(lambda j, i, c: (i, c + j), c=coff)))
    scratch = [pltpu.VMEM((xs[pairs[p]].shape[1], tn), BF16) for p in range(len(ws)) if cast_w[p]]
    kern = functools.partial(_mm_kernel, n_x=len(xs), pairs=tuple(pairs), n_extra=len(extras),
                             epilogue=epilogue, cast_w=cast_w, w_rows_out=w_rows_out)
    return pl.pallas_call(
        kern,
        grid=(n_out // tn, m // tm),
        in_specs=in_specs,
        out_specs=pl.BlockSpec((tm, tn), lambda j, i: (i, j)),
        out_shape=jax.ShapeDtypeStruct((m, n_out), out_dtype),
        scratch_shapes=scratch,
        compiler_params=_cparams(("arbitrary", "arbitrary")),
        name=name,
    )(*xs, *[w for w, _ in ws], *[e for e, _ in extras])


def _w_in_relayout_kernel(a_ref, b0_ref, b1_ref, b2_ref, g0_ref, g1_ref, g2_ref, o_ref, *, n_plain, n_main):
    i = pl.program_id(0)
    tr = o_ref.shape[0]
    gk = N_GATES // NSA_KV_HEADS

    @pl.when(i < n_plain)
    def _():
        o_ref[...] = a_ref[0].astype(BF16)

    @pl.when((i >= n_plain) & (i < n_main))
    def _():
        o_ref[...] = jnp.concatenate([a_ref[0, N_GATES:, :], b0_ref[0], b1_ref[0], b2_ref[0]], axis=0).astype(BF16)

    @pl.when(i == n_main)
    def _():
        d = o_ref.shape[1]
        g = jnp.concatenate([g0_ref[0], g1_ref[0], g2_ref[0], jnp.zeros((LANES - N_GATES, d), F32)], axis=0)
        row = lax.broadcasted_iota(jnp.int32, (LANES, d), 0)
        blocks = [jnp.where(row < gk, g if k == 0 else pltpu.roll(g, LANES - k * gk, 0), 0.0)
                  for k in range(NSA_KV_HEADS)]
        blocks.append(jnp.zeros((tr - NSA_KV_HEADS * LANES, d), F32))
        o_ref[...] = jnp.concatenate(blocks, axis=0).astype(BF16)


def _w_in_relayout(w_in_t, layer, o_ng, nz):
    _, n_in, d = w_in_t.shape
    tr = GATE_PAD
    ng0 = n_in - N_GATES
    assert N_GATES == 3 * SUBLANES and o_ng % tr == 0 and ng0 % tr == 0 and nz == ng0 + tr
    n_plain, n_main = o_ng // tr, ng0 // tr
    sub = tr // SUBLANES

    def small(base8):
        return pl.BlockSpec((1, SUBLANES, d), lambda i: (layer, base8(i), 0))

    return pl.pallas_call(
        functools.partial(_w_in_relayout_kernel, n_plain=n_plain, n_main=n_main),
        grid=(n_main + 1,),
        in_specs=[pl.BlockSpec((1, tr, d), lambda i: (layer, jnp.minimum(i, n_main - 1), 0))]
        + [small(functools.partial(lambda i, j: jnp.minimum(i + 1, n_main) * sub + j, j=j)) for j in range(3)]
        + [small(functools.partial(lambda i, j: o_ng // SUBLANES + j, j=j)) for j in range(3)],
        out_specs=pl.BlockSpec((tr, d), lambda i: (i, 0)),
        out_shape=jax.ShapeDtypeStruct((nz, d), BF16),
        compiler_params=_cparams(("arbitrary",)),
        name="w_in_relayout",
    )(w_in_t, w_in_t, w_in_t, w_in_t, w_in_t, w_in_t, w_in_t)


def _rows_out_kernel(z_ref, _buf, o_ref, *, n_kinds):
    tm = z_ref.shape[0]
    for r in range(n_kinds):
        o_ref[pl.ds(r, tm, stride=n_kinds), :] = z_ref[:, r * HEAD_DIM:(r + 1) * HEAD_DIM]


def _rows_out(z, col0, n_kinds, n_seq, rows_per_seq, first_row, buf, layer, name):
    width = n_kinds * HEAD_DIM
    tm = _pick_tile(rows_per_seq, 512, SUBLANES)
    nt = rows_per_seq // tm
    assert col0 % width == 0 and all(first_row(s) % tm == 0 for s in range(n_seq))
    stride = (first_row(1) - first_row(0)) // tm if n_seq > 1 else 0
    rb0 = first_row(0) // tm
    return pl.pallas_call(
        functools.partial(_rows_out_kernel, n_kinds=n_kinds),
        grid=(n_seq, nt),
        in_specs=[pl.BlockSpec((tm, width), lambda s, i: (rb0 + s * stride + i, col0 // width)),
                  pl.BlockSpec(memory_space=pl.ANY)],
        out_specs=pl.BlockSpec((tm * n_kinds, HEAD_DIM), lambda s, i: ((layer * n_seq + s) * nt + i, 0)),
        out_shape=jax.ShapeDtypeStruct(buf.shape, buf.dtype),
        input_output_aliases={1: 0},
        compiler_params=_cparams(("arbitrary", "arbitrary")),
        name=name,
    )(z, buf)


def _compress_kernel(blk_ref, pe_ref, w1_ref, w2_ref, o_ref, x2d, w1b):
    @pl.when(pl.program_id(1) == 0)
    def _():
        w1b[...] = w1_ref[0].astype(BF16)

    cb = x2d.shape[0]
    for tau in range(CMP_BLOCK):
        x2d[:, tau * HEAD_DIM:(tau + 1) * HEAD_DIM] = (
            blk_ref[pl.ds(tau, cb, stride=CMP_BLOCK), :] + pe_ref[0, tau:tau + 1, :]).astype(BF16)
    h1 = _dot(x2d[...], w1b[...])
    h1 = h1 * jax.nn.sigmoid(h1)
    o_ref[...] = _dot(h1.astype(BF16), w2_ref[0].astype(BF16))


def _compress(tokens, n_blocks, col_block0, pe, w1, w2, cb, name):
    n_rk = 2 * NSA_KV_HEADS
    return pl.pallas_call(
        _compress_kernel,
        grid=(n_rk, n_blocks // cb),
        in_specs=[
            pl.BlockSpec((cb * CMP_BLOCK, HEAD_DIM), lambda rk, t: (t, col_block0 + rk)),
            pl.BlockSpec((1, CMP_BLOCK, HEAD_DIM), lambda rk, t: (rk // NSA_KV_HEADS, 0, 0)),
            pl.BlockSpec((1, CMP_BLOCK * HEAD_DIM, HEAD_DIM), lambda rk, t: (rk // NSA_KV_HEADS, 0, 0)),
            pl.BlockSpec((1, HEAD_DIM, HEAD_DIM), lambda rk, t: (rk // NSA_KV_HEADS, 0, 0)),
        ],
        out_specs=pl.BlockSpec((cb, HEAD_DIM), lambda rk, t: (t, rk)),
        out_shape=jax.ShapeDtypeStruct((n_blocks, n_rk * HEAD_DIM), F32),
        scratch_shapes=[pltpu.VMEM((cb, CMP_BLOCK * HEAD_DIM), BF16),
                        pltpu.VMEM((CMP_BLOCK * HEAD_DIM, HEAD_DIM), BF16)],
        compiler_params=_cparams(("arbitrary", "arbitrary")),
        name=name,
    )(tokens, pe, w1, w2)


def _compress_rows_kernel(tok_ref, pe_ref, w1_ref, w2_ref, o_ref, x2d):
    cb = tok_ref.shape[0]
    for tau in range(CMP_BLOCK):
        tok = tok_ref[:, tau * SUBLANES:(tau + 1) * SUBLANES, :] + pe_ref[tau]
        x2d[:, tau * HEAD_DIM:(tau + 1) * HEAD_DIM] = tok.reshape(cb * SUBLANES, HEAD_DIM).astype(BF16)
    h1 = _dot(x2d[...], w1_ref[...])
    h1 = h1 * jax.nn.sigmoid(h1)
    o_ref[...] = _dot(h1.astype(BF16), w2_ref[...])


def _compress_rows(blocks3d, row_block0, n_blocks, pe, w1, w2, cb, name):
    n_kinds = blocks3d.shape[1] // CMP_BLOCK
    assert n_kinds == SUBLANES
    kinds_k = NSA_KV_HEADS
    zero = jnp.zeros((CMP_BLOCK, n_kinds - 2 * kinds_k, HEAD_DIM), F32)
    pe_rows = jnp.concatenate([jnp.repeat(pe[0][:, None, :], kinds_k, axis=1),
                               jnp.repeat(pe[1][:, None, :], kinds_k, axis=1), zero], axis=1)
    w1_both = jnp.concatenate([w1[0], w1[1]], axis=1).astype(BF16)
    zw = jnp.zeros((HEAD_DIM, HEAD_DIM), F32)
    w2_both = jnp.concatenate([jnp.concatenate([w2[0], zw], axis=1),
                               jnp.concatenate([zw, w2[1]], axis=1)], axis=0).astype(BF16)
    return pl.pallas_call(
        _compress_rows_kernel,
        grid=(n_blocks // cb,),
        in_specs=[
            pl.BlockSpec((cb, CMP_BLOCK * n_kinds, HEAD_DIM), lambda t: (row_block0 + t, 0, 0)),
            pl.BlockSpec((CMP_BLOCK, n_kinds, HEAD_DIM), lambda t: (0, 0, 0)),
            pl.BlockSpec((CMP_BLOCK * HEAD_DIM, 2 * HEAD_DIM), lambda t: (0, 0)),
            pl.BlockSpec((2 * HEAD_DIM, 2 * HEAD_DIM), lambda t: (0, 0)),
        ],
        out_specs=pl.BlockSpec((cb * n_kinds, 2 * HEAD_DIM), lambda t: (t, 0)),
        out_shape=jax.ShapeDtypeStruct((n_blocks * n_kinds, 2 * HEAD_DIM), F32),
        scratch_shapes=[pltpu.VMEM((cb * n_kinds, CMP_BLOCK * HEAD_DIM), BF16)],
        compiler_params=_cparams(("arbitrary",)),
        name=name,
    )(blocks3d, pe_rows, w1_both, w2_both)


def _masked_softmax_unnorm(s, mask):
    s = jnp.where(mask, s, MASK_VALUE)
    m = jnp.max(s, axis=-1, keepdims=True)
    e = jnp.where(mask, jnp.exp(s - m), 0.0)
    den = jnp.maximum(jnp.sum(e, axis=-1, keepdims=True), 1e-30)
    return e, den


def _select_blocks(score_t_ref, n_blocks):
    sc = score_t_ref[...]
    idx = lax.broadcasted_iota(jnp.int32, sc.shape, 0)

    def body(i, cnt):
        row = score_t_ref[pl.ds(i, 1), :]
        return cnt + jnp.where(idx > i, (row >= sc).astype(F32), (row > sc).astype(F32))

    cnt = lax.fori_loop(0, n_blocks, body, jnp.zeros(sc.shape, F32), unroll=math.gcd(n_blocks, 8))
    return ((cnt < N_SELECT) & (sc >= 0.0)).astype(F32)


def _select_blocks_rows(score, blk_of_lane, n_pos, blk_of_pos):
    cnt = jnp.zeros(score.shape, F32)
    for p in range(n_pos):
        col = score[:, p:p + 1]
        cnt = cnt + jnp.where(blk_of_lane > blk_of_pos(p), (col >= score).astype(F32), (col > score).astype(F32))
    return ((cnt < N_SELECT) & (score >= 0.0)).astype(F32)


def _nsa_prompt_kernel(tab_ref, q_ref, ks_ref, vs_ref, kw_ref, vw_ref, gt_ref, kc_ref, vc_ref, _out_buf, o_ref,
                       ksb, vsb, kwb, vwb, kcb, vcb, ebuf, bnear, bcn, sct, sbuf, mrun, lrun, acc, *, seq):
    k = pl.program_id(1)
    qb = pl.program_id(2)
    nc = seq // CMP_BLOCK
    ncp = kcb.shape[0]
    n8 = sct.shape[0]
    g_heads = [k * NSA_GROUP + g for g in range(NSA_GROUP)]
    rows = NSA_GROUP * Q_BLOCK

    @pl.when(qb == 0)
    def _():
        ksb[0:Q_BLOCK, :] = jnp.zeros((Q_BLOCK, HEAD_DIM), BF16)
        ksb[Q_BLOCK:, :] = ks_ref[...].astype(BF16)
        vsb[0:Q_BLOCK, :] = jnp.zeros((Q_BLOCK, HEAD_DIM), BF16)
        vsb[Q_BLOCK:, :] = vs_ref[...].astype(BF16)
        kwb[0:WINDOW, :] = jnp.zeros((WINDOW, HEAD_DIM), BF16)
        kwb[WINDOW:, :] = kw_ref[...].astype(BF16)
        vwb[0:WINDOW, :] = jnp.zeros((WINDOW, HEAD_DIM), BF16)
        vwb[WINDOW:, :] = vw_ref[...].astype(BF16)
        kcb[...] = jnp.zeros(kcb.shape, BF16)
        vcb[...] = jnp.zeros(vcb.shape, BF16)
        kcb[0:nc, :] = kc_ref[...].astype(BF16)
        vcb[0:nc, :] = vc_ref[...].astype(BF16)
        blk = lax.broadcasted_iota(jnp.int32, ebuf.shape, 0)
        key = lax.broadcasted_iota(jnp.int32, ebuf.shape, 1) - Q_BLOCK
        ebuf[...] = ((key >= 0) & (lax.shift_right_logical(key, 6) == blk)).astype(BF16)
        ti = lax.broadcasted_iota(jnp.int32, (Q_BLOCK, Q_BLOCK), 0)
        tj = lax.broadcasted_iota(jnp.int32, (Q_BLOCK, Q_BLOCK), 1)
        b0 = _t5_bias(jnp.maximum(ti - tj, 0), tab_ref, g_heads)
        b1 = _t5_bias(Q_BLOCK + ti - tj, tab_ref, g_heads)
        tc = lax.broadcasted_iota(jnp.int32, (Q_BLOCK, ncp), 0)
        cc = lax.broadcasted_iota(jnp.int32, (Q_BLOCK, ncp), 1)
        dist_cn = jnp.where(cc < 4, tc + (CMP_BLOCK + 1) - CMP_BLOCK * cc, REL_MAX_DIST)
        bc = _t5_bias(jnp.maximum(dist_cn, 0), tab_ref, g_heads)
        for g in range(NSA_GROUP):
            far = tab_ref[REL_BUCKETS - 1, g_heads[g]]
            bnear[g * Q_BLOCK:(g + 1) * Q_BLOCK, 0:Q_BLOCK] = b1[g] - far
            bnear[g * Q_BLOCK:(g + 1) * Q_BLOCK, Q_BLOCK:] = b0[g] - far
            bcn[g * Q_BLOCK:(g + 1) * Q_BLOCK, :] = bc[g] - far

    def rep_g(x):
        return jnp.concatenate([x] * NSA_GROUP, axis=0)

    def fold_max(x):
        m = x[:, 0:LANES]
        for c in range(1, x.shape[1] // LANES):
            m = jnp.maximum(m, x[:, c * LANES:(c + 1) * LANES])
        return m

    def fold_sum(x):
        m = x[:, 0:LANES]
        for c in range(1, x.shape[1] // LANES):
            m = m + x[:, c * LANES:(c + 1) * LANES]
        return m

    t_col = lax.broadcasted_iota(jnp.int32, (Q_BLOCK, 1), 0)
    q_pos = qb * Q_BLOCK + t_col
    scale = HEAD_DIM ** -0.5
    q_all = jnp.concatenate(
        [(q_ref[:, g * HEAD_DIM:(g + 1) * HEAD_DIM] * scale).astype(BF16) for g in range(NSA_GROUP)], axis=0)
    gates = jax.nn.sigmoid(gt_ref[...])

    cidx = lax.broadcasted_iota(jnp.int32, (Q_BLOCK, ncp), 1)
    dist_c = q_pos - (cidx * CMP_BLOCK + (CMP_BLOCK - 1))
    mask_c = rep_g((dist_c >= 0) & (cidx < nc))
    bias_c = pltpu.roll(bcn[...], lax.rem(2 * qb - 2 + ncp, ncp), 1)
    e_c, den_c = _masked_softmax_unnorm(_dot_nt(q_all, kcb[...]) + bias_c, mask_c)
    p_c = e_c / den_c
    o_c = _dot(p_c.astype(BF16), vcb[...])
    imp = p_c[0:Q_BLOCK]
    for g in range(1, NSA_GROUP):
        imp = imp + p_c[g * Q_BLOCK:(g + 1) * Q_BLOCK]

    cur = lax.shift_right_logical(q_pos, 6)
    eligible = (cidx * CMP_BLOCK <= q_pos) & (cidx < nc)
    forced = (cidx == 0) | (cidx == cur) | (cidx == cur - 1)
    score = jnp.where(eligible, imp + jnp.where(forced, FORCE_SCORE, 0.0), -1.0)
    score = jnp.where(cidx < nc, score, -2.0)
    score_t = jnp.concatenate([score[:, c * LANES:(c + 1) * LANES].T for c in range(ncp // LANES)], axis=0)
    sct[...] = score_t[0:n8, :]
    sel_t = _select_blocks(sct, nc)
    if n8 < ncp:
        sel_t = jnp.concatenate([sel_t, jnp.zeros((ncp - n8, Q_BLOCK), F32)], axis=0)
    sel = jnp.concatenate([sel_t[c * LANES:(c + 1) * LANES, :].T for c in range(ncp // LANES)], axis=1)
    sel = sel.astype(BF16)

    near = 2 * Q_BLOCK
    n0 = pl.multiple_of(qb * Q_BLOCK, Q_BLOCK)
    jn = lax.broadcasted_iota(jnp.int32, (Q_BLOCK, near), 1)
    ok_near = (_dot(sel, ebuf[:, pl.ds(n0, near)]) > 0.5) & (jn - Q_BLOCK <= t_col)
    mb_near = rep_g(jnp.where(ok_near, 0.0, MASK_VALUE))
    s_near = _dot_nt(q_all, ksb[pl.ds(n0, near), :]) + bnear[...] + mb_near
    far_end = (qb - 1) * Q_BLOCK
    n_chunks = _divmod_const(jnp.maximum(far_end + (FAR_CHUNK - 1), 0), FAR_CHUNK)[0]
    mrun[...] = jnp.full(mrun.shape, MASK_VALUE, F32)

    def pass1(c, carry):
        c0 = pl.multiple_of(c * FAR_CHUNK, FAR_CHUNK)
        p0 = pl.multiple_of(c * FAR_CHUNK + Q_BLOCK, Q_BLOCK)
        pos = c * FAR_CHUNK + lax.broadcasted_iota(jnp.int32, (Q_BLOCK, FAR_CHUNK), 1)
        ok = (_dot(sel, ebuf[:, pl.ds(p0, FAR_CHUNK)]) > 0.5) & (pos < far_end)
        mb = jnp.where(ok, 0.0, MASK_VALUE)
        s = _dot_nt(q_all, ksb[pl.ds(p0, FAR_CHUNK), :])
        for g in range(NSA_GROUP):
            r = slice(g * Q_BLOCK, (g + 1) * Q_BLOCK)
            sg = s[r] + mb
            sbuf[r, pl.ds(c0, FAR_CHUNK)] = sg
            mrun[r, :] = jnp.maximum(mrun[r, :], fold_max(sg))
        return carry

    lax.fori_loop(0, n_chunks, pass1, 0)
    m_s = jnp.maximum(jnp.max(mrun[...], axis=-1, keepdims=True), jnp.max(s_near, axis=-1, keepdims=True))
    e_near = jnp.exp(s_near - m_s)
    mrun[...] = jnp.broadcast_to(m_s, mrun.shape)
    lrun[...] = fold_sum(e_near)
    acc[...] = _dot(e_near.astype(BF16), vsb[pl.ds(n0, near), :])

    def pass2(c, carry):
        c0 = pl.multiple_of(c * FAR_CHUNK, FAR_CHUNK)
        p0 = pl.multiple_of(c * FAR_CHUNK + Q_BLOCK, Q_BLOCK)
        m_full = mrun[...]
        e = jnp.exp(sbuf[:, pl.ds(c0, FAR_CHUNK)] - jnp.concatenate([m_full] * (FAR_CHUNK // LANES), axis=1))
        lrun[...] = lrun[...] + fold_sum(e)
        acc[...] = acc[...] + _dot(e.astype(BF16), vsb[pl.ds(p0, FAR_CHUNK), :])
        return carry

    lax.fori_loop(0, n_chunks, pass2, 0)
    o_s = acc[...] / jnp.sum(lrun[...], axis=-1, keepdims=True)

    wlen = WINDOW + Q_BLOCK
    jw = lax.broadcasted_iota(jnp.int32, (Q_BLOCK, wlen), 1)
    dist_w = WINDOW + t_col - jw
    ok_w = (dist_w >= 0) & (dist_w <= WINDOW) & (qb * Q_BLOCK - WINDOW + jw >= 0)
    mb_w = rep_g(jnp.where(ok_w, 0.0, MASK_VALUE))
    bias_w = jnp.concatenate([jnp.zeros((rows, wlen - near), F32), bnear[...]], axis=1)
    s_w = _dot_nt(q_all, kwb[pl.ds(n0, wlen), :]) + bias_w + mb_w
    e_w = jnp.exp(s_w - jnp.max(s_w, axis=-1, keepdims=True))
    o_w = _dot(e_w.astype(BF16), vwb[pl.ds(n0, wlen), :]) / jnp.sum(e_w, axis=-1, keepdims=True)

    for g in range(NSA_GROUP):
        r = slice(g * Q_BLOCK, (g + 1) * Q_BLOCK)
        c0 = g * 3
        y = gates[:, c0:c0 + 1] * o_c[r] + gates[:, c0 + 1:c0 + 2] * o_s[r] + gates[:, c0 + 2:c0 + 3] * o_w[r]
        o_ref[:, g * HEAD_DIM:(g + 1) * HEAD_DIM] = y.astype(o_ref.dtype)


def _nsa_prompt(z, comp, rel_bias, n_batch, seq, cols, out_buf):
    nqb = seq // Q_BLOCK
    nc = seq // CMP_BLOCK
    ncp = _round_up(nc, LANES)
    n8 = _round_up(nc, SUBLANES)
    kvc = cols["kv"] // HEAD_DIM
    winc = cols["win"] // HEAD_DIM
    kh = NSA_KV_HEADS
    gw = NSA_GROUP * HEAD_DIM
    rows = NSA_GROUP * Q_BLOCK
    assert seq % FAR_CHUNK == 0

    def zspec(col_block_of_k):
        return pl.BlockSpec((seq, HEAD_DIM), lambda b, k, q: (b, col_block_of_k(k)))

    return pl.pallas_call(
        functools.partial(_nsa_prompt_kernel, seq=seq),
        grid=(n_batch, kh, nqb),
        in_specs=[
            pl.BlockSpec(memory_space=pltpu.SMEM),
            pl.BlockSpec((Q_BLOCK, gw), lambda b, k, q: (b * nqb + q, cols["q"] // gw + k)),
            zspec(lambda k: kvc + 2 * kh + k),
            zspec(lambda k: kvc + 3 * kh + k),
            zspec(lambda k: winc + k),
            zspec(lambda k: winc + kh + k),
            pl.BlockSpec((Q_BLOCK, LANES), lambda b, k, q: (b * nqb + q, cols["ng"] // LANES + k)),
            pl.BlockSpec((nc, HEAD_DIM), lambda b, k, q: (b, k)),
            pl.BlockSpec((nc, HEAD_DIM), lambda b, k, q: (b, kh + k)),
            pl.BlockSpec(memory_space=pl.ANY),
        ],
        out_specs=pl.BlockSpec((Q_BLOCK, gw), lambda b, k, q: (b * nqb + q, k)),
        out_shape=jax.ShapeDtypeStruct(out_buf.shape, out_buf.dtype),
        input_output_aliases={9: 0},
        scratch_shapes=[
            pltpu.VMEM((seq + Q_BLOCK, HEAD_DIM), BF16), pltpu.VMEM((seq + Q_BLOCK, HEAD_DIM), BF16),
            pltpu.VMEM((seq + WINDOW, HEAD_DIM), BF16), pltpu.VMEM((seq + WINDOW, HEAD_DIM), BF16),
            pltpu.VMEM((ncp, HEAD_DIM), BF16), pltpu.VMEM((ncp, HEAD_DIM), BF16),
            pltpu.VMEM((ncp, seq + Q_BLOCK), BF16),
            pltpu.VMEM((rows, 2 * Q_BLOCK), F32),
            pltpu.VMEM((rows, ncp), F32),
            pltpu.VMEM((n8, Q_BLOCK), F32),
            pltpu.VMEM((rows, seq), F32),
            pltpu.VMEM((rows, LANES), F32), pltpu.VMEM((rows, LANES), F32), pltpu.VMEM((rows, LANES), F32),
        ],
        compiler_params=_cparams(("arbitrary", "arbitrary", "arbitrary")),
        name="nsa_prompt",
    )(rel_bias, z, z, z, z, z, z, comp, comp, out_buf)


def _nsa_sample_kernel(pt_ref, tab_ref, q_ref, kvn_ref, wn_ref, gt_ref, win_hbm, pages_hbm, comp_hbm, _out_buf,
                       _win_buf, o_ref, wo_ref,
                       pbuf, cbuf, wbuf, ebuf, psem, csem, wsem, *, n_seq, n_pages, page0, layer, tn):
    b = pl.program_id(0)
    slot = lax.rem(b, 2)
    past = n_pages * 2 * CMP_BLOCK
    ncs = 2 * n_pages
    ns = ncs + 1
    ncp = ebuf.shape[0]
    nkeys = past + LANES
    wb = wbuf.shape[2]
    n_slc = 2 * NSA_KV_HEADS

    def block_of(pos):
        half, page = _divmod_const(pos, n_pages)
        return jnp.where(pos < ncs, 2 * page + half, pos)

    def page_copies(seq_idx, dst_slot, j):
        pid = pt_ref[seq_idx * n_pages + j]
        slc = tuple(
            pltpu.make_async_copy(pages_hbm.at[page0 + pid, :, n_slc + i, :], pbuf.at[dst_slot, i, j],
                                  psem.at[dst_slot])
            for i in range(n_slc))
        return slc + (pltpu.make_async_copy(comp_hbm.at[pid], cbuf.at[dst_slot, j], csem.at[dst_slot]),)

    def window_copies(seq_idx, dst_slot):
        return tuple(
            pltpu.make_async_copy(win_hbm.at[layer, seq_idx, :, i, :], wbuf.at[dst_slot, i], wsem.at[dst_slot])
            for i in range(n_slc))

    def start_seq(seq_idx, dst_slot):
        def body(j, c):
            for cp in page_copies(seq_idx, dst_slot, j):
                cp.start()
            return c
        lax.fori_loop(0, n_pages, body, 0)
        for cp in window_copies(seq_idx, dst_slot):
            cp.start()

    def wait_seq(seq_idx, dst_slot):
        def body(j, c):
            for cp in page_copies(seq_idx, dst_slot, j):
                cp.wait()
            return c
        lax.fori_loop(0, n_pages, body, 0)
        for cp in window_copies(seq_idx, dst_slot):
            cp.wait()

    @pl.when(b == 0)
    def _():
        start_seq(0, 0)
        pos = lax.broadcasted_iota(jnp.int32, ebuf.shape, 0)
        key = lax.broadcasted_iota(jnp.int32, ebuf.shape, 1)
        ebuf[...] = ((lax.shift_right_logical(key, 6) == block_of(pos)) & (pos <= ncs)).astype(BF16)

    @pl.when(b + 1 < n_seq)
    def _():
        start_seq(b + 1, 1 - slot)

    wait_seq(b, slot)

    t_col = lax.broadcasted_iota(jnp.int32, (tn, 1), 0)
    q_pos = past + t_col
    scale = HEAD_DIM ** -0.5
    rows = NSA_GROUP * tn

    def rep_g(x):
        return jnp.concatenate([x] * NSA_GROUP, axis=0)

    def pad_rows(x, n):
        if x.shape[0] == n:
            return x
        return jnp.concatenate([x, jnp.zeros((n - x.shape[0], x.shape[1]), x.dtype)], axis=0)

    pos1 = lax.broadcasted_iota(jnp.int32, (tn, ncp), 1)
    blk1 = block_of(pos1)
    dist_c = q_pos - (blk1 * CMP_BLOCK + (CMP_BLOCK - 1))
    mask_c = rep_g((dist_c >= 0) & (pos1 < ncs))
    cur = lax.shift_right_logical(q_pos, 6)
    eligible = (blk1 * CMP_BLOCK <= q_pos) & (pos1 <= ncs)
    forced = (blk1 == 0) | (blk1 == cur) | (blk1 == cur - 1)

    qk, o_cmp, scores = [], [], []
    for k in range(NSA_KV_HEADS):
        heads = [k * NSA_GROUP + g for g in range(NSA_GROUP)]
        q = jnp.concatenate(
            [q_ref[:, (k * NSA_GROUP + g) * HEAD_DIM:(k * NSA_GROUP + g + 1) * HEAD_DIM] for g in range(NSA_GROUP)],
            axis=0)
        q = (q * scale).astype(BF16)
        qk.append(q)
        kc = jnp.concatenate([cbuf[slot, :, half * SUBLANES + k, 0:HEAD_DIM] for half in range(2)], axis=0)
        vc = jnp.concatenate([cbuf[slot, :, half * SUBLANES + NSA_KV_HEADS + k, HEAD_DIM:] for half in range(2)],
                             axis=0)
        kc = pad_rows(kc, ncp).astype(BF16)
        vc = pad_rows(vc, ncp).astype(BF16)
        bias_c = jnp.concatenate(_t5_bias(jnp.maximum(dist_c, 0), tab_ref, heads), axis=0)
        s = _dot_nt(q, kc) + bias_c
        e, den = _masked_softmax_unnorm(s, mask_c)
        p = e / den
        o_cmp.append(_dot(p.astype(BF16), vc))
        imp = p[0:tn]
        for g in range(1, NSA_GROUP):
            imp = imp + p[g * tn:(g + 1) * tn]
        sc = jnp.where(eligible, imp + jnp.where(forced, FORCE_SCORE, 0.0), -1.0)
        scores.append(jnp.where(pos1 <= ncs, sc, -2.0))

    def block_of_static(p):
        return 2 * (p % n_pages) + p // n_pages if p < ncs else p

    blk_lane = block_of(lax.broadcasted_iota(jnp.int32, (1, ncp), 1))
    sel = _select_blocks_rows(jnp.concatenate(scores, axis=0), blk_lane, ns, block_of_static)

    kpos = lax.broadcasted_iota(jnp.int32, (tn, nkeys), 1)
    dist_s = q_pos - kpos
    causal_s = rep_g((dist_s >= 0) & (kpos < past + tn))
    near = 2 * LANES
    dist_near = jnp.maximum(dist_s[:, nkeys - near:], 0)
    jw = lax.broadcasted_iota(jnp.int32, (tn, wb + LANES), 1)
    dist_w = wb + t_col - jw
    mask_w = rep_g((dist_w >= 0) & (dist_w <= WINDOW) & (past - wb + jw >= 0) & (jw < wb + tn))
    dist_w_near = jnp.maximum(dist_w[:, wb + LANES - near:], 0)
    gates = jax.nn.sigmoid(gt_ref[...])

    for k in range(NSA_KV_HEADS):
        heads = [k * NSA_GROUP + g for g in range(NSA_GROUP)]
        far = jnp.concatenate([jnp.full((tn, 1), tab_ref[REL_BUCKETS - 1, h], F32) for h in heads], axis=0)
        q = qk[k]
        k_new = pad_rows(kvn_ref[:, (2 * NSA_KV_HEADS + k) * HEAD_DIM:(2 * NSA_KV_HEADS + k + 1) * HEAD_DIM], LANES)
        v_new = pad_rows(kvn_ref[:, (3 * NSA_KV_HEADS + k) * HEAD_DIM:(3 * NSA_KV_HEADS + k + 1) * HEAD_DIM], LANES)
        k_all = jnp.concatenate([pbuf[slot, k].reshape(past, HEAD_DIM), k_new], axis=0).astype(BF16)
        v_all = jnp.concatenate([pbuf[slot, NSA_KV_HEADS + k].reshape(past, HEAD_DIM), v_new], axis=0).astype(BF16)
        sel_k = rep_g(sel[k * tn:(k + 1) * tn, :]).astype(BF16)
        mask_s = (_dot(sel_k, ebuf[...]) > 0.5) & causal_s
        b_near = jnp.concatenate(_t5_bias(dist_near, tab_ref, heads), axis=0) - far
        bias_s = jnp.concatenate([jnp.zeros((rows, nkeys - near), F32), b_near], axis=1)
        s = _dot_nt(q, k_all) + far + bias_s
        e, den = _masked_softmax_unnorm(s, mask_s)
        o_s = _dot(e.astype(BF16), v_all) / den
        kw = jnp.concatenate([wbuf[slot, k],
                              pad_rows(wn_ref[:, k * HEAD_DIM:(k + 1) * HEAD_DIM], LANES)], axis=0).astype(BF16)
        vw = jnp.concatenate(
            [wbuf[slot, NSA_KV_HEADS + k],
             pad_rows(wn_ref[:, (NSA_KV_HEADS + k) * HEAD_DIM:(NSA_KV_HEADS + k + 1) * HEAD_DIM], LANES)],
            axis=0).astype(BF16)
        bw_near = jnp.concatenate(_t5_bias(dist_w_near, tab_ref, heads), axis=0) - far
        bias_w = jnp.concatenate([jnp.zeros((rows, wb + LANES - near), F32), bw_near], axis=1)
        sw = _dot_nt(q, kw) + far + bias_w
        ew, denw = _masked_softmax_unnorm(sw, mask_w)
        o_w = _dot(ew.astype(BF16), vw) / denw
        for g in range(NSA_GROUP):
            c0 = k * LANES + g * 3
            r = slice(g * tn, (g + 1) * tn)
            y = (gates[:, c0:c0 + 1] * o_cmp[k][r] + gates[:, c0 + 1:c0 + 2] * o_s[r]
                 + gates[:, c0 + 2:c0 + 3] * o_w[r])
            h = k * NSA_GROUP + g
            o_ref[:, h * HEAD_DIM:(h + 1) * HEAD_DIM] = y.astype(o_ref.dtype)

    for i in range(n_slc):
        wo_ref[0, pl.ds(i, wb - tn, stride=n_slc), :] = wbuf[slot, i, tn:wb, :]
        wo_ref[0, pl.ds((wb - tn) * n_slc + i, tn, stride=n_slc), :] = wn_ref[:, i * HEAD_DIM:(i + 1) * HEAD_DIM]


def _nsa_sample(z, row0, n_seq, tn, pages, page0, comp_pages, cache_win_l, layer, page_table, rel_bias, cols,
                out_buf, win_buf):
    n_pages = page_table.shape[1]
    page_size = pages.shape[1]
    assert page_size == 2 * CMP_BLOCK and tn == SUBLANES and row0 % tn == 0
    past = n_pages * page_size
    ncs = 2 * n_pages
    ncp = _round_up(ncs + 1, LANES)
    wb = cache_win_l.shape[2]
    assert wb == WINDOW and past >= WINDOW
    half_cols = 2 * NSA_KV_HEADS * HEAD_DIM
    n_slc = 2 * NSA_KV_HEADS
    rb = row0 // tn
    grid_spec = pltpu.PrefetchScalarGridSpec(
        num_scalar_prefetch=1,
        grid=(n_seq,),
        in_specs=[
            pl.BlockSpec(memory_space=pltpu.SMEM),
            pl.BlockSpec((tn, NSA_WIDTH), lambda b, pt: (rb + b, cols["q"] // NSA_WIDTH)),
            pl.BlockSpec((tn, 2 * half_cols), lambda b, pt: (rb + b, cols["kv"] // (2 * half_cols))),
            pl.BlockSpec((tn, half_cols), lambda b, pt: (rb + b, cols["win"] // half_cols)),
            pl.BlockSpec((tn, 2 * LANES), lambda b, pt: (rb + b, cols["ng"] // (2 * LANES))),
            pl.BlockSpec(memory_space=pl.ANY),
            pl.BlockSpec(memory_space=pl.ANY),
            pl.BlockSpec(memory_space=pl.ANY),
            pl.BlockSpec(memory_space=pl.ANY),
            pl.BlockSpec(memory_space=pl.ANY),
        ],
        out_specs=[pl.BlockSpec((tn, NSA_WIDTH), lambda b, pt: (rb + b, 0)),
                   pl.BlockSpec((1, wb * n_slc, HEAD_DIM), lambda b, pt: (layer * n_seq + b, 0, 0))],
        scratch_shapes=[
            pltpu.VMEM((2, n_slc, n_pages, page_size, HEAD_DIM), F32),
            pltpu.VMEM((2, n_pages) + comp_pages.shape[1:], F32),
            pltpu.VMEM((2, n_slc, wb, HEAD_DIM), F32),
            pltpu.VMEM((ncp, past + LANES), BF16),
            pltpu.SemaphoreType.DMA((2,)),
            pltpu.SemaphoreType.DMA((2,)),
            pltpu.SemaphoreType.DMA((2,)),
        ],
    )
    return pl.pallas_call(
        functools.partial(_nsa_sample_kernel, n_seq=n_seq, n_pages=n_pages, page0=page0, layer=layer, tn=tn),
        grid_spec=grid_spec,
        out_shape=[jax.ShapeDtypeStruct(out_buf.shape, out_buf.dtype),
                   jax.ShapeDtypeStruct(win_buf.shape, win_buf.dtype)],
        input_output_aliases={9: 0, 10: 1},
        compiler_params=_cparams(("arbitrary",)),
        name="nsa_sample",
    )(page_table.reshape(-1), rel_bias, z, z, z, z, cache_win_l, pages, comp_pages, out_buf, win_buf)


def _cumsum8(x):
    row = lax.broadcasted_iota(jnp.int32, x.shape, 0)
    for sh in (1, 2, 4):
        x = x + jnp.where(row >= sh, pltpu.roll(x, sh, 0), 0.0)
    return x


def _mixer_kernel(zcb_ref, zcc_ref, zch_ref, zhg_ref, cw_ref, cs_ref, hs_ref, lb_ref, hn_ref, _yb_buf, _yc_buf,
                  yb_ref, yc_ref, cso_ref, hso_ref, uprev, st, qf_s, k_s, lf_s, *, layer, tt):
    ti = pl.program_id(1)
    nt = pl.num_programs(1)

    @pl.when(ti == 0)
    def _():
        uprev[...] = jnp.zeros(uprev.shape, F32)
        uprev[SUBLANES - (CONV_K - 1):SUBLANES, :] = cs_ref[0]
        for h in range(HG_HEADS):
            st[h] = hs_ref[0, h].T

    cb = zcb_ref[...]
    u = zcc_ref[...] * zch_ref[...]
    ext = jnp.concatenate([uprev[...], u], axis=0)
    y = cw_ref[CONV_K - 1:CONV_K, :] * u
    for j in range(CONV_K - 1):
        shift = CONV_K - 1 - j
        y = y + cw_ref[j:j + 1, :] * pltpu.roll(ext, shift, 0)[SUBLANES:, :]
    yb_ref[...] = (cb * y).astype(yb_ref.dtype)
    tail = ext[tt:tt + SUBLANES, :]
    uprev[...] = tail

    @pl.when(ti == nt - 1)
    def _():
        cso_ref[0] = pltpu.roll(tail, CONV_K - 1, 0)[0:CONV_K - 1, :]

    hw = HG_HEADS * HG_DK
    lg = lb_ref[...]
    mx = jnp.max(lg, axis=0, keepdims=True)
    ex = jnp.exp(lg - mx)
    sm = ex / jnp.sum(ex, axis=0, keepdims=True)
    cs = sm[0:1]
    for i in range(1, layer + 1):
        cs = cs + sm[i:i + 1]
    lb = cs - sm[0:1]
    fx = zhg_ref[:, hw:2 * hw]
    lb_pos = lb > 0.0
    log_lb = jnp.log(jnp.where(lb_pos, lb, 1.0))
    log_sig = jnp.minimum(fx, 0.0) - jnp.log1p(jnp.exp(-jnp.abs(fx)))
    a2 = jnp.log1p(-lb) + log_sig
    lae = jnp.maximum(log_lb, a2) + jnp.log1p(jnp.exp(-jnp.abs(log_lb - a2)))
    lf_s[...] = jnp.where(lb_pos, lae, log_sig)
    k_s[...] = (1.0 - lb) * jax.nn.sigmoid(-fx)
    hq = zhg_ref[:, 0:hw]
    qf_s[...] = hq * jax.nn.sigmoid(hq)
    norm_g = hn_ref[...]
    ts = lax.broadcasted_iota(jnp.int32, (HG_BLOCK, HG_BLOCK, 1), 0)
    ss = lax.broadcasted_iota(jnp.int32, (HG_BLOCK, HG_BLOCK, 1), 1)
    causal = ss <= ts

    def block(i, c):
        r0 = pl.multiple_of(i * HG_BLOCK, HG_BLOCK)
        rs = pl.ds(r0, HG_BLOCK)
        for h in range(HG_HEADS):
            kc = slice(h * HG_DK, (h + 1) * HG_DK)
            vcol = slice(2 * hw + h * HG_DV, 2 * hw + (h + 1) * HG_DV)
            gcol = slice(2 * hw + HG_HEADS * HG_DV + h * HG_DV, 2 * hw + HG_HEADS * HG_DV + (h + 1) * HG_DV)
            qf = qf_s[rs, kc]
            kk = k_s[rs, kc]
            g = _cumsum8(lf_s[rs, kc])
            v = zhg_ref[rs, vcol]
            gl = g[HG_BLOCK - 1:HG_BLOCK, :]
            st_h = st[h]
            o = _dot_nt((qf * jnp.exp(g)).astype(BF16), st_h.astype(BF16))
            diff = jnp.where(causal, g[:, None, :] - g[None, :, :], 0.0)
            x3 = jnp.where(causal, qf[:, None, :] * kk[None, :, :] * jnp.exp(diff), 0.0)
            a = jnp.sum(x3, axis=-1, keepdims=True)
            o = o + jnp.sum(a * v[None, :, :], axis=1)
            kd = kk * jnp.exp(gl - g)
            st[h] = st_h * jnp.exp(gl) + _dot_tn(v.astype(BF16), kd.astype(BF16))
            on = o * lax.rsqrt(jnp.mean(o * o, axis=-1, keepdims=True) + NORM_EPS) * norm_g
            gx = zhg_ref[rs, gcol]
            yc_ref[rs, h * HG_DV:(h + 1) * HG_DV] = (on * (gx * jax.nn.sigmoid(gx))).astype(yc_ref.dtype)
        return c

    n_blk = tt // HG_BLOCK
    lax.fori_loop(0, n_blk, block, 0, unroll=min(4, n_blk))

    @pl.when(ti == nt - 1)
    def _():
        for h in range(HG_HEADS):
            hso_ref[0, h] = st[h].T


def _mixer(z, row0, n_seq, seq, conv_w_l, conv_state, hg_state, lb_logits, hg_norm_l, layer, cols, yb_buf, yc_buf):
    cw = conv_w_l.shape[-1]
    hgw = 2 * HG_HEADS * HG_DK + 2 * HG_HEADS * HG_DV
    tt = _pick_tile(seq, 256, SUBLANES)
    nt = seq // tt
    rb = row0 // tt
    assert row0 % tt == 0 and cols["cv"] % cw == 0 and cols["hg"] % hgw == 0
    depth = lb_logits.shape[0]
    cvb = cols["cv"] // cw
    return pl.pallas_call(
        functools.partial(_mixer_kernel, layer=layer, tt=tt),
        grid=(n_seq, nt),
        in_specs=[
            pl.BlockSpec((tt, cw), lambda b, t: (rb + b * nt + t, cvb)),
            pl.BlockSpec((tt, cw), lambda b, t: (rb + b * nt + t, cvb + 1)),
            pl.BlockSpec((tt, cw), lambda b, t: (rb + b * nt + t, cvb + 2)),
            pl.BlockSpec((tt, hgw), lambda b, t: (rb + b * nt + t, cols["hg"] // hgw)),
            pl.BlockSpec((CONV_K, cw), lambda b, t: (0, 0)),
            pl.BlockSpec((1, CONV_K - 1, cw), lambda b, t: (b, 0, 0)),
            pl.BlockSpec((1, HG_HEADS, HG_DK, HG_DV), lambda b, t: (b, 0, 0, 0)),
            pl.BlockSpec((depth, HG_HEADS * HG_DK), lambda b, t: (0, 0)),
            pl.BlockSpec((1, HG_DV), lambda b, t: (0, 0)),
            pl.BlockSpec(memory_space=pl.ANY),
            pl.BlockSpec(memory_space=pl.ANY),
        ],
        out_specs=[
            pl.BlockSpec((tt, cw), lambda b, t: (rb + b * nt + t, 0)),
            pl.BlockSpec((tt, HG_HEADS * HG_DV), lambda b, t: (rb + b * nt + t, 0)),
            pl.BlockSpec((1, CONV_K - 1, cw), lambda b, t: (b, 0, 0)),
            pl.BlockSpec((1, HG_HEADS, HG_DK, HG_DV), lambda b, t: (b, 0, 0, 0)),
        ],
        out_shape=[
            jax.ShapeDtypeStruct(yb_buf.shape, yb_buf.dtype),
            jax.ShapeDtypeStruct(yc_buf.shape, yc_buf.dtype),
            jax.ShapeDtypeStruct((n_seq, CONV_K - 1, cw), F32),
            jax.ShapeDtypeStruct((n_seq, HG_HEADS, HG_DK, HG_DV), F32),
        ],
        input_output_aliases={9: 0, 10: 1},
        scratch_shapes=[
            pltpu.VMEM((SUBLANES, cw), F32),
            pltpu.VMEM((HG_HEADS, HG_DV, HG_DK), F32),
            pltpu.VMEM((tt, HG_HEADS * HG_DK), F32),
            pltpu.VMEM((tt, HG_HEADS * HG_DK), F32),
            pltpu.VMEM((tt, HG_HEADS * HG_DK), F32),
        ],
        compiler_params=_cparams(("arbitrary", "arbitrary")),
        name="mixer_conv_hgrn",
    )(z, z, z, z, conv_w_l, conv_state, hg_state, lb_logits, hg_norm_l.reshape(1, HG_DV), yb_buf, yc_buf)


def _ep_identity(prods, extras):
    return prods[0]


def _ep_mix(prods, extras):
    ga, gb, gc = (jax.nn.sigmoid(e) for e in extras)
    return ga * prods[0] + gb * prods[1] + gc * prods[2]


def _ep_residual(prods, extras):
    return extras[0] + prods[0]


def _ep_swiglu(prods, extras):
    return prods[0] * jax.nn.sigmoid(prods[0]) * prods[1]


def _ep_ple(prods, extras):
    return extras[0] + jax.nn.sigmoid(prods[0]) * prods[1]


def kernel(x_prompt, x_sample, p_prompt, p_sample, cache_kv, cache_win, state_conv, state_hgrn, page_table,
           rel_bias, g_mix, w_in, phi_pe, phi_w1, phi_w2, conv_w, hg_lb_logits, hg_norm, w_pa, w_pb, w_pc, w_o,
           g_ffn, w_gate, w_up, w_down, w_pg, w_pp, g_final):
    bp, seq, d = x_prompt.shape
    bs, tn, _ = x_sample.shape
    depth = w_in.shape[0]
    n_in = w_in.shape[-1]
    cw = conv_w.shape[-1]
    d_ff = w_gate.shape[-1]
    ple = w_pp.shape[1]
    n_pool, page_size = cache_kv.shape[1], cache_kv.shape[2]
    mp, ms = bp * seq, bs * tn
    m = mp + ms
    kvw = 6 * NSA_KV_HEADS * HEAD_DIM
    hgw = 2 * HG_HEADS * HG_DK + 2 * HG_HEADS * HG_DV
    assert n_in == NSA_WIDTH + kvw + N_GATES + 3 * cw + hgw + 3 * d
    assert seq % Q_BLOCK == 0 and seq >= WINDOW

    o_ng = NSA_WIDTH + kvw
    cols = {"q": 0, "kv": NSA_WIDTH, "win": NSA_WIDTH + N_KV_ROWS * NSA_KV_HEADS * HEAD_DIM,
            "cv": o_ng, "hg": o_ng + 3 * cw, "mg": o_ng + 3 * cw + hgw, "ng": n_in - N_GATES}
    nz = cols["ng"] + GATE_PAD

    tm = _pick_tile(m, 1024, 256) if m % 256 == 0 else _pick_tile(m, 1024, SUBLANES)
    h = jnp.concatenate([x_prompt.reshape(mp, d), x_sample.reshape(ms, d)], axis=0)
    n_rows = N_KV_ROWS * NSA_KV_HEADS
    pages = cache_kv.reshape(depth * n_pool, page_size, n_rows, HEAD_DIM)
    pool_blocks = cache_kv.reshape(depth * n_pool * (page_size // CMP_BLOCK), CMP_BLOCK * n_rows, HEAD_DIM)
    nb_pool = n_pool * (page_size // CMP_BLOCK)
    cache_win_r = cache_win.reshape(depth, bs, cache_win.shape[2], 2 * NSA_KV_HEADS, HEAD_DIM)
    zeros_conv = jnp.zeros((bp, CONV_K - 1, cw), F32)
    zeros_hg = jnp.zeros((bp, HG_HEADS, HG_DK, HG_DV), F32)

    def w2d(w):
        return w.reshape(w.shape[0] * w.shape[1], w.shape[2])

    w_in_t = jnp.swapaxes(w_in, 1, 2)
    keep = min(WINDOW, seq)
    kv_p = jnp.zeros((depth * mp * n_rows, HEAD_DIM), F32)
    kv_s = jnp.zeros((depth * ms * n_rows, HEAD_DIM), F32)
    win_p = jnp.zeros((depth * bp * keep * (n_rows // 2), HEAD_DIM), F32)
    wb = cache_win.shape[2]
    win_s = jnp.zeros((depth * bs, wb * (n_rows // 2), HEAD_DIM), F32)
    conv_p, conv_s, hg_p, hg_s = [], [], [], []
    for l in range(depth):
        xn = _rmsnorm(h, g_mix[l], BF16)
        z = _fused_matmul([xn], [0], [(_w_in_relayout(w_in_t, l, o_ng, nz), 0)], [], _ep_identity, nz, F32,
                          tm, _pick_tile(nz, 2560, 2 * LANES), "in_proj", w_rows_out=True)

        comp_prompt = _compress(z, mp // CMP_BLOCK, cols["kv"] // HEAD_DIM, phi_pe[l], phi_w1[l], phi_w2[l],
                                _pick_tile(mp // CMP_BLOCK, 256, SUBLANES), "compress_prompt")
        cb_pool = _pick_tile(nb_pool, 64, SUBLANES)
        comp_pool = _compress_rows(pool_blocks, l * nb_pool // cb_pool, nb_pool, phi_pe[l], phi_w1[l], phi_w2[l],
                                   cb_pool, "compress_pool")
        comp_pages = comp_pool.reshape(n_pool, (page_size // CMP_BLOCK) * n_rows, 2 * HEAD_DIM)
        ya = _nsa_prompt(z, comp_prompt, rel_bias, bp, seq, cols, jnp.zeros((m, NSA_WIDTH), F32))
        ya, win_s = _nsa_sample(z, mp, bs, tn, pages, l * n_pool, comp_pages, cache_win_r, l, page_table,
                                rel_bias, cols, ya, win_s)
        yb, yc, cs_p, hs_p = _mixer(z, 0, bp, seq, conv_w[l], zeros_conv, zeros_hg, hg_lb_logits,
                                    hg_norm[l], l, cols, jnp.zeros((m, cw), F32),
                                    jnp.zeros((m, HG_HEADS * HG_DV), F32))
        yb, yc, cs_s, hs_s = _mixer(z, mp, bs, tn, conv_w[l], state_conv[l], state_hgrn[l], hg_lb_logits,
                                    hg_norm[l], l, cols, yb, yc)

        tn_d = _pick_tile(d, 1024, LANES)
        tn_mix = _pick_tile(d, 512, LANES)
        mgb = cols["mg"] // tn_mix
        mixed = _fused_matmul([ya, yb, yc], [0, 1, 2], [(w2d(w_pa), l), (w2d(w_pb), l), (w2d(w_pc), l)],
                              [(z, mgb), (z, mgb + d // tn_mix), (z, mgb + 2 * (d // tn_mix))],
                              _ep_mix, d, BF16, tm, tn_mix, "mix_proj")
        h = _fused_matmul([mixed], [0], [(w2d(w_o), l)], [(h, 0)], _ep_residual, d, F32, tm, tn_d, "out_proj")
        xf = _rmsnorm(h, g_ffn[l], BF16)
        act = _fused_matmul([xf], [0, 0], [(w2d(w_gate), l), (w2d(w_up), l)], [], _ep_swiglu, d_ff, BF16,
                            tm, _pick_tile(d_ff, 512, LANES), "ffn_up")
        h = _fused_matmul([act], [0], [(w_down[l].astype(BF16), 0)], [(h, 0)], _ep_residual, d, F32,
                          tm, _pick_tile(d, 512, LANES), "ffn_down")
        p_all = jnp.concatenate([p_prompt[l].reshape(mp, ple), p_sample[l].reshape(ms, ple)], axis=0)
        h = _fused_matmul([h, p_all], [0, 1], [(w2d(w_pg), l), (w2d(w_pp), l)], [(h, 0)], _ep_ple, d, F32,
                          tm, tn_d, "ple_gate")

        kv_p = _rows_out(z, cols["kv"], n_rows, 1, mp, lambda s: 0, kv_p, l, "kv_rows_prompt")
        kv_s = _rows_out(z, cols["kv"], n_rows, 1, ms, lambda s: mp, kv_s, l, "kv_rows_sample")
        win_p = _rows_out(z, cols["win"], n_rows // 2, bp, keep, lambda s: s * seq + seq - keep, win_p, l,
                          "win_rows_prompt")
        conv_p.append(cs_p); conv_s.append(cs_s); hg_p.append(hs_p); hg_s.append(hs_s)

    y = _rmsnorm(h, g_final, F32)
    return (y[:mp].reshape(bp, seq, d), y[mp:].reshape(bs, tn, d),
            kv_p.reshape(depth, bp, seq, N_KV_ROWS, NSA_KV_HEADS, HEAD_DIM),
            kv_s.reshape(depth, bs, tn, N_KV_ROWS, NSA_KV_HEADS, HEAD_DIM),
            win_p.reshape(depth, bp, keep, 2, NSA_KV_HEADS, HEAD_DIM),
            win_s.reshape(depth, bs, wb, 2, NSA_KV_HEADS, HEAD_DIM), jnp.stack(conv_p), jnp.stack(conv_s),
            jnp.stack(hg_p), jnp.stack(hg_s))
```

```python
import functools
import math

import numpy as np
import jax
import jax.numpy as jnp
from jax import lax
from jax.experimental import pallas as pl
from jax.experimental.pallas import tpu as pltpu

F32 = jnp.float32
BF16 = jnp.bfloat16

LANES = 128
SUBLANES = 8
VMEM_LIMIT_BYTES = 56 * 1024 * 1024

NSA_HEADS = 8
NSA_KV_HEADS = 2
NSA_GROUP = NSA_HEADS // NSA_KV_HEADS
HEAD_DIM = 128
NSA_WIDTH = NSA_HEADS * HEAD_DIM
CMP_BLOCK = 64
N_SELECT = 16
WINDOW = 512
Q_BLOCK = 128
FAR_CHUNK = 1024
FORCE_SCORE = 1.0e4
MASK_VALUE = -1.0e30
N_KV_ROWS = 4
CONV_K = 3
HG_HEADS = 4
HG_DK = 128
HG_DV = 128
HG_BLOCK = SUBLANES
REL_BUCKETS = 32
REL_MAX_DIST = 128
NORM_EPS = 1e-6
N_GATES = 3 * NSA_HEADS
GATE_PAD = 4 * LANES


def _round_up(x, m):
    return -(-x // m) * m


def _pick_tile(n, pref, unit):
    if n <= pref:
        return n
    best = None
    for t in range(unit, pref + 1, unit):
        if n % t == 0:
            best = t
    assert best is not None, (n, pref, unit)
    return best


def _divmod_const(x, n):
    if n & (n - 1) == 0:
        return lax.shift_right_logical(x, n.bit_length() - 1), x & (n - 1)
    return x // n, lax.rem(x, n)


def _t5_thresholds():
    n = np.arange(0, REL_MAX_DIST + 1)
    max_exact = REL_BUCKETS // 2
    nf = np.maximum(n, 1).astype(np.float32)
    ratio = np.log(nf / np.float32(max_exact)) / np.float32(math.log(REL_MAX_DIST / max_exact))
    large = max_exact + (ratio * np.float32(REL_BUCKETS - max_exact)).astype(np.int32)
    large = np.minimum(large, REL_BUCKETS - 1)
    bucket = np.where(n < max_exact, n, large)
    assert np.all(np.diff(bucket) >= 0) and bucket[-1] == REL_BUCKETS - 1
    return [int(np.argmax(bucket >= j)) for j in range(1, REL_BUCKETS)]


T5_THRESHOLDS = _t5_thresholds()


def _t5_bias(dist, tab_ref, heads):
    ge = [dist >= thr for thr in T5_THRESHOLDS]
    out = []
    for h in heads:
        b = jnp.full(dist.shape, tab_ref[0, h], F32)
        for j, m in enumerate(ge):
            b = jnp.where(m, tab_ref[j + 1, h], b)
        out.append(b)
    return out


def _dot(a, b):
    return jnp.dot(a, b, preferred_element_type=F32)


def _dot_nt(a, b):
    return lax.dot_general(a, b, (((1,), (1,)), ((), ())), preferred_element_type=F32)


def _dot_tn(a, b):
    return lax.dot_general(a, b, (((0,), (0,)), ((), ())), preferred_element_type=F32)


def _cparams(sem):
    return pltpu.CompilerParams(dimension_semantics=sem, vmem_limit_bytes=VMEM_LIMIT_BYTES)


def _rmsnorm_kernel(x_ref, g_ref, o_ref):
    x = x_ref[...]
    y = x * lax.rsqrt(jnp.mean(x * x, axis=-1, keepdims=True) + NORM_EPS)
    o_ref[...] = (y * g_ref[...]).astype(o_ref.dtype)


def _rmsnorm(x, g, out_dtype):
    m, d = x.shape
    tm = _pick_tile(m, 1024, SUBLANES)
    return pl.pallas_call(
        _rmsnorm_kernel,
        grid=(m // tm,),
        in_specs=[pl.BlockSpec((tm, d), lambda i: (i, 0)), pl.BlockSpec((1, d), lambda i: (0, 0))],
        out_specs=pl.BlockSpec((tm, d), lambda i: (i, 0)),
        out_shape=jax.ShapeDtypeStruct((m, d), out_dtype),
        compiler_params=_cparams(("parallel",)),
        name="rmsnorm",
    )(x, g.reshape(1, d))


def _mm_kernel(*refs, n_x, pairs, n_extra, epilogue, cast_w, w_rows_out):
    xs = refs[:n_x]
    ws = refs[n_x:n_x + len(pairs)]
    extras = refs[n_x + len(pairs):n_x + len(pairs) + n_extra]
    o_ref = refs[n_x + len(pairs) + n_extra]
    wscr = refs[n_x + len(pairs) + n_extra + 1:]
    i = pl.program_id(1)

    if any(cast_w):
        @pl.when(i == 0)
        def _():
            s = 0
            for p in range(len(pairs)):
                if cast_w[p]:
                    wscr[s][...] = ws[p][...].astype(BF16)
                    s += 1

    xv = [x[...].astype(BF16) for x in xs]
    prods = []
    s = 0
    for p, xi in enumerate(pairs):
        if cast_w[p]:
            w = wscr[s][...]
            s += 1
        else:
            w = ws[p][...]
        prods.append(_dot_nt(xv[xi], w) if w_rows_out else _dot(xv[xi], w))
    o_ref[...] = epilogue(prods, [e[...] for e in extras]).astype(o_ref.dtype)


def _fused_matmul(xs, pairs, ws, extras, epilogue, n_out, out_dtype, tm, tn, name, w_rows_out=False):
    m = xs[0].shape[0]
    assert m % tm == 0 and n_out % tn == 0
    cast_w = tuple(w.dtype != BF16 for w, _ in ws)
    assert not (w_rows_out and any(cast_w))
    in_specs = []
    for x in xs:
        in_specs.append(pl.BlockSpec((tm, x.shape[1]), lambda j, i: (i, 0)))
    for p, (w, roff) in enumerate(ws):
        k = xs[pairs[p]].shape[1]
        if w_rows_out:
            in_specs.append(pl.BlockSpec((tn, k), lambda j, i: (j, 0)))
            continue
        in_specs.append(pl.BlockSpec((k, tn), functools.partial(lambda j, i, r: (r, j), r=roff)))
    for e, coff in extras:
        in_specs.append(pl.BlockSpec((tm, tn), functools.partial(lambda j, i, c: (i, c + j), c=coff)))
    scratch = [pltpu.VMEM((xs[pairs[p]].shape[1], tn), BF16) for p in range(len(ws)) if cast_w[p]]
    kern = functools.partial(_mm_kernel, n_x=len(xs), pairs=tuple(pairs), n_extra=len(extras),
                             epilogue=epilogue, cast_w=cast_w, w_rows_out=w_rows_out)
    return pl.pallas_call(
        kern,
        grid=(n_out // tn, m // tm),
        in_specs=in_specs,
        out_specs=pl.BlockSpec((tm, tn), lambda j, i: (i, j)),
        out_shape=jax.ShapeDtypeStruct((m, n_out), out_dtype),
        scratch_shapes=scratch,
        compiler_params=_cparams(("arbitrary", "arbitrary")),
        name=name,
    )(*xs, *[w for w, _ in ws], *[e for e, _ in extras])


def _w_in_relayout_kernel(a_ref, b0_ref, b1_ref, b2_ref, g0_ref, g1_ref, g2_ref, o_ref, *, n_plain, n_main):
    i = pl.program_id(0)
    tr = o_ref.shape[0]
    gk = N_GATES // NSA_KV_HEADS

    @pl.when(i < n_plain)
    def _():
        o_ref[...] = a_ref[0].astype(BF16)

    @pl.when((i >= n_plain) & (i < n_main))
    def _():
        o_ref[...] = jnp.concatenate([a_ref[0, N_GATES:, :], b0_ref[0], b1_ref[0], b2_ref[0]], axis=0).astype(BF16)

    @pl.when(i == n_main)
    def _():
        d = o_ref.shape[1]
        g = jnp.concatenate([g0_ref[0], g1_ref[0], g2_ref[0], jnp.zeros((LANES - N_GATES, d), F32)], axis=0)
        row = lax.broadcasted_iota(jnp.int32, (LANES, d), 0)
        blocks = [jnp.where(row < gk, g if k == 0 else pltpu.roll(g, LANES - k * gk, 0), 0.0)
                  for k in range(NSA_KV_HEADS)]
        blocks.append(jnp.zeros((tr - NSA_KV_HEADS * LANES, d), F32))
        o_ref[...] = jnp.concatenate(blocks, axis=0).astype(BF16)


def _w_in_relayout(w_in_t, layer, o_ng, nz):
    _, n_in, d = w_in_t.shape
    tr = GATE_PAD
    ng0 = n_in - N_GATES
    assert N_GATES == 3 * SUBLANES and o_ng % tr == 0 and ng0 % tr == 0 and nz == ng0 + tr
    n_plain, n_main = o_ng // tr, ng0 // tr
    sub = tr // SUBLANES

    def small(base8):
        return pl.BlockSpec((1, SUBLANES, d), lambda i: (layer, base8(i), 0))

    return pl.pallas_call(
        functools.partial(_w_in_relayout_kernel, n_plain=n_plain, n_main=n_main),
        grid=(n_main + 1,),
        in_specs=[pl.BlockSpec((1, tr, d), lambda i: (layer, jnp.minimum(i, n_main - 1), 0))]
        + [small(functools.partial(lambda i, j: jnp.minimum(i + 1, n_main) * sub + j, j=j)) for j in range(3)]
        + [small(functools.partial(lambda i, j: o_ng // SUBLANES + j, j=j)) for j in range(3)],
        out_specs=pl.BlockSpec((tr, d), lambda i: (i, 0)),
        out_shape=jax.ShapeDtypeStruct((nz, d), BF16),
        compiler_params=_cparams(("arbitrary",)),
        name="w_in_relayout",
    )(w_in_t, w_in_t, w_in_t, w_in_t, w_in_t, w_in_t, w_in_t)


def _rows_out_kernel(z_ref, _buf, o_ref, *, n_kinds):
    tm = z_ref.shape[0]
    for r in range(n_kinds):
        o_ref[pl.ds(r, tm, stride=n_kinds), :] = z_ref[:, r * HEAD_DIM:(r + 1) * HEAD_DIM]


def _rows_out(z, col0, n_kinds, n_seq, rows_per_seq, first_row, buf, layer, name):
    width = n_kinds * HEAD_DIM
    tm = _pick_tile(rows_per_seq, 512, SUBLANES)
    nt = rows_per_seq // tm
    assert col0 % width == 0 and all(first_row(s) % tm == 0 for s in range(n_seq))
    stride = (first_row(1) - first_row(0)) // tm if n_seq > 1 else 0
    rb0 = first_row(0) // tm
    return pl.pallas_call(
        functools.partial(_rows_out_kernel, n_kinds=n_kinds),
        grid=(n_seq, nt),
        in_specs=[pl.BlockSpec((tm, width), lambda s, i: (rb0 + s * stride + i, col0 // width)),
                  pl.BlockSpec(memory_space=pl.ANY)],
        out_specs=pl.BlockSpec((tm * n_kinds, HEAD_DIM), lambda s, i: ((layer * n_seq + s) * nt + i, 0)),
        out_shape=jax.ShapeDtypeStruct(buf.shape, buf.dtype),
        input_output_aliases={1: 0},
        compiler_params=_cparams(("arbitrary", "arbitrary")),
        name=name,
    )(z, buf)


def _compress_kernel(blk_ref, pe_ref, w1_ref, w2_ref, o_ref, x2d, w1b):
    @pl.when(pl.program_id(1) == 0)
    def _():
        w1b[...] = w1_ref[0].astype(BF16)

    cb = x2d.shape[0]
    for tau in range(CMP_BLOCK):
        x2d[:, tau * HEAD_DIM:(tau + 1) * HEAD_DIM] = (
            blk_ref[pl.ds(tau, cb, stride=CMP_BLOCK), :] + pe_ref[0, tau:tau + 1, :]).astype(BF16)
    h1 = _dot(x2d[...], w1b[...])
    h1 = h1 * jax.nn.sigmoid(h1)
    o_ref[...] = _dot(h1.astype(BF16), w2_ref[0].astype(BF16))


def _compress(tokens, n_blocks, col_block0, pe, w1, w2, cb, name):
    n_rk = 2 * NSA_KV_HEADS
    return pl.pallas_call(
        _compress_kernel,
        grid=(n_rk, n_blocks // cb),
        in_specs=[
            pl.BlockSpec((cb * CMP_BLOCK, HEAD_DIM), lambda rk, t: (t, col_block0 + rk)),
            pl.BlockSpec((1, CMP_BLOCK, HEAD_DIM), lambda rk, t: (rk // NSA_KV_HEADS, 0, 0)),
            pl.BlockSpec((1, CMP_BLOCK * HEAD_DIM, HEAD_DIM), lambda rk, t: (rk // NSA_KV_HEADS, 0, 0)),
            pl.BlockSpec((1, HEAD_DIM, HEAD_DIM), lambda rk, t: (rk // NSA_KV_HEADS, 0, 0)),
        ],
        out_specs=pl.BlockSpec((cb, HEAD_DIM), lambda rk, t: (t, rk)),
        out_shape=jax.ShapeDtypeStruct((n_blocks, n_rk * HEAD_DIM), F32),
        scratch_shapes=[pltpu.VMEM((cb, CMP_BLOCK * HEAD_DIM), BF16),
                        pltpu.VMEM((CMP_BLOCK * HEAD_DIM, HEAD_DIM), BF16)],
        compiler_params=_cparams(("arbitrary", "arbitrary")),
        name=name,
    )(tokens, pe, w1, w2)


def _compress_rows_kernel(pe_ref, w1_ref, w2_ref, tok_hbm, o_ref, tbuf, x2d, sem, *, block0, n_steps, n_used):
    t = pl.program_id(0)
    slot = lax.rem(t, 2)
    cb = tbuf.shape[1]

    def fetch(step, dst_slot):
        return pltpu.make_async_copy(tok_hbm.at[pl.ds(block0 + step * cb, cb), :, 0:n_used, :],
                                     tbuf.at[dst_slot, :, :, 0:n_used, :], sem.at[dst_slot])

    @pl.when(t == 0)
    def _():
        tbuf[...] = jnp.zeros(tbuf.shape, F32)
        fetch(0, 0).start()

    @pl.when(t + 1 < n_steps)
    def _():
        fetch(t + 1, 1 - slot).start()

    fetch(t, slot).wait()
    for tau in range(CMP_BLOCK):
        tok = tbuf[slot, :, tau] + pe_ref[tau]
        x2d[:, tau * HEAD_DIM:(tau + 1) * HEAD_DIM] = tok.reshape(cb * SUBLANES, HEAD_DIM).astype(BF16)
    h1 = _dot(x2d[...], w1_ref[...])
    h1 = h1 * jax.nn.sigmoid(h1)
    o_ref[...] = _dot(h1.astype(BF16), w2_ref[...])


def _compress_rows(blocks4d, block0, n_blocks, pe, w1, w2, cb, name):
    n_kinds = blocks4d.shape[2]
    assert n_kinds == SUBLANES and n_blocks % cb == 0
    kinds_k = NSA_KV_HEADS
    zero = jnp.zeros((CMP_BLOCK, n_kinds - 2 * kinds_k, HEAD_DIM), F32)
    pe_rows = jnp.concatenate([jnp.repeat(pe[0][:, None, :], kinds_k, axis=1),
                               jnp.repeat(pe[1][:, None, :], kinds_k, axis=1), zero], axis=1)
    w1_both = jnp.concatenate([w1[0], w1[1]], axis=1).astype(BF16)
    zw = jnp.zeros((HEAD_DIM, HEAD_DIM), F32)
    w2_both = jnp.concatenate([jnp.concatenate([w2[0], zw], axis=1),
                               jnp.concatenate([zw, w2[1]], axis=1)], axis=0).astype(BF16)
    return pl.pallas_call(
        functools.partial(_compress_rows_kernel, block0=block0, n_steps=n_blocks // cb, n_used=2 * kinds_k),
        grid=(n_blocks // cb,),
        in_specs=[
            pl.BlockSpec((CMP_BLOCK, n_kinds, HEAD_DIM), lambda t: (0, 0, 0)),
            pl.BlockSpec((CMP_BLOCK * HEAD_DIM, 2 * HEAD_DIM), lambda t: (0, 0)),
            pl.BlockSpec((2 * HEAD_DIM, 2 * HEAD_DIM), lambda t: (0, 0)),
            pl.BlockSpec(memory_space=pl.ANY),
        ],
        out_specs=pl.BlockSpec((cb * n_kinds, 2 * HEAD_DIM), lambda t: (t, 0)),
        out_shape=jax.ShapeDtypeStruct((n_blocks * n_kinds, 2 * HEAD_DIM), F32),
        scratch_shapes=[pltpu.VMEM((2, cb, CMP_BLOCK, n_kinds, HEAD_DIM), F32),
                        pltpu.VMEM((cb * n_kinds, CMP_BLOCK * HEAD_DIM), BF16),
                        pltpu.SemaphoreType.DMA((2,))],
        compiler_params=_cparams(("arbitrary",)),
        name=name,
    )(pe_rows, w1_both, w2_both, blocks4d)


def _masked_softmax_unnorm(s, mask):
    s = jnp.where(mask, s, MASK_VALUE)
    m = jnp.max(s, axis=-1, keepdims=True)
    e = jnp.where(mask, jnp.exp(s - m), 0.0)
    den = jnp.maximum(jnp.sum(e, axis=-1, keepdims=True), 1e-30)
    return e, den


def _select_blocks(score_t_ref, n_blocks):
    sc = score_t_ref[...]
    idx = lax.broadcasted_iota(jnp.int32, sc.shape, 0)

    def body(i, cnt):
        row = score_t_ref[pl.ds(i, 1), :]
        return cnt + jnp.where(idx > i, (row >= sc).astype(F32), (row > sc).astype(F32))

    cnt = lax.fori_loop(0, n_blocks, body, jnp.zeros(sc.shape, F32), unroll=math.gcd(n_blocks, 8))
    return ((cnt < N_SELECT) & (sc >= 0.0)).astype(F32)


def _select_blocks_rows(score, blk_of_lane, n_pos, blk_of_pos):
    cnt = jnp.zeros(score.shape, F32)
    for p in range(n_pos):
        col = score[:, p:p + 1]
        cnt = cnt + jnp.where(blk_of_lane > blk_of_pos(p), (col >= score).astype(F32), (col > score).astype(F32))
    return ((cnt < N_SELECT) & (score >= 0.0)).astype(F32)


def _nsa_prompt_kernel(tab_ref, q_ref, ks_ref, vs_ref, kw_ref, vw_ref, gt_ref, kc_ref, vc_ref, _out_buf, o_ref,
                       ksb, vsb, kwb, vwb, kcb, vcb, ebuf, bnear, bcn, sct, sbuf, mrun, lrun, acc, *, seq):
    k = pl.program_id(1)
    qb = pl.program_id(2)
    nc = seq // CMP_BLOCK
    ncp = kcb.shape[0]
    n8 = sct.shape[0]
    g_heads = [k * NSA_GROUP + g for g in range(NSA_GROUP)]
    rows = NSA_GROUP * Q_BLOCK

    @pl.when(qb == 0)
    def _():
        ksb[0:Q_BLOCK, :] = jnp.zeros((Q_BLOCK, HEAD_DIM), BF16)
        ksb[Q_BLOCK:, :] = ks_ref[...].astype(BF16)
        vsb[0:Q_BLOCK, :] = jnp.zeros((Q_BLOCK, HEAD_DIM), BF16)
        vsb[Q_BLOCK:, :] = vs_ref[...].astype(BF16)
        kwb[0:WINDOW, :] = jnp.zeros((WINDOW, HEAD_DIM), BF16)
        kwb[WINDOW:, :] = kw_ref[...].astype(BF16)
        vwb[0:WINDOW, :] = jnp.zeros((WINDOW, HEAD_DIM), BF16)
        vwb[WINDOW:, :] = vw_ref[...].astype(BF16)
        kcb[...] = jnp.zeros(kcb.shape, BF16)
        vcb[...] = jnp.zeros(vcb.shape, BF16)
        kcb[0:nc, :] = kc_ref[...].astype(BF16)
        vcb[0:nc, :] = vc_ref[...].astype(BF16)
        blk = lax.broadcasted_iota(jnp.int32, ebuf.shape, 0)
        key = lax.broadcasted_iota(jnp.int32, ebuf.shape, 1) - Q_BLOCK
        ebuf[...] = ((key >= 0) & (lax.shift_right_logical(key, 6) == blk)).astype(BF16)
        ti = lax.broadcasted_iota(jnp.int32, (Q_BLOCK, Q_BLOCK), 0)
        tj = lax.broadcasted_iota(jnp.int32, (Q_BLOCK, Q_BLOCK), 1)
        b0 = _t5_bias(jnp.maximum(ti - tj, 0), tab_ref, g_heads)
        b1 = _t5_bias(Q_BLOCK + ti - tj, tab_ref, g_heads)
        tc = lax.broadcasted_iota(jnp.int32, (Q_BLOCK, ncp), 0)
        cc = lax.broadcasted_iota(jnp.int32, (Q_BLOCK, ncp), 1)
        dist_cn = jnp.where(cc < 4, tc + (CMP_BLOCK + 1) - CMP_BLOCK * cc, REL_MAX_DIST)
        bc = _t5_bias(jnp.maximum(dist_cn, 0), tab_ref, g_heads)
        for g in range(NSA_GROUP):
            far = tab_ref[REL_BUCKETS - 1, g_heads[g]]
            bnear[g * Q_BLOCK:(g + 1) * Q_BLOCK, 0:Q_BLOCK] = b1[g] - far
            bnear[g * Q_BLOCK:(g + 1) * Q_BLOCK, Q_BLOCK:] = b0[g] - far
            bcn[g * Q_BLOCK:(g + 1) * Q_BLOCK, :] = bc[g] - far

    def rep_g(x):
        return jnp.concatenate([x] * NSA_GROUP, axis=0)

    def fold_max(x):
        m = x[:, 0:LANES]
        for c in range(1, x.shape[1] // LANES):
            m = jnp.maximum(m, x[:, c * LANES:(c + 1) * LANES])
        return m

    def fold_sum(x):
        m = x[:, 0:LANES]
        for c in range(1, x.shape[1] // LANES):
            m = m + x[:, c * LANES:(c + 1) * LANES]
        return m

    t_col = lax.broadcasted_iota(jnp.int32, (Q_BLOCK, 1), 0)
    q_pos = qb * Q_BLOCK + t_col
    scale = HEAD_DIM ** -0.5
    q_all = jnp.concatenate(
        [(q_ref[:, g * HEAD_DIM:(g + 1) * HEAD_DIM] * scale).astype(BF16) for g in range(NSA_GROUP)], axis=0)
    gates = jax.nn.sigmoid(gt_ref[...])

    cidx = lax.broadcasted_iota(jnp.int32, (Q_BLOCK, ncp), 1)
    dist_c = q_pos - (cidx * CMP_BLOCK + (CMP_BLOCK - 1))
    mask_c = rep_g((dist_c >= 0) & (cidx < nc))
    bias_c = pltpu.roll(bcn[...], lax.rem(2 * qb - 2 + ncp, ncp), 1)
    e_c, den_c = _masked_softmax_unnorm(_dot_nt(q_all, kcb[...]) + bias_c, mask_c)
    p_c = e_c / den_c
    o_c = _dot(p_c.astype(BF16), vcb[...])
    imp = p_c[0:Q_BLOCK]
    for g in range(1, NSA_GROUP):
        imp = imp + p_c[g * Q_BLOCK:(g + 1) * Q_BLOCK]

    cur = lax.shift_right_logical(q_pos, 6)
    eligible = (cidx * CMP_BLOCK <= q_pos) & (cidx < nc)
    forced = (cidx == 0) | (cidx == cur) | (cidx == cur - 1)
    score = jnp.where(eligible, imp + jnp.where(forced, FORCE_SCORE, 0.0), -1.0)
    score = jnp.where(cidx < nc, score, -2.0)
    score_t = jnp.concatenate([score[:, c * LANES:(c + 1) * LANES].T for c in range(ncp // LANES)], axis=0)
    sct[...] = score_t[0:n8, :]
    sel_t = _select_blocks(sct, nc)
    if n8 < ncp:
        sel_t = jnp.concatenate([sel_t, jnp.zeros((ncp - n8, Q_BLOCK), F32)], axis=0)
    sel = jnp.concatenate([sel_t[c * LANES:(c + 1) * LANES, :].T for c in range(ncp // LANES)], axis=1)
    sel = sel.astype(BF16)

    near = 2 * Q_BLOCK
    n0 = pl.multiple_of(qb * Q_BLOCK, Q_BLOCK)
    jn = lax.broadcasted_iota(jnp.int32, (Q_BLOCK, near), 1)
    ok_near = (_dot(sel, ebuf[:, pl.ds(n0, near)]) > 0.5) & (jn - Q_BLOCK <= t_col)
    mb_near = rep_g(jnp.where(ok_near, 0.0, MASK_VALUE))
    s_near = _dot_nt(q_all, ksb[pl.ds(n0, near), :]) + bnear[...] + mb_near
    far_end = (qb - 1) * Q_BLOCK
    n_chunks = _divmod_const(jnp.maximum(far_end + (FAR_CHUNK - 1), 0), FAR_CHUNK)[0]
    mrun[...] = jnp.full(mrun.shape, MASK_VALUE, F32)

    def pass1(c, carry):
        c0 = pl.multiple_of(c * FAR_CHUNK, FAR_CHUNK)
        p0 = pl.multiple_of(c * FAR_CHUNK + Q_BLOCK, Q_BLOCK)
        pos = c * FAR_CHUNK + lax.broadcasted_iota(jnp.int32, (Q_BLOCK, FAR_CHUNK), 1)
        ok = (_dot(sel, ebuf[:, pl.ds(p0, FAR_CHUNK)]) > 0.5) & (pos < far_end)
        mb = jnp.where(ok, 0.0, MASK_VALUE)
        s = _dot_nt(q_all, ksb[pl.ds(p0, FAR_CHUNK), :])
        for g in range(NSA_GROUP):
            r = slice(g * Q_BLOCK, (g + 1) * Q_BLOCK)
            sg = s[r] + mb
            sbuf[r, pl.ds(c0, FAR_CHUNK)] = sg
            mrun[r, :] = jnp.maximum(mrun[r, :], fold_max(sg))
        return carry

    lax.fori_loop(0, n_chunks, pass1, 0)
    m_s = jnp.maximum(jnp.max(mrun[...], axis=-1, keepdims=True), jnp.max(s_near, axis=-1, keepdims=True))
    e_near = jnp.exp(s_near - m_s)
    mrun[...] = jnp.broadcast_to(m_s, mrun.shape)
    lrun[...] = fold_sum(e_near)
    acc[...] = _dot(e_near.astype(BF16), vsb[pl.ds(n0, near), :])

    def pass2(c, carry):
        c0 = pl.multiple_of(c * FAR_CHUNK, FAR_CHUNK)
        p0 = pl.multiple_of(c * FAR_CHUNK + Q_BLOCK, Q_BLOCK)
        m_full = mrun[...]
        e = jnp.exp(sbuf[:, pl.ds(c0, FAR_CHUNK)] - jnp.concatenate([m_full] * (FAR_CHUNK // LANES), axis=1))
        lrun[...] = lrun[...] + fold_sum(e)
        acc[...] = acc[...] + _dot(e.astype(BF16), vsb[pl.ds(p0, FAR_CHUNK), :])
        return carry

    lax.fori_loop(0, n_chunks, pass2, 0)
    o_s = acc[...] / jnp.sum(lrun[...], axis=-1, keepdims=True)

    wlen = WINDOW + Q_BLOCK
    jw = lax.broadcasted_iota(jnp.int32, (Q_BLOCK, wlen), 1)
    dist_w = WINDOW + t_col - jw
    ok_w = (dist_w >= 0) & (dist_w <= WINDOW) & (qb * Q_BLOCK - WINDOW + jw >= 0)
    mb_w = rep_g(jnp.where(ok_w, 0.0, MASK_VALUE))
    bias_w = jnp.concatenate([jnp.zeros((rows, wlen - near), F32), bnear[...]], axis=1)
    s_w = _dot_nt(q_all, kwb[pl.ds(n0, wlen), :]) + bias_w + mb_w
    e_w = jnp.exp(s_w - jnp.max(s_w, axis=-1, keepdims=True))
    o_w = _dot(e_w.astype(BF16), vwb[pl.ds(n0, wlen), :]) / jnp.sum(e_w, axis=-1, keepdims=True)

    for g in range(NSA_GROUP):
        r = slice(g * Q_BLOCK, (g + 1) * Q_BLOCK)
        c0 = g * 3
        y = gates[:, c0:c0 + 1] * o_c[r] + gates[:, c0 + 1:c0 + 2] * o_s[r] + gates[:, c0 + 2:c0 + 3] * o_w[r]
        o_ref[:, g * HEAD_DIM:(g + 1) * HEAD_DIM] = y.astype(o_ref.dtype)


def _nsa_prompt(z, comp, rel_bias, n_batch, seq, cols, out_buf):
    nqb = seq // Q_BLOCK
    nc = seq // CMP_BLOCK
    ncp = _round_up(nc, LANES)
    n8 = _round_up(nc, SUBLANES)
    kvc = cols["kv"] // HEAD_DIM
    winc = cols["win"] // HEAD_DIM
    kh = NSA_KV_HEADS
    gw = NSA_GROUP * HEAD_DIM
    rows = NSA_GROUP * Q_BLOCK
    assert seq % FAR_CHUNK == 0

    def zspec(col_block_of_k):
        return pl.BlockSpec((seq, HEAD_DIM), lambda b, k, q: (b, col_block_of_k(k)))

    return pl.pallas_call(
        functools.partial(_nsa_prompt_kernel, seq=seq),
        grid=(n_batch, kh, nqb),
        in_specs=[
            pl.BlockSpec(memory_space=pltpu.SMEM),
            pl.BlockSpec((Q_BLOCK, gw), lambda b, k, q: (b * nqb + q, cols["q"] // gw + k)),
            zspec(lambda k: kvc + 2 * kh + k),
            zspec(lambda k: kvc + 3 * kh + k),
            zspec(lambda k: winc + k),
            zspec(lambda k: winc + kh + k),
            pl.BlockSpec((Q_BLOCK, LANES), lambda b, k, q: (b * nqb + q, cols["ng"] // LANES + k)),
            pl.BlockSpec((nc, HEAD_DIM), lambda b, k, q: (b, k)),
            pl.BlockSpec((nc, HEAD_DIM), lambda b, k, q: (b, kh + k)),
            pl.BlockSpec(memory_space=pl.ANY),
        ],
        out_specs=pl.BlockSpec((Q_BLOCK, gw), lambda b, k, q: (b * nqb + q, k)),
        out_shape=jax.ShapeDtypeStruct(out_buf.shape, out_buf.dtype),
        input_output_aliases={9: 0},
        scratch_shapes=[
            pltpu.VMEM((seq + Q_BLOCK, HEAD_DIM), BF16), pltpu.VMEM((seq + Q_BLOCK, HEAD_DIM), BF16),
            pltpu.VMEM((seq + WINDOW, HEAD_DIM), BF16), pltpu.VMEM((seq + WINDOW, HEAD_DIM), BF16),
            pltpu.VMEM((ncp, HEAD_DIM), BF16), pltpu.VMEM((ncp, HEAD_DIM), BF16),
            pltpu.VMEM((ncp, seq + Q_BLOCK), BF16),
            pltpu.VMEM((rows, 2 * Q_BLOCK), F32),
            pltpu.VMEM((rows, ncp), F32),
            pltpu.VMEM((n8, Q_BLOCK), F32),
            pltpu.VMEM((rows, seq), F32),
            pltpu.VMEM((rows, LANES), F32), pltpu.VMEM((rows, LANES), F32), pltpu.VMEM((rows, LANES), F32),
        ],
        compiler_params=_cparams(("arbitrary", "arbitrary", "arbitrary")),
        name="nsa_prompt",
    )(rel_bias, z, z, z, z, z, z, comp, comp, out_buf)


def _nsa_sample_kernel(pt_ref, tab_ref, q_ref, kvn_ref, wn_ref, gt_ref, win_hbm, pages_hbm, comp_hbm, _out_buf,
                       _win_buf, o_ref, wo_ref,
                       pbuf, cbuf, wbuf, ebuf, psem, csem, wsem, *, n_seq, n_pages, page0, layer, tn):
    b = pl.program_id(0)
    slot = lax.rem(b, 2)
    past = n_pages * 2 * CMP_BLOCK
    ncs = 2 * n_pages
    ns = ncs + 1
    ncp = ebuf.shape[0]
    nkeys = past + LANES
    wb = wbuf.shape[2]
    n_slc = 2 * NSA_KV_HEADS

    def block_of(pos):
        half, page = _divmod_const(pos, n_pages)
        return jnp.where(pos < ncs, 2 * page + half, pos)

    def page_copies(seq_idx, dst_slot, j):
        pid = pt_ref[seq_idx * n_pages + j]
        slc = tuple(
            pltpu.make_async_copy(pages_hbm.at[page0 + pid, :, n_slc + i, :], pbuf.at[dst_slot, i, j],
                                  psem.at[dst_slot])
            for i in range(n_slc))
        return slc + (pltpu.make_async_copy(comp_hbm.at[pid], cbuf.at[dst_slot, j], csem.at[dst_slot]),)

    def window_copies(seq_idx, dst_slot):
        return tuple(
            pltpu.make_async_copy(win_hbm.at[layer, seq_idx, :, i, :], wbuf.at[dst_slot, i], wsem.at[dst_slot])
            for i in range(n_slc))

    def start_seq(seq_idx, dst_slot):
        def body(j, c):
            for cp in page_copies(seq_idx, dst_slot, j):
                cp.start()
            return c
        lax.fori_loop(0, n_pages, body, 0)
        for cp in window_copies(seq_idx, dst_slot):
            cp.start()

    def wait_seq(seq_idx, dst_slot):
        def body(j, c):
            for cp in page_copies(seq_idx, dst_slot, j):
                cp.wait()
            return c
        lax.fori_loop(0, n_pages, body, 0)
        for cp in window_copies(seq_idx, dst_slot):
            cp.wait()

    @pl.when(b == 0)
    def _():
        start_seq(0, 0)
        pos = lax.broadcasted_iota(jnp.int32, ebuf.shape, 0)
        key = lax.broadcasted_iota(jnp.int32, ebuf.shape, 1)
        ebuf[...] = ((lax.shift_right_logical(key, 6) == block_of(pos)) & (pos <= ncs)).astype(BF16)

    @pl.when(b + 1 < n_seq)
    def _():
        start_seq(b + 1, 1 - slot)

    wait_seq(b, slot)

    t_col = lax.broadcasted_iota(jnp.int32, (tn, 1), 0)
    q_pos = past + t_col
    scale = HEAD_DIM ** -0.5
    rows = NSA_GROUP * tn

    def rep_g(x):
        return jnp.concatenate([x] * NSA_GROUP, axis=0)

    def pad_rows(x, n):
        if x.shape[0] == n:
            return x
        return jnp.concatenate([x, jnp.zeros((n - x.shape[0], x.shape[1]), x.dtype)], axis=0)

    pos1 = lax.broadcasted_iota(jnp.int32, (tn, ncp), 1)
    blk1 = block_of(pos1)
    dist_c = q_pos - (blk1 * CMP_BLOCK + (CMP_BLOCK - 1))
    mask_c = rep_g((dist_c >= 0) & (pos1 < ncs))
    cur = lax.shift_right_logical(q_pos, 6)
    eligible = (blk1 * CMP_BLOCK <= q_pos) & (pos1 <= ncs)
    forced = (blk1 == 0) | (blk1 == cur) | (blk1 == cur - 1)

    qk, o_cmp, scores = [], [], []
    for k in range(NSA_KV_HEADS):
        heads = [k * NSA_GROUP + g for g in range(NSA_GROUP)]
        q = jnp.concatenate(
            [q_ref[:, (k * NSA_GROUP + g) * HEAD_DIM:(k * NSA_GROUP + g + 1) * HEAD_DIM] for g in range(NSA_GROUP)],
            axis=0)
        q = (q * scale).astype(BF16)
        qk.append(q)
        kc = jnp.concatenate([cbuf[slot, :, half * SUBLANES + k, 0:HEAD_DIM] for half in range(2)], axis=0)
        vc = jnp.concatenate([cbuf[slot, :, half * SUBLANES + NSA_KV_HEADS + k, HEAD_DIM:] for half in range(2)],
                             axis=0)
        kc = pad_rows(kc, ncp).astype(BF16)
        vc = pad_rows(vc, ncp).astype(BF16)
        bias_c = jnp.concatenate(_t5_bias(jnp.maximum(dist_c, 0), tab_ref, heads), axis=0)
        s = _dot_nt(q, kc) + bias_c
        e, den = _masked_softmax_unnorm(s, mask_c)
        p = e / den
        o_cmp.append(_dot(p.astype(BF16), vc))
        imp = p[0:tn]
        for g in range(1, NSA_GROUP):
            imp = imp + p[g * tn:(g + 1) * tn]
        sc = jnp.where(eligible, imp + jnp.where(forced, FORCE_SCORE, 0.0), -1.0)
        scores.append(jnp.where(pos1 <= ncs, sc, -2.0))

    def block_of_static(p):
        return 2 * (p % n_pages) + p // n_pages if p < ncs else p

    blk_lane = block_of(lax.broadcasted_iota(jnp.int32, (1, ncp), 1))
    sel = _select_blocks_rows(jnp.concatenate(scores, axis=0), blk_lane, ns, block_of_static)

    kpos = lax.broadcasted_iota(jnp.int32, (tn, nkeys), 1)
    dist_s = q_pos - kpos
    causal_s = rep_g((dist_s >= 0) & (kpos < past + tn))
    near = 2 * LANES
    dist_near = jnp.maximum(dist_s[:, nkeys - near:], 0)
    jw = lax.broadcasted_iota(jnp.int32, (tn, wb + LANES), 1)
    dist_w = wb + t_col - jw
    mask_w = rep_g((dist_w >= 0) & (dist_w <= WINDOW) & (past - wb + jw >= 0) & (jw < wb + tn))
    dist_w_near = jnp.maximum(dist_w[:, wb + LANES - near:], 0)
    gates = jax.nn.sigmoid(gt_ref[...])

    for k in range(NSA_KV_HEADS):
        heads = [k * NSA_GROUP + g for g in range(NSA_GROUP)]
        far = jnp.concatenate([jnp.full((tn, 1), tab_ref[REL_BUCKETS - 1, h], F32) for h in heads], axis=0)
        q = qk[k]
        k_new = pad_rows(kvn_ref[:, (2 * NSA_KV_HEADS + k) * HEAD_DIM:(2 * NSA_KV_HEADS + k + 1) * HEAD_DIM], LANES)
        v_new = pad_rows(kvn_ref[:, (3 * NSA_KV_HEADS + k) * HEAD_DIM:(3 * NSA_KV_HEADS + k + 1) * HEAD_DIM], LANES)
        k_all = jnp.concatenate([pbuf[slot, k].reshape(past, HEAD_DIM), k_new], axis=0).astype(BF16)
        v_all = jnp.concatenate([pbuf[slot, NSA_KV_HEADS + k].reshape(past, HEAD_DIM), v_new], axis=0).astype(BF16)
        sel_k = rep_g(sel[k * tn:(k + 1) * tn, :]).astype(BF16)
        mask_s = (_dot(sel_k, ebuf[...]) > 0.5) & causal_s
        b_near = jnp.concatenate(_t5_bias(dist_near, tab_ref, heads), axis=0) - far
        bias_s = jnp.concatenate([jnp.zeros((rows, nkeys - near), F32), b_near], axis=1)
        s = _dot_nt(q, k_all) + far + bias_s
        e, den = _masked_softmax_unnorm(s, mask_s)
        o_s = _dot(e.astype(BF16), v_all) / den
        kw = jnp.concatenate([wbuf[slot, k],
                              pad_rows(wn_ref[:, k * HEAD_DIM:(k + 1) * HEAD_DIM], LANES)], axis=0).astype(BF16)
        vw = jnp.concatenate(
            [wbuf[slot, NSA_KV_HEADS + k],
             pad_rows(wn_ref[:, (NSA_KV_HEADS + k) * HEAD_DIM:(NSA_KV_HEADS + k + 1) * HEAD_DIM], LANES)],
            axis=0).astype(BF16)
        bw_near = jnp.concatenate(_t5_bias(dist_w_near, tab_ref, heads), axis=0) - far
        bias_w = jnp.concatenate([jnp.zeros((rows, wb + LANES - near), F32), bw_near], axis=1)
        sw = _dot_nt(q, kw) + far + bias_w
        ew, denw = _masked_softmax_unnorm(sw, mask_w)
        o_w = _dot(ew.astype(BF16), vw) / denw
        for g in range(NSA_GROUP):
            c0 = k * LANES + g * 3
            r = slice(g * tn, (g + 1) * tn)
            y = (gates[:, c0:c0 + 1] * o_cmp[k][r] + gates[:, c0 + 1:c0 + 2] * o_s[r]
                 + gates[:, c0 + 2:c0 + 3] * o_w[r])
            h = k * NSA_GROUP + g
            o_ref[:, h * HEAD_DIM:(h + 1) * HEAD_DIM] = y.astype(o_ref.dtype)

    for i in range(n_slc):
        wo_ref[0, pl.ds(i, wb - tn, stride=n_slc), :] = wbuf[slot, i, tn:wb, :]
        wo_ref[0, pl.ds((wb - tn) * n_slc + i, tn, stride=n_slc), :] = wn_ref[:, i * HEAD_DIM:(i + 1) * HEAD_DIM]


def _nsa_sample(z, row0, n_seq, tn, pages, page0, comp_pages, cache_win_l, layer, page_table, rel_bias, cols,
                out_buf, win_buf):
    n_pages = page_table.shape[1]
    page_size = pages.shape[1]
    assert page_size == 2 * CMP_BLOCK and tn == SUBLANES and row0 % tn == 0
    past = n_pages * page_size
    ncs = 2 * n_pages
    ncp = _round_up(ncs + 1, LANES)
    wb = cache_win_l.shape[2]
    assert wb == WINDOW and past >= WINDOW
    half_cols = 2 * NSA_KV_HEADS * HEAD_DIM
    n_slc = 2 * NSA_KV_HEADS
    rb = row0 // tn
    grid_spec = pltpu.PrefetchScalarGridSpec(
        num_scalar_prefetch=1,
        grid=(n_seq,),
        in_specs=[
            pl.BlockSpec(memory_space=pltpu.SMEM),
            pl.BlockSpec((tn, NSA_WIDTH), lambda b, pt: (rb + b, cols["q"] // NSA_WIDTH)),
            pl.BlockSpec((tn, 2 * half_cols), lambda b, pt: (rb + b, cols["kv"] // (2 * half_cols))),
            pl.BlockSpec((tn, half_cols), lambda b, pt: (rb + b, cols["win"] // half_cols)),
            pl.BlockSpec((tn, 2 * LANES), lambda b, pt: (rb + b, cols["ng"] // (2 * LANES))),
            pl.BlockSpec(memory_space=pl.ANY),
            pl.BlockSpec(memory_space=pl.ANY),
            pl.BlockSpec(memory_space=pl.ANY),
            pl.BlockSpec(memory_space=pl.ANY),
            pl.BlockSpec(memory_space=pl.ANY),
        ],
        out_specs=[pl.BlockSpec((tn, NSA_WIDTH), lambda b, pt: (rb + b, 0)),
                   pl.BlockSpec((1, wb * n_slc, HEAD_DIM), lambda b, pt: (layer * n_seq + b, 0, 0))],
        scratch_shapes=[
            pltpu.VMEM((2, n_slc, n_pages, page_size, HEAD_DIM), F32),
            pltpu.VMEM((2, n_pages) + comp_pages.shape[1:], F32),
            pltpu.VMEM((2, n_slc, wb, HEAD_DIM), F32),
            pltpu.VMEM((ncp, past + LANES), BF16),
            pltpu.SemaphoreType.DMA((2,)),
            pltpu.SemaphoreType.DMA((2,)),
            pltpu.SemaphoreType.DMA((2,)),
        ],
    )
    return pl.pallas_call(
        functools.partial(_nsa_sample_kernel, n_seq=n_seq, n_pages=n_pages, page0=page0, layer=layer, tn=tn),
        grid_spec=grid_spec,
        out_shape=[jax.ShapeDtypeStruct(out_buf.shape, out_buf.dtype),
                   jax.ShapeDtypeStruct(win_buf.shape, win_buf.dtype)],
        input_output_aliases={9: 0, 10: 1},
        compiler_params=_cparams(("arbitrary",)),
        name="nsa_sample",
    )(page_table.reshape(-1), rel_bias, z, z, z, z, cache_win_l, pages, comp_pages, out_buf, win_buf)


def _cumsum8(x):
    row = lax.broadcasted_iota(jnp.int32, x.shape, 0)
    for sh in (1, 2, 4):
        x = x + jnp.where(row >= sh, pltpu.roll(x, sh, 0), 0.0)
    return x


def _mixer_kernel(zcb_ref, zcc_ref, zch_ref, zhg_ref, cw_ref, cs_ref, hs_ref, lb_ref, hn_ref, _yb_buf, _yc_buf,
                  yb_ref, yc_ref, cso_ref, hso_ref, uprev, st, qf_s, k_s, lf_s, *, layer, tt):
    ti = pl.program_id(1)
    nt = pl.num_programs(1)

    @pl.when(ti == 0)
    def _():
        uprev[...] = jnp.zeros(uprev.shape, F32)
        uprev[SUBLANES - (CONV_K - 1):SUBLANES, :] = cs_ref[0]
        for h in range(HG_HEADS):
            st[h] = hs_ref[0, h].T

    cb = zcb_ref[...]
    u = zcc_ref[...] * zch_ref[...]
    ext = jnp.concatenate([uprev[...], u], axis=0)
    y = cw_ref[CONV_K - 1:CONV_K, :] * u
    for j in range(CONV_K - 1):
        shift = CONV_K - 1 - j
        y = y + cw_ref[j:j + 1, :] * pltpu.roll(ext, shift, 0)[SUBLANES:, :]
    yb_ref[...] = (cb * y).astype(yb_ref.dtype)
    tail = ext[tt:tt + SUBLANES, :]
    uprev[...] = tail

    @pl.when(ti == nt - 1)
    def _():
        cso_ref[0] = pltpu.roll(tail, CONV_K - 1, 0)[0:CONV_K - 1, :]

    hw = HG_HEADS * HG_DK
    lg = lb_ref[...]
    mx = jnp.max(lg, axis=0, keepdims=True)
    ex = jnp.exp(lg - mx)
    sm = ex / jnp.sum(ex, axis=0, keepdims=True)
    cs = sm[0:1]
    for i in range(1, layer + 1):
        cs = cs + sm[i:i + 1]
    lb = cs - sm[0:1]
    fx = zhg_ref[:, hw:2 * hw]
    lb_pos = lb > 0.0
    log_lb = jnp.log(jnp.where(lb_pos, lb, 1.0))
    log_sig = jnp.minimum(fx, 0.0) - jnp.log1p(jnp.exp(-jnp.abs(fx)))
    a2 = jnp.log1p(-lb) + log_sig
    lae = jnp.maximum(log_lb, a2) + jnp.log1p(jnp.exp(-jnp.abs(log_lb - a2)))
    lf_s[...] = jnp.where(lb_pos, lae, log_sig)
    k_s[...] = (1.0 - lb) * jax.nn.sigmoid(-fx)
    hq = zhg_ref[:, 0:hw]
    qf_s[...] = hq * jax.nn.sigmoid(hq)
    norm_g = hn_ref[...]
    ts = lax.broadcasted_iota(jnp.int32, (HG_BLOCK, HG_BLOCK, 1), 0)
    ss = lax.broadcasted_iota(jnp.int32, (HG_BLOCK, HG_BLOCK, 1), 1)
    causal = ss <= ts

    def block(i, c):
        r0 = pl.multiple_of(i * HG_BLOCK, HG_BLOCK)
        rs = pl.ds(r0, HG_BLOCK)
        for h in range(HG_HEADS):
            kc = slice(h * HG_DK, (h + 1) * HG_DK)
            vcol = slice(2 * hw + h * HG_DV, 2 * hw + (h + 1) * HG_DV)
            gcol = slice(2 * hw + HG_HEADS * HG_DV + h * HG_DV, 2 * hw + HG_HEADS * HG_DV + (h + 1) * HG_DV)
            qf = qf_s[rs, kc]
            kk = k_s[rs, kc]
            g = _cumsum8(lf_s[rs, kc])
            v = zhg_ref[rs, vcol]
            gl = g[HG_BLOCK - 1:HG_BLOCK, :]
            st_h = st[h]
            o = _dot_nt((qf * jnp.exp(g)).astype(BF16), st_h.astype(BF16))
            diff = jnp.where(causal, g[:, None, :] - g[None, :, :], 0.0)
            x3 = jnp.where(causal, qf[:, None, :] * kk[None, :, :] * jnp.exp(diff), 0.0)
            a = jnp.sum(x3, axis=-1, keepdims=True)
            o = o + jnp.sum(a * v[None, :, :], axis=1)
            kd = kk * jnp.exp(gl - g)
            st[h] = st_h * jnp.exp(gl) + _dot_tn(v.astype(BF16), kd.astype(BF16))
            on = o * lax.rsqrt(jnp.mean(o * o, axis=-1, keepdims=True) + NORM_EPS) * norm_g
            gx = zhg_ref[rs, gcol]
            yc_ref[rs, h * HG_DV:(h + 1) * HG_DV] = (on * (gx * jax.nn.sigmoid(gx))).astype(yc_ref.dtype)
        return c

    n_blk = tt // HG_BLOCK
    lax.fori_loop(0, n_blk, block, 0, unroll=min(4, n_blk))

    @pl.when(ti == nt - 1)
    def _():
        for h in range(HG_HEADS):
            hso_ref[0, h] = st[h].T


def _mixer(z, row0, n_seq, seq, conv_w_l, conv_state, hg_state, lb_logits, hg_norm_l, layer, cols, yb_buf, yc_buf):
    cw = conv_w_l.shape[-1]
    hgw = 2 * HG_HEADS * HG_DK + 2 * HG_HEADS * HG_DV
    tt = _pick_tile(seq, 256, SUBLANES)
    nt = seq // tt
    rb = row0 // tt
    assert row0 % tt == 0 and cols["cv"] % cw == 0 and cols["hg"] % hgw == 0
    depth = lb_logits.shape[0]
    cvb = cols["cv"] // cw
    return pl.pallas_call(
        functools.partial(_mixer_kernel, layer=layer, tt=tt),
        grid=(n_seq, nt),
        in_specs=[
            pl.BlockSpec((tt, cw), lambda b, t: (rb + b * nt + t, cvb)),
            pl.BlockSpec((tt, cw), lambda b, t: (rb + b * nt + t, cvb + 1)),
            pl.BlockSpec((tt, cw), lambda b, t: (rb + b * nt + t, cvb + 2)),
            pl.BlockSpec((tt, hgw), lambda b, t: (rb + b * nt + t, cols["hg"] // hgw)),
            pl.BlockSpec((CONV_K, cw), lambda b, t: (0, 0)),
            pl.BlockSpec((1, CONV_K - 1, cw), lambda b, t: (b, 0, 0)),
            pl.BlockSpec((1, HG_HEADS, HG_DK, HG_DV), lambda b, t: (b, 0, 0, 0)),
            pl.BlockSpec((depth, HG_HEADS * HG_DK), lambda b, t: (0, 0)),
            pl.BlockSpec((1, HG_DV), lambda b, t: (0, 0)),
            pl.BlockSpec(memory_space=pl.ANY),
            pl.BlockSpec(memory_space=pl.ANY),
        ],
        out_specs=[
            pl.BlockSpec((tt, cw), lambda b, t: (rb + b * nt + t, 0)),
            pl.BlockSpec((tt, HG_HEADS * HG_DV), lambda b, t: (rb + b * nt + t, 0)),
            pl.BlockSpec((1, CONV_K - 1, cw), lambda b, t: (b, 0, 0)),
            pl.BlockSpec((1, HG_HEADS, HG_DK, HG_DV), lambda b, t: (b, 0, 0, 0)),
        ],
        out_shape=[
            jax.ShapeDtypeStruct(yb_buf.shape, yb_buf.dtype),
            jax.ShapeDtypeStruct(yc_buf.shape, yc_buf.dtype),
            jax.ShapeDtypeStruct((n_seq, CONV_K - 1, cw), F32),
            jax.ShapeDtypeStruct((n_seq, HG_HEADS, HG_DK, HG_DV), F32),
        ],
        input_output_aliases={9: 0, 10: 1},
        scratch_shapes=[
            pltpu.VMEM((SUBLANES, cw), F32),
            pltpu.VMEM((HG_HEADS, HG_DV, HG_DK), F32),
            pltpu.VMEM((tt, HG_HEADS * HG_DK), F32),
            pltpu.VMEM((tt, HG_HEADS * HG_DK), F32),
            pltpu.VMEM((tt, HG_HEADS * HG_DK), F32),
        ],
        compiler_params=_cparams(("arbitrary", "arbitrary")),
        name="mixer_conv_hgrn",
    )(z, z, z, z, conv_w_l, conv_state, hg_state, lb_logits, hg_norm_l.reshape(1, HG_DV), yb_buf, yc_buf)


def _ep_identity(prods, extras):
    return prods[0]


def _ep_mix(prods, extras):
    ga, gb, gc = (jax.nn.sigmoid(e) for e in extras)
    return ga * prods[0] + gb * prods[1] + gc * prods[2]


def _ep_residual(prods, extras):
    return extras[0] + prods[0]


def _ep_swiglu(prods, extras):
    return prods[0] * jax.nn.sigmoid(prods[0]) * prods[1]


def _ep_ple(prods, extras):
    return extras[0] + jax.nn.sigmoid(prods[0]) * prods[1]


def kernel(x_prompt, x_sample, p_prompt, p_sample, cache_kv, cache_win, state_conv, state_hgrn, page_table,
           rel_bias, g_mix, w_in, phi_pe, phi_w1, phi_w2, conv_w, hg_lb_logits, hg_norm, w_pa, w_pb, w_pc, w_o,
           g_ffn, w_gate, w_up, w_down, w_pg, w_pp, g_final):
    bp, seq, d = x_prompt.shape
    bs, tn, _ = x_sample.shape
    depth = w_in.shape[0]
    n_in = w_in.shape[-1]
    cw = conv_w.shape[-1]
    d_ff = w_gate.shape[-1]
    ple = w_pp.shape[1]
    n_pool, page_size = cache_kv.shape[1], cache_kv.shape[2]
    mp, ms = bp * seq, bs * tn
    m = mp + ms
    kvw = 6 * NSA_KV_HEADS * HEAD_DIM
    hgw = 2 * HG_HEADS * HG_DK + 2 * HG_HEADS * HG_DV
    assert n_in == NSA_WIDTH + kvw + N_GATES + 3 * cw + hgw + 3 * d
    assert seq % Q_BLOCK == 0 and seq >= WINDOW

    o_ng = NSA_WIDTH + kvw
    cols = {"q": 0, "kv": NSA_WIDTH, "win": NSA_WIDTH + N_KV_ROWS * NSA_KV_HEADS * HEAD_DIM,
            "cv": o_ng, "hg": o_ng + 3 * cw, "mg": o_ng + 3 * cw + hgw, "ng": n_in - N_GATES}
    nz = cols["ng"] + GATE_PAD

    tm = _pick_tile(m, 1024, 256) if m % 256 == 0 else _pick_tile(m, 1024, SUBLANES)
    h = jnp.concatenate([x_prompt.reshape(mp, d), x_sample.reshape(ms, d)], axis=0)
    n_rows = N_KV_ROWS * NSA_KV_HEADS
    pages = cache_kv.reshape(depth * n_pool, page_size, n_rows, HEAD_DIM)
    pool_blocks = cache_kv.reshape(depth * n_pool * (page_size // CMP_BLOCK), CMP_BLOCK, n_rows, HEAD_DIM)
    nb_pool = n_pool * (page_size // CMP_BLOCK)
    cache_win_r = cache_win.reshape(depth, bs, cache_win.shape[2], 2 * NSA_KV_HEADS, HEAD_DIM)
    zeros_conv = jnp.zeros((bp, CONV_K - 1, cw), F32)
    zeros_hg = jnp.zeros((bp, HG_HEADS, HG_DK, HG_DV), F32)

    def w2d(w):
        return w.reshape(w.shape[0] * w.shape[1], w.shape[2])

    w_in_t = jnp.swapaxes(w_in, 1, 2)
    keep = min(WINDOW, seq)
    kv_p = jnp.zeros((depth * mp * n_rows, HEAD_DIM), F32)
    kv_s = jnp.zeros((depth * ms * n_rows, HEAD_DIM), F32)
    win_p = jnp.zeros((depth * bp * keep * (n_rows // 2), HEAD_DIM), F32)
    wb = cache_win.shape[2]
    win_s = jnp.zeros((depth * bs, wb * (n_rows // 2), HEAD_DIM), F32)
    conv_p, conv_s, hg_p, hg_s = [], [], [], []
    for l in range(depth):
        xn = _rmsnorm(h, g_mix[l], BF16)
        z = _fused_matmul([xn], [0], [(_w_in_relayout(w_in_t, l, o_ng, nz), 0)], [], _ep_identity, nz, F32,
                          tm, _pick_tile(nz, 2560, 2 * LANES), "in_proj", w_rows_out=True)

        comp_prompt = _compress(z, mp // CMP_BLOCK, cols["kv"] // HEAD_DIM, phi_pe[l], phi_w1[l], phi_w2[l],
                                _pick_tile(mp // CMP_BLOCK, 256, SUBLANES), "compress_prompt")
        cb_pool = _pick_tile(nb_pool, 64, SUBLANES)
        comp_pool = _compress_rows(pool_blocks, l * nb_pool, nb_pool, phi_pe[l], phi_w1[l], phi_w2[l],
                                   cb_pool, "compress_pool")
        comp_pages = comp_pool.reshape(n_pool, (page_size // CMP_BLOCK) * n_rows, 2 * HEAD_DIM)
        ya = _nsa_prompt(z, comp_prompt, rel_bias, bp, seq, cols, jnp.zeros((m, NSA_WIDTH), F32))
        ya, win_s = _nsa_sample(z, mp, bs, tn, pages, l * n_pool, comp_pages, cache_win_r, l, page_table,
                                rel_bias, cols, ya, win_s)
        yb, yc, cs_p, hs_p = _mixer(z, 0, bp, seq, conv_w[l], zeros_conv, zeros_hg, hg_lb_logits,
                                    hg_norm[l], l, cols, jnp.zeros((m, cw), F32),
                                    jnp.zeros((m, HG_HEADS * HG_DV), F32))
        yb, yc, cs_s, hs_s = _mixer(z, mp, bs, tn, conv_w[l], state_conv[l], state_hgrn[l], hg_lb_logits,
                                    hg_norm[l], l, cols, yb, yc)

        tn_d = _pick_tile(d, 1024, LANES)
        tn_mix = tn_d
        mgb = cols["mg"] // tn_mix
        mixed = _fused_matmul([ya, yb, yc], [0, 1, 2],
                              [(w2d(w_pa).astype(BF16), l), (w2d(w_pb).astype(BF16), l), (w2d(w_pc).astype(BF16), l)],
                              [(z, mgb), (z, mgb + d // tn_mix), (z, mgb + 2 * (d // tn_mix))],
                              _ep_mix, d, BF16, tm, tn_mix, "mix_proj")
        h = _fused_matmul([mixed], [0], [(w2d(w_o), l)], [(h, 0)], _ep_residual, d, F32, tm, tn_d, "out_proj")
        xf = _rmsnorm(h, g_ffn[l], BF16)
        act = _fused_matmul([xf], [0, 0], [(w2d(w_gate), l), (w2d(w_up), l)], [], _ep_swiglu, d_ff, BF16,
                            tm, _pick_tile(d_ff, 512, LANES), "ffn_up")
        h = _fused_matmul([act], [0], [(w_down[l].astype(BF16), 0)], [(h, 0)], _ep_residual, d, F32,
                          tm, _pick_tile(d, 512, LANES), "ffn_down")
        p_all = jnp.concatenate([p_prompt[l].reshape(mp, ple), p_sample[l].reshape(ms, ple)], axis=0)
        h = _fused_matmul([h, p_all], [0, 1], [(w2d(w_pg), l), (w2d(w_pp), l)], [(h, 0)], _ep_ple, d, F32,
                          tm, tn_d, "ple_gate")

        kv_p = _rows_out(z, cols["kv"], n_rows, 1, mp, lambda s: 0, kv_p, l, "kv_rows_prompt")
        kv_s = _rows_out(z, cols["kv"], n_rows, 1, ms, lambda s: mp, kv_s, l, "kv_rows_sample")
        win_p = _rows_out(z, cols["win"], n_rows // 2, bp, keep, lambda s: s * seq + seq - keep, win_p, l,
                          "win_rows_prompt")
        conv_p.append(cs_p); conv_s.append(cs_s); hg_p.append(hs_p); hg_s.append(hs_s)

    y = _rmsnorm(h, g_final, F32)
    return (y[:mp].reshape(bp, seq, d), y[mp:].reshape(bs, tn, d),
            kv_p.reshape(depth, bp, seq, N_KV_ROWS, NSA_KV_HEADS, HEAD_DIM),
            kv_s.reshape(depth, bs, tn, N_KV_ROWS, NSA_KV_HEADS, HEAD_DIM),
            win_p.reshape(depth, bp, keep, 2, NSA_KV_HEADS, HEAD_DIM),
            win_s.reshape(depth, bs, wb, 2, NSA_KV_HEADS, HEAD_DIM), jnp.stack(conv_p), jnp.stack(conv_s),
            jnp.stack(hg_p), jnp.stack(hg_s))
```

```python
import functools
import math

import numpy as np
import jax
import jax.numpy as jnp
from jax import lax
from jax.experimental import pallas as pl
from jax.experimental.pallas import tpu as pltpu

F32 = jnp.float32
BF16 = jnp.bfloat16

LANES = 128
SUBLANES = 8
VMEM_LIMIT_BYTES = 56 * 1024 * 1024

NSA_HEADS = 8
NSA_KV_HEADS = 2
NSA_GROUP = NSA_HEADS // NSA_KV_HEADS
HEAD_DIM = 128
NSA_WIDTH = NSA_HEADS * HEAD_DIM
CMP_BLOCK = 64
N_SELECT = 16
WINDOW = 512
Q_BLOCK = 128
FAR_CHUNK = 1024
FORCE_SCORE = 1.0e4
MASK_VALUE = -1.0e30
N_KV_ROWS = 4
CONV_K = 3
HG_HEADS = 4
HG_DK = 128
HG_DV = 128
HG_BLOCK = SUBLANES
REL_BUCKETS = 32
REL_MAX_DIST = 128
NORM_EPS = 1e-6
N_GATES = 3 * NSA_HEADS
GATE_PAD = 4 * LANES


def _round_up(x, m):
    return -(-x // m) * m


def _pick_tile(n, pref, unit):
    if n <= pref:
        return n
    best = None
    for t in range(unit, pref + 1, unit):
        if n % t == 0:
            best = t
    assert best is not None, (n, pref, unit)
    return best


def _divmod_const(x, n):
    if n & (n - 1) == 0:
        return lax.shift_right_logical(x, n.bit_length() - 1), x & (n - 1)
    return x // n, lax.rem(x, n)


def _t5_thresholds():
    n = np.arange(0, REL_MAX_DIST + 1)
    max_exact = REL_BUCKETS // 2
    nf = np.maximum(n, 1).astype(np.float32)
    ratio = np.log(nf / np.float32(max_exact)) / np.float32(math.log(REL_MAX_DIST / max_exact))
    large = max_exact + (ratio * np.float32(REL_BUCKETS - max_exact)).astype(np.int32)
    large = np.minimum(large, REL_BUCKETS - 1)
    bucket = np.where(n < max_exact, n, large)
    assert np.all(np.diff(bucket) >= 0) and bucket[-1] == REL_BUCKETS - 1
    return [int(np.argmax(bucket >= j)) for j in range(1, REL_BUCKETS)]


T5_THRESHOLDS = _t5_thresholds()


def _t5_bias(dist, tab_ref, heads):
    ge = [dist >= thr for thr in T5_THRESHOLDS]
    out = []
    for h in heads:
        b = jnp.full(dist.shape, tab_ref[0, h], F32)
        for j, m in enumerate(ge):
            b = jnp.where(m, tab_ref[j + 1, h], b)
        out.append(b)
    return out


def _dot(a, b):
    return jnp.dot(a, b, preferred_element_type=F32)


def _dot_nt(a, b):
    return lax.dot_general(a, b, (((1,), (1,)), ((), ())), preferred_element_type=F32)


def _dot_tn(a, b):
    return lax.dot_general(a, b, (((0,), (0,)), ((), ())), preferred_element_type=F32)


def _cparams(sem):
    return pltpu.CompilerParams(dimension_semantics=sem, vmem_limit_bytes=VMEM_LIMIT_BYTES)


def _rmsnorm_kernel(x_ref, g_ref, o_ref):
    x = x_ref[...]
    y = x * lax.rsqrt(jnp.mean(x * x, axis=-1, keepdims=True) + NORM_EPS)
    o_ref[...] = (y * g_ref[...]).astype(o_ref.dtype)


def _rmsnorm(x, g, out_dtype):
    m, d = x.shape
    tm = _pick_tile(m, 1024, SUBLANES)
    return pl.pallas_call(
        _rmsnorm_kernel,
        grid=(m // tm,),
        in_specs=[pl.BlockSpec((tm, d), lambda i: (i, 0)), pl.BlockSpec((1, d), lambda i: (0, 0))],
        out_specs=pl.BlockSpec((tm, d), lambda i: (i, 0)),
        out_shape=jax.ShapeDtypeStruct((m, d), out_dtype),
        compiler_params=_cparams(("parallel",)),
        name="rmsnorm",
    )(x, g.reshape(1, d))


def _rmsnorm_split_kernel(x_ref, g_ref, a_ref, b_ref, *, n_a):
    x = x_ref[...]
    y = (x * lax.rsqrt(jnp.mean(x * x, axis=-1, keepdims=True) + NORM_EPS)) * g_ref[...]
    i = pl.program_id(0)

    @pl.when(i == 0)
    def _():
        b_ref[...] = jnp.zeros(b_ref.shape, b_ref.dtype)

    @pl.when(i < n_a)
    def _():
        a_ref[...] = y

    @pl.when(i >= n_a)
    def _():
        b_ref[...] = y


def _rmsnorm_split(x, g, rows_a):
    m, d = x.shape
    tm = math.gcd(math.gcd(rows_a, m - rows_a), 256)
    n_a, n_b = rows_a // tm, (m - rows_a) // tm
    return pl.pallas_call(
        functools.partial(_rmsnorm_split_kernel, n_a=n_a),
        grid=(n_a + n_b,),
        in_specs=[pl.BlockSpec((tm, d), lambda i: (i, 0)), pl.BlockSpec((1, d), lambda i: (0, 0))],
        out_specs=[pl.BlockSpec((tm, d), lambda i: (jnp.minimum(i, n_a - 1), 0)),
                   pl.BlockSpec((tm, d), lambda i: (jnp.maximum(i - n_a, 0), 0))],
        out_shape=[jax.ShapeDtypeStruct((rows_a, d), F32), jax.ShapeDtypeStruct((m - rows_a, d), F32)],
        compiler_params=_cparams(("arbitrary",)),
        name="rmsnorm_split",
    )(x, g.reshape(1, d))


def _mm_kernel(*refs, n_x, pairs, n_extra, epilogue, cast_w, w_rows_out):
    xs = refs[:n_x]
    ws = refs[n_x:n_x + len(pairs)]
    extras = refs[n_x + len(pairs):n_x + len(pairs) + n_extra]
    o_ref = refs[n_x + len(pairs) + n_extra]
    wscr = refs[n_x + len(pairs) + n_extra + 1:]
    i = pl.program_id(1)

    if any(cast_w):
        @pl.when(i == 0)
        def _():
            s = 0
            for p in range(len(pairs)):
                if cast_w[p]:
                    wscr[s][...] = ws[p][...].astype(BF16)
                    s += 1

    xv = [x[...].astype(BF16) for x in xs]
    prods = []
    s = 0
    for p, xi in enumerate(pairs):
        if cast_w[p]:
            w = wscr[s][...]
            s += 1
        else:
            w = ws[p][...]
        prods.append(_dot_nt(xv[xi], w) if w_rows_out else _dot(xv[xi], w))
    o_ref[...] = epilogue(prods, [e[...] for e in extras]).astype(o_ref.dtype)


def _fused_matmul(xs, pairs, ws, extras, epilogue, n_out, out_dtype, tm, tn, name, w_rows_out=False,
                  single_buffer_w=False):
    m = xs[0].shape[0]
    assert m % tm == 0 and n_out % tn == 0
    cast_w = tuple(w.dtype != BF16 for w, _ in ws)
    assert not (w_rows_out and any(cast_w))
    in_specs = []
    for x in xs:
        in_specs.append(pl.BlockSpec((tm, x.shape[1]), lambda j, i: (i, 0)))
    for p, (w, roff) in enumerate(ws):
        k = xs[pairs[p]].shape[1]
        if w_rows_out:
            in_specs.append(pl.BlockSpec((tn, k), lambda j, i: (j, 0)))
            continue
        mode = {"pipeline_mode": pl.Buffered(1)} if single_buffer_w else {}
        in_specs.append(pl.BlockSpec((k, tn), functools.partial(lambda j, i, r: (r, j), r=roff), **mode))
    for e, coff in extras:
        in_specs.append(pl.BlockSpec((tm, tn), functools.partial(lambda j, i, c: (i, c + j), c=coff)))
    scratch = [pltpu.VMEM((xs[pairs[p]].shape[1], tn), BF16) for p in range(len(ws)) if cast_w[p]]
    kern = functools.partial(_mm_kernel, n_x=len(xs), pairs=tuple(pairs), n_extra=len(extras),
                             epilogue=epilogue, cast_w=cast_w, w_rows_out=w_rows_out)
    return pl.pallas_call(
        kern,
        grid=(n_out // tn, m // tm),
        in_specs=in_specs,
        out_specs=pl.BlockSpec((tm, tn), lambda j, i: (i, j)),
        out_shape=jax.ShapeDtypeStruct((m, n_out), out_dtype),
        scratch_shapes=scratch,
        compiler_params=_cparams(("arbitrary", "arbitrary")),
        name=name,
    )(*xs, *[w for w, _ in ws], *[e for e, _ in extras])


def _w_in_relayout_kernel(a_ref, b0_ref, b1_ref, b2_ref, g0_ref, g1_ref, g2_ref, o_ref, *, n_plain, n_main):
    i = pl.program_id(0)
    tr = o_ref.shape[0]
    gk = N_GATES // NSA_KV_HEADS

    @pl.when(i < n_plain)
    def _():
        o_ref[...] = a_ref[0].astype(BF16)

    @pl.when((i >= n_plain) & (i < n_main))
    def _():
        o_ref[...] = jnp.concatenate([a_ref[0, N_GATES:, :], b0_ref[0], b1_ref[0], b2_ref[0]], axis=0).astype(BF16)

    @pl.when(i == n_main)
    def _():
        d = o_ref.shape[1]
        g = jnp.concatenate([g0_ref[0], g1_ref[0], g2_ref[0], jnp.zeros((LANES - N_GATES, d), F32)], axis=0)
        row = lax.broadcasted_iota(jnp.int32, (LANES, d), 0)
        blocks = [jnp.where(row < gk, g if k == 0 else pltpu.roll(g, LANES - k * gk, 0), 0.0)
                  for k in range(NSA_KV_HEADS)]
        blocks.append(jnp.zeros((tr - NSA_KV_HEADS * LANES, d), F32))
        o_ref[...] = jnp.concatenate(blocks, axis=0).astype(BF16)


def _w_in_relayout(w_in_t, layer, o_ng, nz):
    _, n_in, d = w_in_t.shape
    tr = GATE_PAD
    ng0 = n_in - N_GATES
    assert N_GATES == 3 * SUBLANES and o_ng % tr == 0 and ng0 % tr == 0 and nz == ng0 + tr
    n_plain, n_main = o_ng // tr, ng0 // tr
    sub = tr // SUBLANES

    def small(base8):
        return pl.BlockSpec((1, SUBLANES, d), lambda i: (layer, base8(i), 0))

    return pl.pallas_call(
        functools.partial(_w_in_relayout_kernel, n_plain=n_plain, n_main=n_main),
        grid=(n_main + 1,),
        in_specs=[pl.BlockSpec((1, tr, d), lambda i: (layer, jnp.minimum(i, n_main - 1), 0))]
        + [small(functools.partial(lambda i, j: jnp.minimum(i + 1, n_main) * sub + j, j=j)) for j in range(3)]
        + [small(functools.partial(lambda i, j: o_ng // SUBLANES + j, j=j)) for j in range(3)],
        out_specs=pl.BlockSpec((tr, d), lambda i: (i, 0)),
        out_shape=jax.ShapeDtypeStruct((nz, d), BF16),
        compiler_params=_cparams(("arbitrary",)),
        name="w_in_relayout",
    )(w_in_t, w_in_t, w_in_t, w_in_t, w_in_t, w_in_t, w_in_t)


def _rows_out_kernel(z_ref, _buf, o_ref, *, n_kinds):
    tm = z_ref.shape[0]
    for r in range(n_kinds):
        o_ref[pl.ds(r, tm, stride=n_kinds), :] = z_ref[:, r * HEAD_DIM:(r + 1) * HEAD_DIM]


def _rows_out(z, col0, n_kinds, n_seq, rows_per_seq, first_row, buf, layer, name):
    width = n_kinds * HEAD_DIM
    tm = _pick_tile(rows_per_seq, 512, SUBLANES)
    nt = rows_per_seq // tm
    assert col0 % width == 0 and all(first_row(s) % tm == 0 for s in range(n_seq))
    stride = (first_row(1) - first_row(0)) // tm if n_seq > 1 else 0
    rb0 = first_row(0) // tm
    return pl.pallas_call(
        functools.partial(_rows_out_kernel, n_kinds=n_kinds),
        grid=(n_seq, nt),
        in_specs=[pl.BlockSpec((tm, width), lambda s, i: (rb0 + s * stride + i, col0 // width)),
                  pl.BlockSpec(memory_space=pl.ANY)],
        out_specs=pl.BlockSpec((tm * n_kinds, HEAD_DIM), lambda s, i: ((layer * n_seq + s) * nt + i, 0)),
        out_shape=jax.ShapeDtypeStruct(buf.shape, buf.dtype),
        input_output_aliases={1: 0},
        compiler_params=_cparams(("arbitrary", "arbitrary")),
        name=name,
    )(z, buf)


def _compress_kernel(blk_ref, pe_ref, w1_ref, w2_ref, o_ref, x2d, w1b):
    @pl.when(pl.program_id(1) == 0)
    def _():
        w1b[...] = w1_ref[0].astype(BF16)

    cb = x2d.shape[0]
    for tau in range(CMP_BLOCK):
        x2d[:, tau * HEAD_DIM:(tau + 1) * HEAD_DIM] = (
            blk_ref[pl.ds(tau, cb, stride=CMP_BLOCK), :] + pe_ref[0, tau:tau + 1, :]).astype(BF16)
    h1 = _dot(x2d[...], w1b[...])
    h1 = h1 * jax.nn.sigmoid(h1)
    o_ref[...] = _dot(h1.astype(BF16), w2_ref[0].astype(BF16))


def _compress(tokens, n_blocks, col_block0, pe, w1, w2, cb, name):
    n_rk = 2 * NSA_KV_HEADS
    return pl.pallas_call(
        _compress_kernel,
        grid=(n_rk, n_blocks // cb),
        in_specs=[
            pl.BlockSpec((cb * CMP_BLOCK, HEAD_DIM), lambda rk, t: (t, col_block0 + rk)),
            pl.BlockSpec((1, CMP_BLOCK, HEAD_DIM), lambda rk, t: (rk // NSA_KV_HEADS, 0, 0)),
            pl.BlockSpec((1, CMP_BLOCK * HEAD_DIM, HEAD_DIM), lambda rk, t: (rk // NSA_KV_HEADS, 0, 0)),
            pl.BlockSpec((1, HEAD_DIM, HEAD_DIM), lambda rk, t: (rk // NSA_KV_HEADS, 0, 0)),
        ],
        out_specs=pl.BlockSpec((cb, HEAD_DIM), lambda rk, t: (t, rk)),
        out_shape=jax.ShapeDtypeStruct((n_blocks, n_rk * HEAD_DIM), F32),
        scratch_shapes=[pltpu.VMEM((cb, CMP_BLOCK * HEAD_DIM), BF16),
                        pltpu.VMEM((CMP_BLOCK * HEAD_DIM, HEAD_DIM), BF16)],
        compiler_params=_cparams(("arbitrary", "arbitrary")),
        name=name,
    )(tokens, pe, w1, w2)


def _compress_rows_kernel(pe_ref, w1_ref, w2_ref, tok_hbm, o_ref, tbuf, x2d, sem, *, block0, n_steps, n_used):
    t = pl.program_id(0)
    slot = lax.rem(t, 2)
    cb = tbuf.shape[1]

    def fetch(step, dst_slot):
        return pltpu.make_async_copy(tok_hbm.at[pl.ds(block0 + step * cb, cb), :, 0:n_used, :],
                                     tbuf.at[dst_slot, :, :, 0:n_used, :], sem.at[dst_slot])

    @pl.when(t == 0)
    def _():
        tbuf[...] = jnp.zeros(tbuf.shape, F32)
        fetch(0, 0).start()

    @pl.when(t + 1 < n_steps)
    def _():
        fetch(t + 1, 1 - slot).start()

    fetch(t, slot).wait()
    for tau in range(CMP_BLOCK):
        tok = tbuf[slot, :, tau] + pe_ref[tau]
        x2d[:, tau * HEAD_DIM:(tau + 1) * HEAD_DIM] = tok.reshape(cb * SUBLANES, HEAD_DIM).astype(BF16)
    h1 = _dot(x2d[...], w1_ref[...])
    h1 = h1 * jax.nn.sigmoid(h1)
    o_ref[...] = _dot(h1.astype(BF16), w2_ref[...])


def _compress_rows(blocks4d, block0, n_blocks, pe, w1, w2, cb, name):
    n_kinds = blocks4d.shape[2]
    assert n_kinds == SUBLANES and n_blocks % cb == 0
    kinds_k = NSA_KV_HEADS
    zero = jnp.zeros((CMP_BLOCK, n_kinds - 2 * kinds_k, HEAD_DIM), F32)
    pe_rows = jnp.concatenate([jnp.repeat(pe[0][:, None, :], kinds_k, axis=1),
                               jnp.repeat(pe[1][:, None, :], kinds_k, axis=1), zero], axis=1)
    w1_both = jnp.concatenate([w1[0], w1[1]], axis=1).astype(BF16)
    zw = jnp.zeros((HEAD_DIM, HEAD_DIM), F32)
    w2_both = jnp.concatenate([jnp.concatenate([w2[0], zw], axis=1),
                               jnp.concatenate([zw, w2[1]], axis=1)], axis=0).astype(BF16)
    return pl.pallas_call(
        functools.partial(_compress_rows_kernel, block0=block0, n_steps=n_blocks // cb, n_used=2 * kinds_k),
        grid=(n_blocks // cb,),
        in_specs=[
            pl.BlockSpec((CMP_BLOCK, n_kinds, HEAD_DIM), lambda t: (0, 0, 0)),
            pl.BlockSpec((CMP_BLOCK * HEAD_DIM, 2 * HEAD_DIM), lambda t: (0, 0)),
            pl.BlockSpec((2 * HEAD_DIM, 2 * HEAD_DIM), lambda t: (0, 0)),
            pl.BlockSpec(memory_space=pl.ANY),
        ],
        out_specs=pl.BlockSpec((cb * n_kinds, 2 * HEAD_DIM), lambda t: (t, 0)),
        out_shape=jax.ShapeDtypeStruct((n_blocks * n_kinds, 2 * HEAD_DIM), F32),
        scratch_shapes=[pltpu.VMEM((2, cb, CMP_BLOCK, n_kinds, HEAD_DIM), F32),
                        pltpu.VMEM((cb * n_kinds, CMP_BLOCK * HEAD_DIM), BF16),
                        pltpu.SemaphoreType.DMA((2,))],
        compiler_params=_cparams(("arbitrary",)),
        name=name,
    )(pe_rows, w1_both, w2_both, blocks4d)


def _masked_softmax_unnorm(s, mask):
    s = jnp.where(mask, s, MASK_VALUE)
    m = jnp.max(s, axis=-1, keepdims=True)
    e = jnp.where(mask, jnp.exp(s - m), 0.0)
    den = jnp.maximum(jnp.sum(e, axis=-1, keepdims=True), 1e-30)
    return e, den


def _select_blocks(score_t_ref, n_blocks):
    sc = score_t_ref[...]
    idx = lax.broadcasted_iota(jnp.int32, sc.shape, 0)

    def body(i, cnt):
        row = score_t_ref[pl.ds(i, 1), :]
        return cnt + jnp.where(idx > i, (row >= sc).astype(F32), (row > sc).astype(F32))

    cnt = lax.fori_loop(0, n_blocks, body, jnp.zeros(sc.shape, F32), unroll=math.gcd(n_blocks, 8))
    return ((cnt < N_SELECT) & (sc >= 0.0)).astype(F32)


def _select_blocks_rows(score, blk_of_lane, n_pos, blk_of_pos):
    cnt = jnp.zeros(score.shape, F32)
    for p in range(n_pos):
        col = score[:, p:p + 1]
        cnt = cnt + jnp.where(blk_of_lane > blk_of_pos(p), (col >= score).astype(F32), (col > score).astype(F32))
    return ((cnt < N_SELECT) & (score >= 0.0)).astype(F32)


def _nsa_prompt_kernel(tab_ref, q_ref, ks_ref, vs_ref, kw_ref, vw_ref, gt_ref, kc_ref, vc_ref, _out_buf, o_ref,
                       ksb, vsb, kwb, vwb, kcb, vcb, ebuf, bnear, bcn, sct, sbuf, mrun, lrun, acc, *, seq):
    k = pl.program_id(1)
    qb = pl.program_id(2)
    nc = seq // CMP_BLOCK
    ncp = kcb.shape[0]
    n8 = sct.shape[0]
    g_heads = [k * NSA_GROUP + g for g in range(NSA_GROUP)]
    rows = NSA_GROUP * Q_BLOCK

    @pl.when(qb == 0)
    def _():
        ksb[0:Q_BLOCK, :] = jnp.zeros((Q_BLOCK, HEAD_DIM), BF16)
        ksb[Q_BLOCK:, :] = ks_ref[...].astype(BF16)
        vsb[0:Q_BLOCK, :] = jnp.zeros((Q_BLOCK, HEAD_DIM), BF16)
        vsb[Q_BLOCK:, :] = vs_ref[...].astype(BF16)
        kwb[0:WINDOW, :] = jnp.zeros((WINDOW, HEAD_DIM), BF16)
        kwb[WINDOW:, :] = kw_ref[...].astype(BF16)
        vwb[0:WINDOW, :] = jnp.zeros((WINDOW, HEAD_DIM), BF16)
        vwb[WINDOW:, :] = vw_ref[...].astype(BF16)
        kcb[...] = jnp.zeros(kcb.shape, BF16)
        vcb[...] = jnp.zeros(vcb.shape, BF16)
        kcb[0:nc, :] = kc_ref[...].astype(BF16)
        vcb[0:nc, :] = vc_ref[...].astype(BF16)
        blk = lax.broadcasted_iota(jnp.int32, ebuf.shape, 0)
        key = lax.broadcasted_iota(jnp.int32, ebuf.shape, 1) - Q_BLOCK
        ebuf[...] = ((key >= 0) & (lax.shift_right_logical(key, 6) == blk)).astype(BF16)
        ti = lax.broadcasted_iota(jnp.int32, (Q_BLOCK, Q_BLOCK), 0)
        tj = lax.broadcasted_iota(jnp.int32, (Q_BLOCK, Q_BLOCK), 1)
        b0 = _t5_bias(jnp.maximum(ti - tj, 0), tab_ref, g_heads)
        b1 = _t5_bias(Q_BLOCK + ti - tj, tab_ref, g_heads)
        tc = lax.broadcasted_iota(jnp.int32, (Q_BLOCK, ncp), 0)
        cc = lax.broadcasted_iota(jnp.int32, (Q_BLOCK, ncp), 1)
        dist_cn = jnp.where(cc < 4, tc + (CMP_BLOCK + 1) - CMP_BLOCK * cc, REL_MAX_DIST)
        bc = _t5_bias(jnp.maximum(dist_cn, 0), tab_ref, g_heads)
        for g in range(NSA_GROUP):
            far = tab_ref[REL_BUCKETS - 1, g_heads[g]]
            bnear[g * Q_BLOCK:(g + 1) * Q_BLOCK, 0:Q_BLOCK] = b1[g] - far
            bnear[g * Q_BLOCK:(g + 1) * Q_BLOCK, Q_BLOCK:] = b0[g] - far
            bcn[g * Q_BLOCK:(g + 1) * Q_BLOCK, :] = bc[g] - far

    def rep_g(x):
        return jnp.concatenate([x] * NSA_GROUP, axis=0)

    def fold_max(x):
        m = x[:, 0:LANES]
        for c in range(1, x.shape[1] // LANES):
            m = jnp.maximum(m, x[:, c * LANES:(c + 1) * LANES])
        return m

    def fold_sum(x):
        m = x[:, 0:LANES]
        for c in range(1, x.shape[1] // LANES):
            m = m + x[:, c * LANES:(c + 1) * LANES]
        return m

    t_col = lax.broadcasted_iota(jnp.int32, (Q_BLOCK, 1), 0)
    q_pos = qb * Q_BLOCK + t_col
    scale = HEAD_DIM ** -0.5
    q_all = jnp.concatenate(
        [(q_ref[:, g * HEAD_DIM:(g + 1) * HEAD_DIM] * scale).astype(BF16) for g in range(NSA_GROUP)], axis=0)
    gates = jax.nn.sigmoid(gt_ref[...])

    cidx = lax.broadcasted_iota(jnp.int32, (Q_BLOCK, ncp), 1)
    dist_c = q_pos - (cidx * CMP_BLOCK + (CMP_BLOCK - 1))
    mask_c = rep_g((dist_c >= 0) & (cidx < nc))
    bias_c = pltpu.roll(bcn[...], lax.rem(2 * qb - 2 + ncp, ncp), 1)
    e_c, den_c = _masked_softmax_unnorm(_dot_nt(q_all, kcb[...]) + bias_c, mask_c)
    p_c = e_c / den_c
    o_c = _dot(p_c.astype(BF16), vcb[...])
    imp = p_c[0:Q_BLOCK]
    for g in range(1, NSA_GROUP):
        imp = imp + p_c[g * Q_BLOCK:(g + 1) * Q_BLOCK]

    cur = lax.shift_right_logical(q_pos, 6)
    eligible = (cidx * CMP_BLOCK <= q_pos) & (cidx < nc)
    forced = (cidx == 0) | (cidx == cur) | (cidx == cur - 1)
    score = jnp.where(eligible, imp + jnp.where(forced, FORCE_SCORE, 0.0), -1.0)
    score = jnp.where(cidx < nc, score, -2.0)
    score_t = jnp.concatenate([score[:, c * LANES:(c + 1) * LANES].T for c in range(ncp // LANES)], axis=0)
    sct[...] = score_t[0:n8, :]
    sel_t = _select_blocks(sct, nc)
    if n8 < ncp:
        sel_t = jnp.concatenate([sel_t, jnp.zeros((ncp - n8, Q_BLOCK), F32)], axis=0)
    sel = jnp.concatenate([sel_t[c * LANES:(c + 1) * LANES, :].T for c in range(ncp // LANES)], axis=1)
    sel = sel.astype(BF16)

    near = 2 * Q_BLOCK
    n0 = pl.multiple_of(qb * Q_BLOCK, Q_BLOCK)
    jn = lax.broadcasted_iota(jnp.int32, (Q_BLOCK, near), 1)
    ok_near = (_dot(sel, ebuf[:, pl.ds(n0, near)]) > 0.5) & (jn - Q_BLOCK <= t_col)
    mb_near = rep_g(jnp.where(ok_near, 0.0, MASK_VALUE))
    s_near = _dot_nt(q_all, ksb[pl.ds(n0, near), :]) + bnear[...] + mb_near
    far_end = (qb - 1) * Q_BLOCK
    n_chunks = _divmod_const(jnp.maximum(far_end + (FAR_CHUNK - 1), 0), FAR_CHUNK)[0]
    mrun[...] = jnp.full(mrun.shape, MASK_VALUE, F32)

    def pass1(c, carry):
        c0 = pl.multiple_of(c * FAR_CHUNK, FAR_CHUNK)
        p0 = pl.multiple_of(c * FAR_CHUNK + Q_BLOCK, Q_BLOCK)
        pos = c * FAR_CHUNK + lax.broadcasted_iota(jnp.int32, (Q_BLOCK, FAR_CHUNK), 1)
        ok = (_dot(sel, ebuf[:, pl.ds(p0, FAR_CHUNK)]) > 0.5) & (pos < far_end)
        mb = jnp.where(ok, 0.0, MASK_VALUE)
        s = _dot_nt(q_all, ksb[pl.ds(p0, FAR_CHUNK), :])
        for g in range(NSA_GROUP):
            r = slice(g * Q_BLOCK, (g + 1) * Q_BLOCK)
            sg = s[r] + mb
            sbuf[r, pl.ds(c0, FAR_CHUNK)] = sg
            mrun[r, :] = jnp.maximum(mrun[r, :], fold_max(sg))
        return carry

    lax.fori_loop(0, n_chunks, pass1, 0)
    m_s = jnp.maximum(jnp.max(mrun[...], axis=-1, keepdims=True), jnp.max(s_near, axis=-1, keepdims=True))
    e_near = jnp.exp(s_near - m_s)
    mrun[...] = jnp.broadcast_to(m_s, mrun.shape)
    lrun[...] = fold_sum(e_near)
    acc[...] = _dot(e_near.astype(BF16), vsb[pl.ds(n0, near), :])

    def pass2(c, carry):
        c0 = pl.multiple_of(c * FAR_CHUNK, FAR_CHUNK)
        p0 = pl.multiple_of(c * FAR_CHUNK + Q_BLOCK, Q_BLOCK)
        m_full = mrun[...]
        e = jnp.exp(sbuf[:, pl.ds(c0, FAR_CHUNK)] - jnp.concatenate([m_full] * (FAR_CHUNK // LANES), axis=1))
        lrun[...] = lrun[...] + fold_sum(e)
        acc[...] = acc[...] + _dot(e.astype(BF16), vsb[pl.ds(p0, FAR_CHUNK), :])
        return carry

    lax.fori_loop(0, n_chunks, pass2, 0)
    o_s = acc[...] / jnp.sum(lrun[...], axis=-1, keepdims=True)

    wlen = WINDOW + Q_BLOCK
    jw = lax.broadcasted_iota(jnp.int32, (Q_BLOCK, wlen), 1)
    dist_w = WINDOW + t_col - jw
    ok_w = (dist_w >= 0) & (dist_w <= WINDOW) & (qb * Q_BLOCK - WINDOW + jw >= 0)
    mb_w = rep_g(jnp.where(ok_w, 0.0, MASK_VALUE))
    bias_w = jnp.concatenate([jnp.zeros((rows, wlen - near), F32), bnear[...]], axis=1)
    s_w = _dot_nt(q_all, kwb[pl.ds(n0, wlen), :]) + bias_w + mb_w
    e_w = jnp.exp(s_w - jnp.max(s_w, axis=-1, keepdims=True))
    o_w = _dot(e_w.astype(BF16), vwb[pl.ds(n0, wlen), :]) / jnp.sum(e_w, axis=-1, keepdims=True)

    for g in range(NSA_GROUP):
        r = slice(g * Q_BLOCK, (g + 1) * Q_BLOCK)
        c0 = g * 3
        y = gates[:, c0:c0 + 1] * o_c[r] + gates[:, c0 + 1:c0 + 2] * o_s[r] + gates[:, c0 + 2:c0 + 3] * o_w[r]
        o_ref[:, g * HEAD_DIM:(g + 1) * HEAD_DIM] = y.astype(o_ref.dtype)


def _nsa_prompt(z, comp, rel_bias, n_batch, seq, cols, out_buf):
    nqb = seq // Q_BLOCK
    nc = seq // CMP_BLOCK
    ncp = _round_up(nc, LANES)
    n8 = _round_up(nc, SUBLANES)
    kvc = cols["kv"] // HEAD_DIM
    winc = cols["win"] // HEAD_DIM
    kh = NSA_KV_HEADS
    gw = NSA_GROUP * HEAD_DIM
    rows = NSA_GROUP * Q_BLOCK
    assert seq % FAR_CHUNK == 0

    def zspec(col_block_of_k):
        return pl.BlockSpec((seq, HEAD_DIM), lambda b, k, q: (b, col_block_of_k(k)))

    return pl.pallas_call(
        functools.partial(_nsa_prompt_kernel, seq=seq),
        grid=(n_batch, kh, nqb),
        in_specs=[
            pl.BlockSpec(memory_space=pltpu.SMEM),
            pl.BlockSpec((Q_BLOCK, gw), lambda b, k, q: (b * nqb + q, cols["q"] // gw + k)),
            zspec(lambda k: kvc + 2 * kh + k),
            zspec(lambda k: kvc + 3 * kh + k),
            zspec(lambda k: winc + k),
            zspec(lambda k: winc + kh + k),
            pl.BlockSpec((Q_BLOCK, LANES), lambda b, k, q: (b * nqb + q, cols["ng"] // LANES + k)),
            pl.BlockSpec((nc, HEAD_DIM), lambda b, k, q: (b, k)),
            pl.BlockSpec((nc, HEAD_DIM), lambda b, k, q: (b, kh + k)),
            pl.BlockSpec(memory_space=pl.ANY),
        ],
        out_specs=pl.BlockSpec((Q_BLOCK, gw), lambda b, k, q: (b * nqb + q, k)),
        out_shape=jax.ShapeDtypeStruct(out_buf.shape, out_buf.dtype),
        input_output_aliases={9: 0},
        scratch_shapes=[
            pltpu.VMEM((seq + Q_BLOCK, HEAD_DIM), BF16), pltpu.VMEM((seq + Q_BLOCK, HEAD_DIM), BF16),
            pltpu.VMEM((seq + WINDOW, HEAD_DIM), BF16), pltpu.VMEM((seq + WINDOW, HEAD_DIM), BF16),
            pltpu.VMEM((ncp, HEAD_DIM), BF16), pltpu.VMEM((ncp, HEAD_DIM), BF16),
            pltpu.VMEM((ncp, seq + Q_BLOCK), BF16),
            pltpu.VMEM((rows, 2 * Q_BLOCK), F32),
            pltpu.VMEM((rows, ncp), F32),
            pltpu.VMEM((n8, Q_BLOCK), F32),
            pltpu.VMEM((rows, seq), F32),
            pltpu.VMEM((rows, LANES), F32), pltpu.VMEM((rows, LANES), F32), pltpu.VMEM((rows, LANES), F32),
        ],
        compiler_params=_cparams(("arbitrary", "arbitrary", "arbitrary")),
        name="nsa_prompt",
    )(rel_bias, z, z, z, z, z, z, comp, comp, out_buf)


def _nsa_sample_kernel(pt_ref, tab_ref, q_ref, kvn_ref, wn_ref, gt_ref, win_hbm, pages_hbm, comp_hbm, _out_buf,
                       _win_buf, o_ref, wo_ref,
                       pbuf, cbuf, wbuf, ebuf, psem, csem, wsem, *, n_seq, n_pages, page0, layer, tn):
    b = pl.program_id(0)
    slot = lax.rem(b, 2)
    past = n_pages * 2 * CMP_BLOCK
    ncs = 2 * n_pages
    ns = ncs + 1
    ncp = ebuf.shape[0]
    nkeys = past + LANES
    wb = wbuf.shape[2]
    n_slc = 2 * NSA_KV_HEADS

    def block_of(pos):
        half, page = _divmod_const(pos, n_pages)
        return jnp.where(pos < ncs, 2 * page + half, pos)

    def page_copies(seq_idx, dst_slot, j):
        pid = pt_ref[seq_idx * n_pages + j]
        slc = tuple(
            pltpu.make_async_copy(pages_hbm.at[page0 + pid, :, n_slc + i, :], pbuf.at[dst_slot, i, j],
                                  psem.at[dst_slot])
            for i in range(n_slc))
        return slc + (pltpu.make_async_copy(comp_hbm.at[pid], cbuf.at[dst_slot, j], csem.at[dst_slot]),)

    def window_copies(seq_idx, dst_slot):
        return tuple(
            pltpu.make_async_copy(win_hbm.at[layer, seq_idx, :, i, :], wbuf.at[dst_slot, i], wsem.at[dst_slot])
            for i in range(n_slc))

    def start_seq(seq_idx, dst_slot):
        def body(j, c):
            for cp in page_copies(seq_idx, dst_slot, j):
                cp.start()
            return c
        lax.fori_loop(0, n_pages, body, 0)
        for cp in window_copies(seq_idx, dst_slot):
            cp.start()

    def wait_seq(seq_idx, dst_slot):
        def body(j, c):
            for cp in page_copies(seq_idx, dst_slot, j):
                cp.wait()
            return c
        lax.fori_loop(0, n_pages, body, 0)
        for cp in window_copies(seq_idx, dst_slot):
            cp.wait()

    @pl.when(b == 0)
    def _():
        start_seq(0, 0)
        pos = lax.broadcasted_iota(jnp.int32, ebuf.shape, 0)
        key = lax.broadcasted_iota(jnp.int32, ebuf.shape, 1)
        ebuf[...] = ((lax.shift_right_logical(key, 6) == block_of(pos)) & (pos <= ncs)).astype(BF16)

    @pl.when(b + 1 < n_seq)
    def _():
        start_seq(b + 1, 1 - slot)

    wait_seq(b, slot)

    t_col = lax.broadcasted_iota(jnp.int32, (tn, 1), 0)
    q_pos = past + t_col
    scale = HEAD_DIM ** -0.5
    rows = NSA_GROUP * tn

    def rep_g(x):
        return jnp.concatenate([x] * NSA_GROUP, axis=0)

    def pad_rows(x, n):
        if x.shape[0] == n:
            return x
        return jnp.concatenate([x, jnp.zeros((n - x.shape[0], x.shape[1]), x.dtype)], axis=0)

    pos1 = lax.broadcasted_iota(jnp.int32, (tn, ncp), 1)
    blk1 = block_of(pos1)
    dist_c = q_pos - (blk1 * CMP_BLOCK + (CMP_BLOCK - 1))
    mask_c = rep_g((dist_c >= 0) & (pos1 < ncs))
    cur = lax.shift_right_logical(q_pos, 6)
    eligible = (blk1 * CMP_BLOCK <= q_pos) & (pos1 <= ncs)
    forced = (blk1 == 0) | (blk1 == cur) | (blk1 == cur - 1)

    qk, o_cmp, scores = [], [], []
    for k in range(NSA_KV_HEADS):
        heads = [k * NSA_GROUP + g for g in range(NSA_GROUP)]
        q = jnp.concatenate(
            [q_ref[:, (k * NSA_GROUP + g) * HEAD_DIM:(k * NSA_GROUP + g + 1) * HEAD_DIM] for g in range(NSA_GROUP)],
            axis=0)
        q = (q * scale).astype(BF16)
        qk.append(q)
        kc = jnp.concatenate([cbuf[slot, :, half * SUBLANES + k, 0:HEAD_DIM] for half in range(2)], axis=0)
        vc = jnp.concatenate([cbuf[slot, :, half * SUBLANES + NSA_KV_HEADS + k, HEAD_DIM:] for half in range(2)],
                             axis=0)
        kc = pad_rows(kc, ncp).astype(BF16)
        vc = pad_rows(vc, ncp).astype(BF16)
        bias_c = jnp.concatenate(_t5_bias(jnp.maximum(dist_c, 0), tab_ref, heads), axis=0)
        s = _dot_nt(q, kc) + bias_c
        e, den = _masked_softmax_unnorm(s, mask_c)
        p = e / den
        o_cmp.append(_dot(p.astype(BF16), vc))
        imp = p[0:tn]
        for g in range(1, NSA_GROUP):
            imp = imp + p[g * tn:(g + 1) * tn]
        sc = jnp.where(eligible, imp + jnp.where(forced, FORCE_SCORE, 0.0), -1.0)
        scores.append(jnp.where(pos1 <= ncs, sc, -2.0))

    def block_of_static(p):
        return 2 * (p % n_pages) + p // n_pages if p < ncs else p

    blk_lane = block_of(lax.broadcasted_iota(jnp.int32, (1, ncp), 1))
    sel = _select_blocks_rows(jnp.concatenate(scores, axis=0), blk_lane, ns, block_of_static)

    kpos = lax.broadcasted_iota(jnp.int32, (tn, nkeys), 1)
    dist_s = q_pos - kpos
    causal_s = rep_g((dist_s >= 0) & (kpos < past + tn))
    near = 2 * LANES
    dist_near = jnp.maximum(dist_s[:, nkeys - near:], 0)
    jw = lax.broadcasted_iota(jnp.int32, (tn, wb + LANES), 1)
    dist_w = wb + t_col - jw
    mask_w = rep_g((dist_w >= 0) & (dist_w <= WINDOW) & (past - wb + jw >= 0) & (jw < wb + tn))
    dist_w_near = jnp.maximum(dist_w[:, wb + LANES - near:], 0)
    gates = jax.nn.sigmoid(gt_ref[...])

    for k in range(NSA_KV_HEADS):
        heads = [k * NSA_GROUP + g for g in range(NSA_GROUP)]
        far = jnp.concatenate([jnp.full((tn, 1), tab_ref[REL_BUCKETS - 1, h], F32) for h in heads], axis=0)
        q = qk[k]
        k_new = pad_rows(kvn_ref[:, (2 * NSA_KV_HEADS + k) * HEAD_DIM:(2 * NSA_KV_HEADS + k + 1) * HEAD_DIM], LANES)
        v_new = pad_rows(kvn_ref[:, (3 * NSA_KV_HEADS + k) * HEAD_DIM:(3 * NSA_KV_HEADS + k + 1) * HEAD_DIM], LANES)
        k_all = jnp.concatenate([pbuf[slot, k].reshape(past, HEAD_DIM), k_new], axis=0).astype(BF16)
        v_all = jnp.concatenate([pbuf[slot, NSA_KV_HEADS + k].reshape(past, HEAD_DIM), v_new], axis=0).astype(BF16)
        sel_k = rep_g(sel[k * tn:(k + 1) * tn, :]).astype(BF16)
        mask_s = (_dot(sel_k, ebuf[...]) > 0.5) & causal_s
        b_near = jnp.concatenate(_t5_bias(dist_near, tab_ref, heads), axis=0) - far
        bias_s = jnp.concatenate([jnp.zeros((rows, nkeys - near), F32), b_near], axis=1)
        s = _dot_nt(q, k_all) + far + bias_s
        e, den = _masked_softmax_unnorm(s, mask_s)
        o_s = _dot(e.astype(BF16), v_all) / den
        kw = jnp.concatenate([wbuf[slot, k],
                              pad_rows(wn_ref[:, k * HEAD_DIM:(k + 1) * HEAD_DIM], LANES)], axis=0).astype(BF16)
        vw = jnp.concatenate(
            [wbuf[slot, NSA_KV_HEADS + k],
             pad_rows(wn_ref[:, (NSA_KV_HEADS + k) * HEAD_DIM:(NSA_KV_HEADS + k + 1) * HEAD_DIM], LANES)],
            axis=0).astype(BF16)
        bw_near = jnp.concatenate(_t5_bias(dist_w_near, tab_ref, heads), axis=0) - far
        bias_w = jnp.concatenate([jnp.zeros((rows, wb + LANES - near), F32), bw_near], axis=1)
        sw = _dot_nt(q, kw) + far + bias_w
        ew, denw = _masked_softmax_unnorm(sw, mask_w)
        o_w = _dot(ew.astype(BF16), vw) / denw
        for g in range(NSA_GROUP):
            c0 = k * LANES + g * 3
            r = slice(g * tn, (g + 1) * tn)
            y = (gates[:, c0:c0 + 1] * o_cmp[k][r] + gates[:, c0 + 1:c0 + 2] * o_s[r]
                 + gates[:, c0 + 2:c0 + 3] * o_w[r])
            h = k * NSA_GROUP + g
            o_ref[:, h * HEAD_DIM:(h + 1) * HEAD_DIM] = y.astype(o_ref.dtype)

    for i in range(n_slc):
        wo_ref[0, pl.ds(i, wb - tn, stride=n_slc), :] = wbuf[slot, i, tn:wb, :]
        wo_ref[0, pl.ds((wb - tn) * n_slc + i, tn, stride=n_slc), :] = wn_ref[:, i * HEAD_DIM:(i + 1) * HEAD_DIM]


def _nsa_sample(z, row0, n_seq, tn, pages, page0, comp_pages, cache_win_l, layer, page_table, rel_bias, cols,
                out_buf, win_buf):
    n_pages = page_table.shape[1]
    page_size = pages.shape[1]
    assert page_size == 2 * CMP_BLOCK and tn == SUBLANES and row0 % tn == 0
    past = n_pages * page_size
    ncs = 2 * n_pages
    ncp = _round_up(ncs + 1, LANES)
    wb = cache_win_l.shape[2]
    assert wb == WINDOW and past >= WINDOW
    half_cols = 2 * NSA_KV_HEADS * HEAD_DIM
    n_slc = 2 * NSA_KV_HEADS
    rb = row0 // tn
    grid_spec = pltpu.PrefetchScalarGridSpec(
        num_scalar_prefetch=1,
        grid=(n_seq,),
        in_specs=[
            pl.BlockSpec(memory_space=pltpu.SMEM),
            pl.BlockSpec((tn, NSA_WIDTH), lambda b, pt: (rb + b, cols["q"] // NSA_WIDTH)),
            pl.BlockSpec((tn, 2 * half_cols), lambda b, pt: (rb + b, cols["kv"] // (2 * half_cols))),
            pl.BlockSpec((tn, half_cols), lambda b, pt: (rb + b, cols["win"] // half_cols)),
            pl.BlockSpec((tn, 2 * LANES), lambda b, pt: (rb + b, cols["ng"] // (2 * LANES))),
            pl.BlockSpec(memory_space=pl.ANY),
            pl.BlockSpec(memory_space=pl.ANY),
            pl.BlockSpec(memory_space=pl.ANY),
            pl.BlockSpec(memory_space=pl.ANY),
            pl.BlockSpec(memory_space=pl.ANY),
        ],
        out_specs=[pl.BlockSpec((tn, NSA_WIDTH), lambda b, pt: (rb + b, 0)),
                   pl.BlockSpec((1, wb * n_slc, HEAD_DIM), lambda b, pt: (layer * n_seq + b, 0, 0))],
        scratch_shapes=[
            pltpu.VMEM((2, n_slc, n_pages, page_size, HEAD_DIM), F32),
            pltpu.VMEM((2, n_pages) + comp_pages.shape[1:], F32),
            pltpu.VMEM((2, n_slc, wb, HEAD_DIM), F32),
            pltpu.VMEM((ncp, past + LANES), BF16),
            pltpu.SemaphoreType.DMA((2,)),
            pltpu.SemaphoreType.DMA((2,)),
            pltpu.SemaphoreType.DMA((2,)),
        ],
    )
    return pl.pallas_call(
        functools.partial(_nsa_sample_kernel, n_seq=n_seq, n_pages=n_pages, page0=page0, layer=layer, tn=tn),
        grid_spec=grid_spec,
        out_shape=[jax.ShapeDtypeStruct(out_buf.shape, out_buf.dtype),
                   jax.ShapeDtypeStruct(win_buf.shape, win_buf.dtype)],
        input_output_aliases={9: 0, 10: 1},
        compiler_params=_cparams(("arbitrary",)),
        name="nsa_sample",
    )(page_table.reshape(-1), rel_bias, z, z, z, z, cache_win_l, pages, comp_pages, out_buf, win_buf)


def _cumsum8(x):
    row = lax.broadcasted_iota(jnp.int32, x.shape, 0)
    for sh in (1, 2, 4):
        x = x + jnp.where(row >= sh, pltpu.roll(x, sh, 0), 0.0)
    return x


def _mixer_kernel(zcb_ref, zcc_ref, zch_ref, zhg_ref, cw_ref, cs_ref, hs_ref, lb_ref, hn_ref, _yb_buf, _yc_buf,
                  yb_ref, yc_ref, cso_ref, hso_ref, uprev, st, qf_s, k_s, lf_s, *, layer, tt):
    ti = pl.program_id(1)
    nt = pl.num_programs(1)

    @pl.when(ti == 0)
    def _():
        uprev[...] = jnp.zeros(uprev.shape, F32)
        uprev[SUBLANES - (CONV_K - 1):SUBLANES, :] = cs_ref[0]
        for h in range(HG_HEADS):
            st[h] = hs_ref[0, h].T

    cb = zcb_ref[...]
    u = zcc_ref[...] * zch_ref[...]
    ext = jnp.concatenate([uprev[...], u], axis=0)
    y = cw_ref[CONV_K - 1:CONV_K, :] * u
    for j in range(CONV_K - 1):
        shift = CONV_K - 1 - j
        y = y + cw_ref[j:j + 1, :] * pltpu.roll(ext, shift, 0)[SUBLANES:, :]
    yb_ref[...] = (cb * y).astype(yb_ref.dtype)
    tail = ext[tt:tt + SUBLANES, :]
    uprev[...] = tail

    @pl.when(ti == nt - 1)
    def _():
        cso_ref[0] = pltpu.roll(tail, CONV_K - 1, 0)[0:CONV_K - 1, :]

    hw = HG_HEADS * HG_DK
    lg = lb_ref[...]
    mx = jnp.max(lg, axis=0, keepdims=True)
    ex = jnp.exp(lg - mx)
    sm = ex / jnp.sum(ex, axis=0, keepdims=True)
    cs = sm[0:1]
    for i in range(1, layer + 1):
        cs = cs + sm[i:i + 1]
    lb = cs - sm[0:1]
    fx = zhg_ref[:, hw:2 * hw]
    lb_pos = lb > 0.0
    log_lb = jnp.log(jnp.where(lb_pos, lb, 1.0))
    log_sig = jnp.minimum(fx, 0.0) - jnp.log1p(jnp.exp(-jnp.abs(fx)))
    a2 = jnp.log1p(-lb) + log_sig
    lae = jnp.maximum(log_lb, a2) + jnp.log1p(jnp.exp(-jnp.abs(log_lb - a2)))
    lf_s[...] = jnp.where(lb_pos, lae, log_sig)
    k_s[...] = (1.0 - lb) * jax.nn.sigmoid(-fx)
    hq = zhg_ref[:, 0:hw]
    qf_s[...] = hq * jax.nn.sigmoid(hq)
    norm_g = hn_ref[...]
    ts = lax.broadcasted_iota(jnp.int32, (HG_BLOCK, HG_BLOCK, 1), 0)
    ss = lax.broadcasted_iota(jnp.int32, (HG_BLOCK, HG_BLOCK, 1), 1)
    causal = ss <= ts

    def block(i, c):
        r0 = pl.multiple_of(i * HG_BLOCK, HG_BLOCK)
        rs = pl.ds(r0, HG_BLOCK)
        for h in range(HG_HEADS):
            kc = slice(h * HG_DK, (h + 1) * HG_DK)
            vcol = slice(2 * hw + h * HG_DV, 2 * hw + (h + 1) * HG_DV)
            gcol = slice(2 * hw + HG_HEADS * HG_DV + h * HG_DV, 2 * hw + HG_HEADS * HG_DV + (h + 1) * HG_DV)
            qf = qf_s[rs, kc]
            kk = k_s[rs, kc]
            g = _cumsum8(lf_s[rs, kc])
            v = zhg_ref[rs, vcol]
            gl = g[HG_BLOCK - 1:HG_BLOCK, :]
            st_h = st[h]
            o = _dot_nt((qf * jnp.exp(g)).astype(BF16), st_h.astype(BF16))
            diff = jnp.where(causal, g[:, None, :] - g[None, :, :], 0.0)
            x3 = jnp.where(causal, qf[:, None, :] * kk[None, :, :] * jnp.exp(diff), 0.0)
            a = jnp.sum(x3, axis=-1, keepdims=True)
            o = o + jnp.sum(a * v[None, :, :], axis=1)
            kd = kk * jnp.exp(gl - g)
            st[h] = st_h * jnp.exp(gl) + _dot_tn(v.astype(BF16), kd.astype(BF16))
            on = o * lax.rsqrt(jnp.mean(o * o, axis=-1, keepdims=True) + NORM_EPS) * norm_g
            gx = zhg_ref[rs, gcol]
            yc_ref[rs, h * HG_DV:(h + 1) * HG_DV] = (on * (gx * jax.nn.sigmoid(gx))).astype(yc_ref.dtype)
        return c

    n_blk = tt // HG_BLOCK
    lax.fori_loop(0, n_blk, block, 0, unroll=min(4, n_blk))

    @pl.when(ti == nt - 1)
    def _():
        for h in range(HG_HEADS):
            hso_ref[0, h] = st[h].T


def _mixer(z, row0, n_seq, seq, conv_w_l, conv_state, hg_state, lb_logits, hg_norm_l, layer, cols, yb_buf, yc_buf):
    cw = conv_w_l.shape[-1]
    hgw = 2 * HG_HEADS * HG_DK + 2 * HG_HEADS * HG_DV
    tt = _pick_tile(seq, 256, SUBLANES)
    nt = seq // tt
    rb = row0 // tt
    assert row0 % tt == 0 and cols["cv"] % cw == 0 and cols["hg"] % hgw == 0
    depth = lb_logits.shape[0]
    cvb = cols["cv"] // cw
    return pl.pallas_call(
        functools.partial(_mixer_kernel, layer=layer, tt=tt),
        grid=(n_seq, nt),
        in_specs=[
            pl.BlockSpec((tt, cw), lambda b, t: (rb + b * nt + t, cvb)),
            pl.BlockSpec((tt, cw), lambda b, t: (rb + b * nt + t, cvb + 1)),
            pl.BlockSpec((tt, cw), lambda b, t: (rb + b * nt + t, cvb + 2)),
            pl.BlockSpec((tt, hgw), lambda b, t: (rb + b * nt + t, cols["hg"] // hgw)),
            pl.BlockSpec((CONV_K, cw), lambda b, t: (0, 0)),
            pl.BlockSpec((1, CONV_K - 1, cw), lambda b, t: (b, 0, 0)),
            pl.BlockSpec((1, HG_HEADS, HG_DK, HG_DV), lambda b, t: (b, 0, 0, 0)),
            pl.BlockSpec((depth, HG_HEADS * HG_DK), lambda b, t: (0, 0)),
            pl.BlockSpec((1, HG_DV), lambda b, t: (0, 0)),
            pl.BlockSpec(memory_space=pl.ANY),
            pl.BlockSpec(memory_space=pl.ANY),
        ],
        out_specs=[
            pl.BlockSpec((tt, cw), lambda b, t: (rb + b * nt + t, 0)),
            pl.BlockSpec((tt, HG_HEADS * HG_DV), lambda b, t: (rb + b * nt + t, 0)),
            pl.BlockSpec((1, CONV_K - 1, cw), lambda b, t: (b, 0, 0)),
            pl.BlockSpec((1, HG_HEADS, HG_DK, HG_DV), lambda b, t: (b, 0, 0, 0)),
        ],
        out_shape=[
            jax.ShapeDtypeStruct(yb_buf.shape, yb_buf.dtype),
            jax.ShapeDtypeStruct(yc_buf.shape, yc_buf.dtype),
            jax.ShapeDtypeStruct((n_seq, CONV_K - 1, cw), F32),
            jax.ShapeDtypeStruct((n_seq, HG_HEADS, HG_DK, HG_DV), F32),
        ],
        input_output_aliases={9: 0, 10: 1},
        scratch_shapes=[
            pltpu.VMEM((SUBLANES, cw), F32),
            pltpu.VMEM((HG_HEADS, HG_DV, HG_DK), F32),
            pltpu.VMEM((tt, HG_HEADS * HG_DK), F32),
            pltpu.VMEM((tt, HG_HEADS * HG_DK), F32),
            pltpu.VMEM((tt, HG_HEADS * HG_DK), F32),
        ],
        compiler_params=_cparams(("arbitrary", "arbitrary")),
        name="mixer_conv_hgrn",
    )(z, z, z, z, conv_w_l, conv_state, hg_state, lb_logits, hg_norm_l.reshape(1, HG_DV), yb_buf, yc_buf)


def _ep_identity(prods, extras):
    return prods[0]


def _ep_mix(prods, extras):
    ga, gb, gc = (jax.nn.sigmoid(e) for e in extras)
    return ga * prods[0] + gb * prods[1] + gc * prods[2]


def _ep_residual(prods, extras):
    return extras[0] + prods[0]


def _ep_swiglu(prods, extras):
    return prods[0] * jax.nn.sigmoid(prods[0]) * prods[1]


def _ep_ple(prods, extras):
    return extras[0] + jax.nn.sigmoid(prods[0]) * prods[1]


def kernel(x_prompt, x_sample, p_prompt, p_sample, cache_kv, cache_win, state_conv, state_hgrn, page_table,
           rel_bias, g_mix, w_in, phi_pe, phi_w1, phi_w2, conv_w, hg_lb_logits, hg_norm, w_pa, w_pb, w_pc, w_o,
           g_ffn, w_gate, w_up, w_down, w_pg, w_pp, g_final):
    bp, seq, d = x_prompt.shape
    bs, tn, _ = x_sample.shape
    depth = w_in.shape[0]
    n_in = w_in.shape[-1]
    cw = conv_w.shape[-1]
    d_ff = w_gate.shape[-1]
    ple = w_pp.shape[1]
    n_pool, page_size = cache_kv.shape[1], cache_kv.shape[2]
    mp, ms = bp * seq, bs * tn
    m = mp + ms
    kvw = 6 * NSA_KV_HEADS * HEAD_DIM
    hgw = 2 * HG_HEADS * HG_DK + 2 * HG_HEADS * HG_DV
    assert n_in == NSA_WIDTH + kvw + N_GATES + 3 * cw + hgw + 3 * d
    assert seq % Q_BLOCK == 0 and seq >= WINDOW

    o_ng = NSA_WIDTH + kvw
    cols = {"q": 0, "kv": NSA_WIDTH, "win": NSA_WIDTH + N_KV_ROWS * NSA_KV_HEADS * HEAD_DIM,
            "cv": o_ng, "hg": o_ng + 3 * cw, "mg": o_ng + 3 * cw + hgw, "ng": n_in - N_GATES}
    nz = cols["ng"] + GATE_PAD

    tm = _pick_tile(m, 1024, 256) if m % 256 == 0 else _pick_tile(m, 1024, SUBLANES)
    h = jnp.concatenate([x_prompt.reshape(mp, d), x_sample.reshape(ms, d)], axis=0)
    n_rows = N_KV_ROWS * NSA_KV_HEADS
    pages = cache_kv.reshape(depth * n_pool, page_size, n_rows, HEAD_DIM)
    pool_blocks = cache_kv.reshape(depth * n_pool * (page_size // CMP_BLOCK), CMP_BLOCK, n_rows, HEAD_DIM)
    nb_pool = n_pool * (page_size // CMP_BLOCK)
    cache_win_r = cache_win.reshape(depth, bs, cache_win.shape[2], 2 * NSA_KV_HEADS, HEAD_DIM)
    zeros_conv = jnp.zeros((bp, CONV_K - 1, cw), F32)
    zeros_hg = jnp.zeros((bp, HG_HEADS, HG_DK, HG_DV), F32)

    def w2d(w):
        return w.reshape(w.shape[0] * w.shape[1], w.shape[2])

    w_in_t = jnp.swapaxes(w_in, 1, 2)
    keep = min(WINDOW, seq)
    kv_p = jnp.zeros((depth * mp * n_rows, HEAD_DIM), F32)
    kv_s = jnp.zeros((depth * ms * n_rows, HEAD_DIM), F32)
    win_p = jnp.zeros((depth * bp * keep * (n_rows // 2), HEAD_DIM), F32)
    wb = cache_win.shape[2]
    win_s = jnp.zeros((depth * bs, wb * (n_rows // 2), HEAD_DIM), F32)
    conv_p, conv_s, hg_p, hg_s = [], [], [], []
    for l in range(depth):
        xn = _rmsnorm(h, g_mix[l], BF16)
        z = _fused_matmul([xn], [0], [(_w_in_relayout(w_in_t, l, o_ng, nz), 0)], [], _ep_identity, nz, F32,
                          tm, _pick_tile(nz, 2560, 2 * LANES), "in_proj", w_rows_out=True)

        comp_prompt = _compress(z, mp // CMP_BLOCK, cols["kv"] // HEAD_DIM, phi_pe[l], phi_w1[l], phi_w2[l],
                                _pick_tile(mp // CMP_BLOCK, 256, SUBLANES), "compress_prompt")
        cb_pool = _pick_tile(nb_pool, 64, SUBLANES)
        comp_pool = _compress_rows(pool_blocks, l * nb_pool, nb_pool, phi_pe[l], phi_w1[l], phi_w2[l],
                                   cb_pool, "compress_pool")
        comp_pages = comp_pool.reshape(n_pool, (page_size // CMP_BLOCK) * n_rows, 2 * HEAD_DIM)
        ya = _nsa_prompt(z, comp_prompt, rel_bias, bp, seq, cols, jnp.zeros((m, NSA_WIDTH), F32))
        ya, win_s = _nsa_sample(z, mp, bs, tn, pages, l * n_pool, comp_pages, cache_win_r, l, page_table,
                                rel_bias, cols, ya, win_s)
        yb, yc, cs_p, hs_p = _mixer(z, 0, bp, seq, conv_w[l], zeros_conv, zeros_hg, hg_lb_logits,
                                    hg_norm[l], l, cols, jnp.zeros((m, cw), F32),
                                    jnp.zeros((m, HG_HEADS * HG_DV), F32))
        yb, yc, cs_s, hs_s = _mixer(z, mp, bs, tn, conv_w[l], state_conv[l], state_hgrn[l], hg_lb_logits,
                                    hg_norm[l], l, cols, yb, yc)

        tn_d = _pick_tile(d, 1024, LANES)
        tn_mix = tn_d
        mgb = cols["mg"] // tn_mix
        mixed = _fused_matmul([ya, yb, yc], [0, 1, 2],
                              [(w2d(w_pa).astype(BF16), l), (w2d(w_pb).astype(BF16), l), (w2d(w_pc).astype(BF16), l)],
                              [(z, mgb), (z, mgb + d // tn_mix), (z, mgb + 2 * (d // tn_mix))],
                              _ep_mix, d, BF16, tm, tn_mix, "mix_proj")
        h = _fused_matmul([mixed], [0], [(w2d(w_o), l)], [(h, 0)], _ep_residual, d, F32, tm, tn_d, "out_proj")
        xf = _rmsnorm(h, g_ffn[l], BF16)
        act = _fused_matmul([xf], [0, 0], [(w2d(w_gate), l), (w2d(w_up), l)], [], _ep_swiglu, d_ff, BF16,
                            tm, _pick_tile(d_ff, 512, LANES), "ffn_up")
        h = _fused_matmul([act], [0], [(w_down[l].astype(BF16), 0)], [(h, 0)], _ep_residual, d, F32,
                          tm, _pick_tile(d, 1024, LANES), "ffn_down", single_buffer_w=True)
        p_all = jnp.concatenate([p_prompt[l].reshape(mp, ple), p_sample[l].reshape(ms, ple)], axis=0)
        h = _fused_matmul([h, p_all], [0, 1], [(w2d(w_pg), l), (w2d(w_pp), l)], [(h, 0)], _ep_ple, d, F32,
                          tm, tn_d, "ple_gate")

        kv_p = _rows_out(z, cols["kv"], n_rows, 1, mp, lambda s: 0, kv_p, l, "kv_rows_prompt")
        kv_s = _rows_out(z, cols["kv"], n_rows, 1, ms, lambda s: mp, kv_s, l, "kv_rows_sample")
        win_p = _rows_out(z, cols["win"], n_rows // 2, bp, keep, lambda s: s * seq + seq - keep, win_p, l,
                          "win_rows_prompt")
        conv_p.append(cs_p); conv_s.append(cs_s); hg_p.append(hs_p); hg_s.append(hs_s)

    y_p, y_s = _rmsnorm_split(h, g_final, mp)
    return (y_p.reshape(bp, seq, d), y_s.reshape(bs, tn, d),
            kv_p.reshape(depth, bp, seq, N_KV_ROWS, NSA_KV_HEADS, HEAD_DIM),
            kv_s.reshape(depth, bs, tn, N_KV_ROWS, NSA_KV_HEADS, HEAD_DIM),
            win_p.reshape(depth, bp, keep, 2, NSA_KV_HEADS, HEAD_DIM),
            win_s.reshape(depth, bs, wb, 2, NSA_KV_HEADS, HEAD_DIM), jnp.stack(conv_p), jnp.stack(conv_s),
            jnp.stack(hg_p), jnp.stack(hg_s))
```

```python
import functools
import math

import numpy as np
import jax
import jax.numpy as jnp
from jax import lax
from jax.experimental import pallas as pl
from jax.experimental.pallas import tpu as pltpu

F32 = jnp.float32
BF16 = jnp.bfloat16

LANES = 128
SUBLANES = 8
VMEM_LIMIT_BYTES = 56 * 1024 * 1024

NSA_HEADS = 8
NSA_KV_HEADS = 2
NSA_GROUP = NSA_HEADS // NSA_KV_HEADS
HEAD_DIM = 128
NSA_WIDTH = NSA_HEADS * HEAD_DIM
CMP_BLOCK = 64
N_SELECT = 16
WINDOW = 512
Q_BLOCK = 128
FAR_CHUNK = 1024
FORCE_SCORE = 1.0e4
MASK_VALUE = -1.0e30
N_KV_ROWS = 4
CONV_K = 3
HG_HEADS = 4
HG_DK = 128
HG_DV = 128
HG_BLOCK = SUBLANES
REL_BUCKETS = 32
REL_MAX_DIST = 128
NORM_EPS = 1e-6
N_GATES = 3 * NSA_HEADS
GATE_PAD = 4 * LANES


def _round_up(x, m):
    return -(-x // m) * m


def _pick_tile(n, pref, unit):
    if n <= pref:
        return n
    best = None
    for t in range(unit, pref + 1, unit):
        if n % t == 0:
            best = t
    assert best is not None, (n, pref, unit)
    return best


def _divmod_const(x, n):
    if n & (n - 1) == 0:
        return lax.shift_right_logical(x, n.bit_length() - 1), x & (n - 1)
    return x // n, lax.rem(x, n)


def _t5_thresholds():
    n = np.arange(0, REL_MAX_DIST + 1)
    max_exact = REL_BUCKETS // 2
    nf = np.maximum(n, 1).astype(np.float32)
    ratio = np.log(nf / np.float32(max_exact)) / np.float32(math.log(REL_MAX_DIST / max_exact))
    large = max_exact + (ratio * np.float32(REL_BUCKETS - max_exact)).astype(np.int32)
    large = np.minimum(large, REL_BUCKETS - 1)
    bucket = np.where(n < max_exact, n, large)
    assert np.all(np.diff(bucket) >= 0) and bucket[-1] == REL_BUCKETS - 1
    return [int(np.argmax(bucket >= j)) for j in range(1, REL_BUCKETS)]


T5_THRESHOLDS = _t5_thresholds()


def _t5_bias(dist, tab_ref, heads):
    ge = [dist >= thr for thr in T5_THRESHOLDS]
    out = []
    for h in heads:
        b = jnp.full(dist.shape, tab_ref[0, h], F32)
        for j, m in enumerate(ge):
            b = jnp.where(m, tab_ref[j + 1, h], b)
        out.append(b)
    return out


def _dot(a, b):
    return jnp.dot(a, b, preferred_element_type=F32)


def _dot_nt(a, b):
    return lax.dot_general(a, b, (((1,), (1,)), ((), ())), preferred_element_type=F32)


def _dot_tn(a, b):
    return lax.dot_general(a, b, (((0,), (0,)), ((), ())), preferred_element_type=F32)


def _cparams(sem):
    return pltpu.CompilerParams(dimension_semantics=sem, vmem_limit_bytes=VMEM_LIMIT_BYTES)


def _rmsnorm_kernel(x_ref, g_ref, o_ref):
    x = x_ref[...]
    y = x * lax.rsqrt(jnp.mean(x * x, axis=-1, keepdims=True) + NORM_EPS)
    o_ref[...] = (y * g_ref[...]).astype(o_ref.dtype)


def _rmsnorm(x, g, out_dtype):
    m, d = x.shape
    tm = _pick_tile(m, 1024, SUBLANES)
    return pl.pallas_call(
        _rmsnorm_kernel,
        grid=(m // tm,),
        in_specs=[pl.BlockSpec((tm, d), lambda i: (i, 0)), pl.BlockSpec((1, d), lambda i: (0, 0))],
        out_specs=pl.BlockSpec((tm, d), lambda i: (i, 0)),
        out_shape=jax.ShapeDtypeStruct((m, d), out_dtype),
        compiler_params=_cparams(("parallel",)),
        name="rmsnorm",
    )(x, g.reshape(1, d))


def _rmsnorm_split_kernel(x_ref, g_ref, a_ref, b_ref, *, n_a):
    x = x_ref[...]
    y = (x * lax.rsqrt(jnp.mean(x * x, axis=-1, keepdims=True) + NORM_EPS)) * g_ref[...]
    i = pl.program_id(0)

    @pl.when(i == 0)
    def _():
        b_ref[...] = jnp.zeros(b_ref.shape, b_ref.dtype)

    @pl.when(i < n_a)
    def _():
        a_ref[...] = y

    @pl.when(i >= n_a)
    def _():
        b_ref[...] = y


def _rmsnorm_split(x, g, rows_a):
    m, d = x.shape
    tm = math.gcd(math.gcd(rows_a, m - rows_a), 256)
    n_a, n_b = rows_a // tm, (m - rows_a) // tm
    return pl.pallas_call(
        functools.partial(_rmsnorm_split_kernel, n_a=n_a),
        grid=(n_a + n_b,),
        in_specs=[pl.BlockSpec((tm, d), lambda i: (i, 0)), pl.BlockSpec((1, d), lambda i: (0, 0))],
        out_specs=[pl.BlockSpec((tm, d), lambda i: (jnp.minimum(i, n_a - 1), 0)),
                   pl.BlockSpec((tm, d), lambda i: (jnp.maximum(i - n_a, 0), 0))],
        out_shape=[jax.ShapeDtypeStruct((rows_a, d), F32), jax.ShapeDtypeStruct((m - rows_a, d), F32)],
        compiler_params=_cparams(("arbitrary",)),
        name="rmsnorm_split",
    )(x, g.reshape(1, d))


def _mm_kernel(*refs, n_x, pairs, n_extra, epilogue, cast_w, w_rows_out):
    xs = refs[:n_x]
    ws = refs[n_x:n_x + len(pairs)]
    extras = refs[n_x + len(pairs):n_x + len(pairs) + n_extra]
    o_ref = refs[n_x + len(pairs) + n_extra]
    wscr = refs[n_x + len(pairs) + n_extra + 1:]
    i = pl.program_id(1)

    if any(cast_w):
        @pl.when(i == 0)
        def _():
            s = 0
            for p in range(len(pairs)):
                if cast_w[p]:
                    wscr[s][...] = ws[p][...].astype(BF16)
                    s += 1

    xv = [x[...].astype(BF16) for x in xs]
    prods = []
    s = 0
    for p, xi in enumerate(pairs):
        if cast_w[p]:
            w = wscr[s][...]
            s += 1
        else:
            w = ws[p][...]
        prods.append(_dot_nt(xv[xi], w) if w_rows_out else _dot(xv[xi], w))
    o_ref[...] = epilogue(prods, [e[...] for e in extras]).astype(o_ref.dtype)


def _fused_matmul(xs, pairs, ws, extras, epilogue, n_out, out_dtype, tm, tn, name, w_rows_out=False,
                  single_buffer_w=False):
    m = xs[0].shape[0]
    assert m % tm == 0 and n_out % tn == 0
    cast_w = tuple(w.dtype != BF16 for w, _ in ws)
    assert not (w_rows_out and any(cast_w))
    in_specs = []
    for x in xs:
        in_specs.append(pl.BlockSpec((tm, x.shape[1]), lambda j, i: (i, 0)))
    for p, (w, roff) in enumerate(ws):
        k = xs[pairs[p]].shape[1]
        if w_rows_out:
            in_specs.append(pl.BlockSpec((tn, k), lambda j, i: (j, 0)))
            continue
        mode = {"pipeline_mode": pl.Buffered(1)} if single_buffer_w else {}
        in_specs.append(pl.BlockSpec((k, tn), functools.partial(lambda j, i, r: (r, j), r=roff), **mode))
    for e, coff in extras:
        in_specs.append(pl.BlockSpec((tm, tn), functools.partial(lambda j, i, c: (i, c + j), c=coff)))
    scratch = [pltpu.VMEM((xs[pairs[p]].shape[1], tn), BF16) for p in range(len(ws)) if cast_w[p]]
    kern = functools.partial(_mm_kernel, n_x=len(xs), pairs=tuple(pairs), n_extra=len(extras),
                             epilogue=epilogue, cast_w=cast_w, w_rows_out=w_rows_out)
    return pl.pallas_call(
        kern,
        grid=(n_out // tn, m // tm),
        in_specs=in_specs,
        out_specs=pl.BlockSpec((tm, tn), lambda j, i: (i, j)),
        out_shape=jax.ShapeDtypeStruct((m, n_out), out_dtype),
        scratch_shapes=scratch,
        compiler_params=_cparams(("arbitrary", "arbitrary")),
        name=name,
    )(*xs, *[w for w, _ in ws], *[e for e, _ in extras])


def _w_in_relayout_kernel(a_ref, b0_ref, b1_ref, b2_ref, g0_ref, g1_ref, g2_ref, o_ref, *, n_plain, n_main):
    i = pl.program_id(0)
    tr = o_ref.shape[0]
    gk = N_GATES // NSA_KV_HEADS

    @pl.when(i < n_plain)
    def _():
        o_ref[...] = a_ref[0].astype(BF16)

    @pl.when((i >= n_plain) & (i < n_main))
    def _():
        o_ref[...] = jnp.concatenate([a_ref[0, N_GATES:, :], b0_ref[0], b1_ref[0], b2_ref[0]], axis=0).astype(BF16)

    @pl.when(i == n_main)
    def _():
        d = o_ref.shape[1]
        g = jnp.concatenate([g0_ref[0], g1_ref[0], g2_ref[0], jnp.zeros((LANES - N_GATES, d), F32)], axis=0)
        row = lax.broadcasted_iota(jnp.int32, (LANES, d), 0)
        blocks = [jnp.where(row < gk, g if k == 0 else pltpu.roll(g, LANES - k * gk, 0), 0.0)
                  for k in range(NSA_KV_HEADS)]
        blocks.append(jnp.zeros((tr - NSA_KV_HEADS * LANES, d), F32))
        o_ref[...] = jnp.concatenate(blocks, axis=0).astype(BF16)


def _w_in_relayout(w_in_t, layer, o_ng, nz):
    _, n_in, d = w_in_t.shape
    tr = GATE_PAD
    ng0 = n_in - N_GATES
    assert N_GATES == 3 * SUBLANES and o_ng % tr == 0 and ng0 % tr == 0 and nz == ng0 + tr
    n_plain, n_main = o_ng // tr, ng0 // tr
    sub = tr // SUBLANES

    def small(base8):
        return pl.BlockSpec((1, SUBLANES, d), lambda i: (layer, base8(i), 0))

    return pl.pallas_call(
        functools.partial(_w_in_relayout_kernel, n_plain=n_plain, n_main=n_main),
        grid=(n_main + 1,),
        in_specs=[pl.BlockSpec((1, tr, d), lambda i: (layer, jnp.minimum(i, n_main - 1), 0))]
        + [small(functools.partial(lambda i, j: jnp.minimum(i + 1, n_main) * sub + j, j=j)) for j in range(3)]
        + [small(functools.partial(lambda i, j: o_ng // SUBLANES + j, j=j)) for j in range(3)],
        out_specs=pl.BlockSpec((tr, d), lambda i: (i, 0)),
        out_shape=jax.ShapeDtypeStruct((nz, d), BF16),
        compiler_params=_cparams(("arbitrary",)),
        name="w_in_relayout",
    )(w_in_t, w_in_t, w_in_t, w_in_t, w_in_t, w_in_t, w_in_t)


def _rows_out_kernel(z_ref, _buf, o_ref, *, n_kinds):
    tm = z_ref.shape[0]
    for r in range(n_kinds):
        o_ref[pl.ds(r, tm, stride=n_kinds), :] = z_ref[:, r * HEAD_DIM:(r + 1) * HEAD_DIM]


def _rows_out(z, col0, n_kinds, n_seq, rows_per_seq, first_row, buf, layer, name):
    width = n_kinds * HEAD_DIM
    tm = _pick_tile(rows_per_seq, 512, SUBLANES)
    nt = rows_per_seq // tm
    assert col0 % width == 0 and all(first_row(s) % tm == 0 for s in range(n_seq))
    stride = (first_row(1) - first_row(0)) // tm if n_seq > 1 else 0
    rb0 = first_row(0) // tm
    return pl.pallas_call(
        functools.partial(_rows_out_kernel, n_kinds=n_kinds),
        grid=(n_seq, nt),
        in_specs=[pl.BlockSpec((tm, width), lambda s, i: (rb0 + s * stride + i, col0 // width)),
                  pl.BlockSpec(memory_space=pl.ANY)],
        out_specs=pl.BlockSpec((tm * n_kinds, HEAD_DIM), lambda s, i: ((layer * n_seq + s) * nt + i, 0)),
        out_shape=jax.ShapeDtypeStruct(buf.shape, buf.dtype),
        input_output_aliases={1: 0},
        compiler_params=_cparams(("arbitrary", "arbitrary")),
        name=name,
    )(z, buf)


def _compress_kernel(blk_ref, pe_ref, w1_ref, w2_ref, o_ref, x2d, w1b):
    @pl.when(pl.program_id(1) == 0)
    def _():
        w1b[...] = w1_ref[0].astype(BF16)

    cb = x2d.shape[0]
    for tau in range(CMP_BLOCK):
        x2d[:, tau * HEAD_DIM:(tau + 1) * HEAD_DIM] = (
            blk_ref[pl.ds(tau, cb, stride=CMP_BLOCK), :] + pe_ref[0, tau:tau + 1, :]).astype(BF16)
    h1 = _dot(x2d[...], w1b[...])
    h1 = h1 * jax.nn.sigmoid(h1)
    o_ref[...] = _dot(h1.astype(BF16), w2_ref[0].astype(BF16))


def _compress(tokens, n_blocks, col_block0, pe, w1, w2, cb, name):
    n_rk = 2 * NSA_KV_HEADS
    return pl.pallas_call(
        _compress_kernel,
        grid=(n_rk, n_blocks // cb),
        in_specs=[
            pl.BlockSpec((cb * CMP_BLOCK, HEAD_DIM), lambda rk, t: (t, col_block0 + rk)),
            pl.BlockSpec((1, CMP_BLOCK, HEAD_DIM), lambda rk, t: (rk // NSA_KV_HEADS, 0, 0)),
            pl.BlockSpec((1, CMP_BLOCK * HEAD_DIM, HEAD_DIM), lambda rk, t: (rk // NSA_KV_HEADS, 0, 0)),
            pl.BlockSpec((1, HEAD_DIM, HEAD_DIM), lambda rk, t: (rk // NSA_KV_HEADS, 0, 0)),
        ],
        out_specs=pl.BlockSpec((cb, HEAD_DIM), lambda rk, t: (t, rk)),
        out_shape=jax.ShapeDtypeStruct((n_blocks, n_rk * HEAD_DIM), F32),
        scratch_shapes=[pltpu.VMEM((cb, CMP_BLOCK * HEAD_DIM), BF16),
                        pltpu.VMEM((CMP_BLOCK * HEAD_DIM, HEAD_DIM), BF16)],
        compiler_params=_cparams(("arbitrary", "arbitrary")),
        name=name,
    )(tokens, pe, w1, w2)


def _compress_rows_kernel(pe_ref, w1_ref, w2_ref, tok_hbm, o_ref, tbuf, x2d, sem, *, block0, n_steps, n_used):
    t = pl.program_id(0)
    slot = lax.rem(t, 2)
    cb = tbuf.shape[1]

    def fetch(step, dst_slot):
        return pltpu.make_async_copy(tok_hbm.at[pl.ds(block0 + step * cb, cb), :, 0:n_used, :],
                                     tbuf.at[dst_slot, :, :, 0:n_used, :], sem.at[dst_slot])

    @pl.when(t == 0)
    def _():
        tbuf[...] = jnp.zeros(tbuf.shape, F32)
        fetch(0, 0).start()

    @pl.when(t + 1 < n_steps)
    def _():
        fetch(t + 1, 1 - slot).start()

    fetch(t, slot).wait()
    for tau in range(CMP_BLOCK):
        tok = tbuf[slot, :, tau] + pe_ref[tau]
        x2d[:, tau * HEAD_DIM:(tau + 1) * HEAD_DIM] = tok.reshape(cb * SUBLANES, HEAD_DIM).astype(BF16)
    h1 = _dot(x2d[...], w1_ref[...])
    h1 = h1 * jax.nn.sigmoid(h1)
    o_ref[...] = _dot(h1.astype(BF16), w2_ref[...])


def _compress_rows(blocks4d, block0, n_blocks, pe, w1, w2, cb, name):
    n_kinds = blocks4d.shape[2]
    assert n_kinds == SUBLANES and n_blocks % cb == 0
    kinds_k = NSA_KV_HEADS
    zero = jnp.zeros((CMP_BLOCK, n_kinds - 2 * kinds_k, HEAD_DIM), F32)
    pe_rows = jnp.concatenate([jnp.repeat(pe[0][:, None, :], kinds_k, axis=1),
                               jnp.repeat(pe[1][:, None, :], kinds_k, axis=1), zero], axis=1)
    w1_both = jnp.concatenate([w1[0], w1[1]], axis=1).astype(BF16)
    zw = jnp.zeros((HEAD_DIM, HEAD_DIM), F32)
    w2_both = jnp.concatenate([jnp.concatenate([w2[0], zw], axis=1),
                               jnp.concatenate([zw, w2[1]], axis=1)], axis=0).astype(BF16)
    return pl.pallas_call(
        functools.partial(_compress_rows_kernel, block0=block0, n_steps=n_blocks // cb, n_used=2 * kinds_k),
        grid=(n_blocks // cb,),
        in_specs=[
            pl.BlockSpec((CMP_BLOCK, n_kinds, HEAD_DIM), lambda t: (0, 0, 0)),
            pl.BlockSpec((CMP_BLOCK * HEAD_DIM, 2 * HEAD_DIM), lambda t: (0, 0)),
            pl.BlockSpec((2 * HEAD_DIM, 2 * HEAD_DIM), lambda t: (0, 0)),
            pl.BlockSpec(memory_space=pl.ANY),
        ],
        out_specs=pl.BlockSpec((cb * n_kinds, 2 * HEAD_DIM), lambda t: (t, 0)),
        out_shape=jax.ShapeDtypeStruct((n_blocks * n_kinds, 2 * HEAD_DIM), F32),
        scratch_shapes=[pltpu.VMEM((2, cb, CMP_BLOCK, n_kinds, HEAD_DIM), F32),
                        pltpu.VMEM((cb * n_kinds, CMP_BLOCK * HEAD_DIM), BF16),
                        pltpu.SemaphoreType.DMA((2,))],
        compiler_params=_cparams(("arbitrary",)),
        name=name,
    )(pe_rows, w1_both, w2_both, blocks4d)


def _masked_softmax_unnorm(s, mask):
    s = jnp.where(mask, s, MASK_VALUE)
    m = jnp.max(s, axis=-1, keepdims=True)
    e = jnp.where(mask, jnp.exp(s - m), 0.0)
    den = jnp.maximum(jnp.sum(e, axis=-1, keepdims=True), 1e-30)
    return e, den


def _select_blocks(score_t_ref, n_blocks):
    sc = score_t_ref[...]
    idx = lax.broadcasted_iota(jnp.int32, sc.shape, 0)

    def body(i, cnt):
        row = score_t_ref[pl.ds(i, 1), :]
        return cnt + jnp.where(idx > i, (row >= sc).astype(F32), (row > sc).astype(F32))

    cnt = lax.fori_loop(0, n_blocks, body, jnp.zeros(sc.shape, F32), unroll=math.gcd(n_blocks, 8))
    return ((cnt < N_SELECT) & (sc >= 0.0)).astype(F32)


def _select_blocks_rows(score, blk_of_lane, n_pos, blk_of_pos):
    cnt = jnp.zeros(score.shape, F32)
    for p in range(n_pos):
        col = score[:, p:p + 1]
        cnt = cnt + jnp.where(blk_of_lane > blk_of_pos(p), (col >= score).astype(F32), (col > score).astype(F32))
    return ((cnt < N_SELECT) & (score >= 0.0)).astype(F32)


def _nsa_prompt_kernel(tab_ref, q_ref, ks_ref, vs_ref, kw_ref, vw_ref, gt_ref, kc_ref, vc_ref, _out_buf, o_ref,
                       ksb, vsb, kwb, vwb, kcb, vcb, ebuf, bnear, bcn, sct, sbuf, mrun, lrun, acc, *, seq):
    k = pl.program_id(1)
    qb = pl.program_id(2)
    nc = seq // CMP_BLOCK
    ncp = kcb.shape[0]
    n8 = sct.shape[0]
    g_heads = [k * NSA_GROUP + g for g in range(NSA_GROUP)]
    rows = NSA_GROUP * Q_BLOCK

    @pl.when(qb == 0)
    def _():
        ksb[0:Q_BLOCK, :] = jnp.zeros((Q_BLOCK, HEAD_DIM), BF16)
        ksb[Q_BLOCK:, :] = ks_ref[...].astype(BF16)
        vsb[0:Q_BLOCK, :] = jnp.zeros((Q_BLOCK, HEAD_DIM), BF16)
        vsb[Q_BLOCK:, :] = vs_ref[...].astype(BF16)
        kwb[0:WINDOW, :] = jnp.zeros((WINDOW, HEAD_DIM), BF16)
        kwb[WINDOW:, :] = kw_ref[...].astype(BF16)
        vwb[0:WINDOW, :] = jnp.zeros((WINDOW, HEAD_DIM), BF16)
        vwb[WINDOW:, :] = vw_ref[...].astype(BF16)
        kcb[...] = jnp.zeros(kcb.shape, BF16)
        vcb[...] = jnp.zeros(vcb.shape, BF16)
        kcb[0:nc, :] = kc_ref[...].astype(BF16)
        vcb[0:nc, :] = vc_ref[...].astype(BF16)
        blk = lax.broadcasted_iota(jnp.int32, ebuf.shape, 0)
        key = lax.broadcasted_iota(jnp.int32, ebuf.shape, 1) - Q_BLOCK
        ebuf[...] = ((key >= 0) & (lax.shift_right_logical(key, 6) == blk)).astype(BF16)
        ti = lax.broadcasted_iota(jnp.int32, (Q_BLOCK, Q_BLOCK), 0)
        tj = lax.broadcasted_iota(jnp.int32, (Q_BLOCK, Q_BLOCK), 1)
        b0 = _t5_bias(jnp.maximum(ti - tj, 0), tab_ref, g_heads)
        b1 = _t5_bias(Q_BLOCK + ti - tj, tab_ref, g_heads)
        tc = lax.broadcasted_iota(jnp.int32, (Q_BLOCK, ncp), 0)
        cc = lax.broadcasted_iota(jnp.int32, (Q_BLOCK, ncp), 1)
        dist_cn = jnp.where(cc < 4, tc + (CMP_BLOCK + 1) - CMP_BLOCK * cc, REL_MAX_DIST)
        bc = _t5_bias(jnp.maximum(dist_cn, 0), tab_ref, g_heads)
        for g in range(NSA_GROUP):
            far = tab_ref[REL_BUCKETS - 1, g_heads[g]]
            bnear[g * Q_BLOCK:(g + 1) * Q_BLOCK, 0:Q_BLOCK] = b1[g] - far
            bnear[g * Q_BLOCK:(g + 1) * Q_BLOCK, Q_BLOCK:] = b0[g] - far
            bcn[g * Q_BLOCK:(g + 1) * Q_BLOCK, :] = bc[g] - far

    def rep_g(x):
        return jnp.concatenate([x] * NSA_GROUP, axis=0)

    def fold_max(x):
        m = x[:, 0:LANES]
        for c in range(1, x.shape[1] // LANES):
            m = jnp.maximum(m, x[:, c * LANES:(c + 1) * LANES])
        return m

    def fold_sum(x):
        m = x[:, 0:LANES]
        for c in range(1, x.shape[1] // LANES):
            m = m + x[:, c * LANES:(c + 1) * LANES]
        return m

    t_col = lax.broadcasted_iota(jnp.int32, (Q_BLOCK, 1), 0)
    q_pos = qb * Q_BLOCK + t_col
    scale = HEAD_DIM ** -0.5
    q_all = jnp.concatenate(
        [(q_ref[:, g * HEAD_DIM:(g + 1) * HEAD_DIM] * scale).astype(BF16) for g in range(NSA_GROUP)], axis=0)
    gates = jax.nn.sigmoid(gt_ref[...])

    cidx = lax.broadcasted_iota(jnp.int32, (Q_BLOCK, ncp), 1)
    dist_c = q_pos - (cidx * CMP_BLOCK + (CMP_BLOCK - 1))
    mask_c = rep_g((dist_c >= 0) & (cidx < nc))
    bias_c = pltpu.roll(bcn[...], lax.rem(2 * qb - 2 + ncp, ncp), 1)
    e_c, den_c = _masked_softmax_unnorm(_dot_nt(q_all, kcb[...]) + bias_c, mask_c)
    p_c = e_c / den_c
    o_c = _dot(p_c.astype(BF16), vcb[...])
    imp = p_c[0:Q_BLOCK]
    for g in range(1, NSA_GROUP):
        imp = imp + p_c[g * Q_BLOCK:(g + 1) * Q_BLOCK]

    cur = lax.shift_right_logical(q_pos, 6)
    eligible = (cidx * CMP_BLOCK <= q_pos) & (cidx < nc)
    forced = (cidx == 0) | (cidx == cur) | (cidx == cur - 1)
    score = jnp.where(eligible, imp + jnp.where(forced, FORCE_SCORE, 0.0), -1.0)
    score = jnp.where(cidx < nc, score, -2.0)
    score_t = jnp.concatenate([score[:, c * LANES:(c + 1) * LANES].T for c in range(ncp // LANES)], axis=0)
    sct[...] = score_t[0:n8, :]
    sel_t = _select_blocks(sct, nc)
    if n8 < ncp:
        sel_t = jnp.concatenate([sel_t, jnp.zeros((ncp - n8, Q_BLOCK), F32)], axis=0)
    sel = jnp.concatenate([sel_t[c * LANES:(c + 1) * LANES, :].T for c in range(ncp // LANES)], axis=1)
    sel = sel.astype(BF16)

    near = 2 * Q_BLOCK
    n0 = pl.multiple_of(qb * Q_BLOCK, Q_BLOCK)
    jn = lax.broadcasted_iota(jnp.int32, (Q_BLOCK, near), 1)
    ok_near = (_dot(sel, ebuf[:, pl.ds(n0, near)]) > 0.5) & (jn - Q_BLOCK <= t_col)
    mb_near = rep_g(jnp.where(ok_near, 0.0, MASK_VALUE))
    s_near = _dot_nt(q_all, ksb[pl.ds(n0, near), :]) + bnear[...] + mb_near
    far_end = (qb - 1) * Q_BLOCK
    n_chunks = _divmod_const(jnp.maximum(far_end + (FAR_CHUNK - 1), 0), FAR_CHUNK)[0]
    mrun[...] = jnp.full(mrun.shape, MASK_VALUE, F32)

    def pass1(c, carry):
        c0 = pl.multiple_of(c * FAR_CHUNK, FAR_CHUNK)
        p0 = pl.multiple_of(c * FAR_CHUNK + Q_BLOCK, Q_BLOCK)
        pos = c * FAR_CHUNK + lax.broadcasted_iota(jnp.int32, (Q_BLOCK, FAR_CHUNK), 1)
        ok = (_dot(sel, ebuf[:, pl.ds(p0, FAR_CHUNK)]) > 0.5) & (pos < far_end)
        mb = jnp.where(ok, 0.0, MASK_VALUE)
        s = _dot_nt(q_all, ksb[pl.ds(p0, FAR_CHUNK), :])
        for g in range(NSA_GROUP):
            r = slice(g * Q_BLOCK, (g + 1) * Q_BLOCK)
            sg = s[r] + mb
            sbuf[r, pl.ds(c0, FAR_CHUNK)] = sg
            mrun[r, :] = jnp.maximum(mrun[r, :], fold_max(sg))
        return carry

    lax.fori_loop(0, n_chunks, pass1, 0)
    m_s = jnp.maximum(jnp.max(mrun[...], axis=-1, keepdims=True), jnp.max(s_near, axis=-1, keepdims=True))
    e_near = jnp.exp(s_near - m_s)
    mrun[...] = jnp.broadcast_to(m_s, mrun.shape)
    lrun[...] = fold_sum(e_near)
    acc[...] = _dot(e_near.astype(BF16), vsb[pl.ds(n0, near), :])

    def pass2(c, carry):
        c0 = pl.multiple_of(c * FAR_CHUNK, FAR_CHUNK)
        p0 = pl.multiple_of(c * FAR_CHUNK + Q_BLOCK, Q_BLOCK)
        m_full = mrun[...]
        e = jnp.exp(sbuf[:, pl.ds(c0, FAR_CHUNK)] - jnp.concatenate([m_full] * (FAR_CHUNK // LANES), axis=1))
        lrun[...] = lrun[...] + fold_sum(e)
        acc[...] = acc[...] + _dot(e.astype(BF16), vsb[pl.ds(p0, FAR_CHUNK), :])
        return carry

    lax.fori_loop(0, n_chunks, pass2, 0)
    o_s = acc[...] / jnp.sum(lrun[...], axis=-1, keepdims=True)

    wlen = WINDOW + Q_BLOCK
    jw = lax.broadcasted_iota(jnp.int32, (Q_BLOCK, wlen), 1)
    dist_w = WINDOW + t_col - jw
    ok_w = (dist_w >= 0) & (dist_w <= WINDOW) & (qb * Q_BLOCK - WINDOW + jw >= 0)
    mb_w = rep_g(jnp.where(ok_w, 0.0, MASK_VALUE))
    bias_w = jnp.concatenate([jnp.zeros((rows, wlen - near), F32), bnear[...]], axis=1)
    s_w = _dot_nt(q_all, kwb[pl.ds(n0, wlen), :]) + bias_w + mb_w
    e_w = jnp.exp(s_w - jnp.max(s_w, axis=-1, keepdims=True))
    o_w = _dot(e_w.astype(BF16), vwb[pl.ds(n0, wlen), :]) / jnp.sum(e_w, axis=-1, keepdims=True)

    for g in range(NSA_GROUP):
        r = slice(g * Q_BLOCK, (g + 1) * Q_BLOCK)
        c0 = g * 3
        y = gates[:, c0:c0 + 1] * o_c[r] + gates[:, c0 + 1:c0 + 2] * o_s[r] + gates[:, c0 + 2:c0 + 3] * o_w[r]
        o_ref[:, g * HEAD_DIM:(g + 1) * HEAD_DIM] = y.astype(o_ref.dtype)


def _nsa_prompt(z, comp, rel_bias, n_batch, seq, cols, out_buf):
    nqb = seq // Q_BLOCK
    nc = seq // CMP_BLOCK
    ncp = _round_up(nc, LANES)
    n8 = _round_up(nc, SUBLANES)
    kvc = cols["kv"] // HEAD_DIM
    winc = cols["win"] // HEAD_DIM
    kh = NSA_KV_HEADS
    gw = NSA_GROUP * HEAD_DIM
    rows = NSA_GROUP * Q_BLOCK
    assert seq % FAR_CHUNK == 0

    def zspec(col_block_of_k):
        return pl.BlockSpec((seq, HEAD_DIM), lambda b, k, q: (b, col_block_of_k(k)))

    return pl.pallas_call(
        functools.partial(_nsa_prompt_kernel, seq=seq),
        grid=(n_batch, kh, nqb),
        in_specs=[
            pl.BlockSpec(memory_space=pltpu.SMEM),
            pl.BlockSpec((Q_BLOCK, gw), lambda b, k, q: (b * nqb + q, cols["q"] // gw + k)),
            zspec(lambda k: kvc + 2 * kh + k),
            zspec(lambda k: kvc + 3 * kh + k),
            zspec(lambda k: winc + k),
            zspec(lambda k: winc + kh + k),
            pl.BlockSpec((Q_BLOCK, LANES), lambda b, k, q: (b * nqb + q, cols["ng"] // LANES + k)),
            pl.BlockSpec((nc, HEAD_DIM), lambda b, k, q: (b, k)),
            pl.BlockSpec((nc, HEAD_DIM), lambda b, k, q: (b, kh + k)),
            pl.BlockSpec(memory_space=pl.ANY),
        ],
        out_specs=pl.BlockSpec((Q_BLOCK, gw), lambda b, k, q: (b * nqb + q, k)),
        out_shape=jax.ShapeDtypeStruct(out_buf.shape, out_buf.dtype),
        input_output_aliases={9: 0},
        scratch_shapes=[
            pltpu.VMEM((seq + Q_BLOCK, HEAD_DIM), BF16), pltpu.VMEM((seq + Q_BLOCK, HEAD_DIM), BF16),
            pltpu.VMEM((seq + WINDOW, HEAD_DIM), BF16), pltpu.VMEM((seq + WINDOW, HEAD_DIM), BF16),
            pltpu.VMEM((ncp, HEAD_DIM), BF16), pltpu.VMEM((ncp, HEAD_DIM), BF16),
            pltpu.VMEM((ncp, seq + Q_BLOCK), BF16),
            pltpu.VMEM((rows, 2 * Q_BLOCK), F32),
            pltpu.VMEM((rows, ncp), F32),
            pltpu.VMEM((n8, Q_BLOCK), F32),
            pltpu.VMEM((rows, seq), F32),
            pltpu.VMEM((rows, LANES), F32), pltpu.VMEM((rows, LANES), F32), pltpu.VMEM((rows, LANES), F32),
        ],
        compiler_params=_cparams(("arbitrary", "arbitrary", "arbitrary")),
        name="nsa_prompt",
    )(rel_bias, z, z, z, z, z, z, comp, comp, out_buf)


def _nsa_sample_kernel(pt_ref, tab_ref, q_ref, kvn_ref, wn_ref, gt_ref, win_hbm, pages_hbm, comp_hbm, _out_buf,
                       _win_buf, o_ref, wo_ref,
                       pbuf, cbuf, wbuf, ebuf, psem, csem, wsem, *, n_seq, n_pages, page0, layer, tn):
    b = pl.program_id(0)
    slot = lax.rem(b, 2)
    past = n_pages * 2 * CMP_BLOCK
    ncs = 2 * n_pages
    ns = ncs + 1
    ncp = ebuf.shape[0]
    nkeys = past + LANES
    wb = wbuf.shape[2]
    n_slc = 2 * NSA_KV_HEADS

    def block_of(pos):
        half, page = _divmod_const(pos, n_pages)
        return jnp.where(pos < ncs, 2 * page + half, pos)

    def page_copies(seq_idx, dst_slot, j):
        pid = pt_ref[seq_idx * n_pages + j]
        slc = tuple(
            pltpu.make_async_copy(pages_hbm.at[page0 + pid, :, n_slc + i, :], pbuf.at[dst_slot, i, j],
                                  psem.at[dst_slot])
            for i in range(n_slc))
        return slc + (pltpu.make_async_copy(comp_hbm.at[pid], cbuf.at[dst_slot, j], csem.at[dst_slot]),)

    def window_copies(seq_idx, dst_slot):
        return tuple(
            pltpu.make_async_copy(win_hbm.at[layer, seq_idx, :, i, :], wbuf.at[dst_slot, i], wsem.at[dst_slot])
            for i in range(n_slc))

    def start_seq(seq_idx, dst_slot):
        def body(j, c):
            for cp in page_copies(seq_idx, dst_slot, j):
                cp.start()
            return c
        lax.fori_loop(0, n_pages, body, 0)
        for cp in window_copies(seq_idx, dst_slot):
            cp.start()

    def wait_seq(seq_idx, dst_slot):
        def body(j, c):
            for cp in page_copies(seq_idx, dst_slot, j):
                cp.wait()
            return c
        lax.fori_loop(0, n_pages, body, 0)
        for cp in window_copies(seq_idx, dst_slot):
            cp.wait()

    @pl.when(b == 0)
    def _():
        start_seq(0, 0)
        pos = lax.broadcasted_iota(jnp.int32, ebuf.shape, 0)
        key = lax.broadcasted_iota(jnp.int32, ebuf.shape, 1)
        ebuf[...] = ((lax.shift_right_logical(key, 6) == block_of(pos)) & (pos <= ncs)).astype(BF16)

    @pl.when(b + 1 < n_seq)
    def _():
        start_seq(b + 1, 1 - slot)

    wait_seq(b, slot)

    t_col = lax.broadcasted_iota(jnp.int32, (tn, 1), 0)
    q_pos = past + t_col
    scale = HEAD_DIM ** -0.5
    rows = NSA_GROUP * tn

    def rep_g(x):
        return jnp.concatenate([x] * NSA_GROUP, axis=0)

    def pad_rows(x, n):
        if x.shape[0] == n:
            return x
        return jnp.concatenate([x, jnp.zeros((n - x.shape[0], x.shape[1]), x.dtype)], axis=0)

    pos1 = lax.broadcasted_iota(jnp.int32, (tn, ncp), 1)
    blk1 = block_of(pos1)
    dist_c = q_pos - (blk1 * CMP_BLOCK + (CMP_BLOCK - 1))
    mask_c = rep_g((dist_c >= 0) & (pos1 < ncs))
    cur = lax.shift_right_logical(q_pos, 6)
    eligible = (blk1 * CMP_BLOCK <= q_pos) & (pos1 <= ncs)
    forced = (blk1 == 0) | (blk1 == cur) | (blk1 == cur - 1)

    qk, o_cmp, scores = [], [], []
    for k in range(NSA_KV_HEADS):
        heads = [k * NSA_GROUP + g for g in range(NSA_GROUP)]
        q = jnp.concatenate(
            [q_ref[:, (k * NSA_GROUP + g) * HEAD_DIM:(k * NSA_GROUP + g + 1) * HEAD_DIM] for g in range(NSA_GROUP)],
            axis=0)
        q = (q * scale).astype(BF16)
        qk.append(q)
        kc = jnp.concatenate([cbuf[slot, :, half * SUBLANES + k, 0:HEAD_DIM] for half in range(2)], axis=0)
        vc = jnp.concatenate([cbuf[slot, :, half * SUBLANES + NSA_KV_HEADS + k, HEAD_DIM:] for half in range(2)],
                             axis=0)
        kc = pad_rows(kc, ncp).astype(BF16)
        vc = pad_rows(vc, ncp).astype(BF16)
        bias_c = jnp.concatenate(_t5_bias(jnp.maximum(dist_c, 0), tab_ref, heads), axis=0)
        s = _dot_nt(q, kc) + bias_c
        e, den = _masked_softmax_unnorm(s, mask_c)
        p = e / den
        o_cmp.append(_dot(p.astype(BF16), vc))
        imp = p[0:tn]
        for g in range(1, NSA_GROUP):
            imp = imp + p[g * tn:(g + 1) * tn]
        sc = jnp.where(eligible, imp + jnp.where(forced, FORCE_SCORE, 0.0), -1.0)
        scores.append(jnp.where(pos1 <= ncs, sc, -2.0))

    def block_of_static(p):
        return 2 * (p % n_pages) + p // n_pages if p < ncs else p

    blk_lane = block_of(lax.broadcasted_iota(jnp.int32, (1, ncp), 1))
    sel = _select_blocks_rows(jnp.concatenate(scores, axis=0), blk_lane, ns, block_of_static)

    kpos = lax.broadcasted_iota(jnp.int32, (tn, nkeys), 1)
    dist_s = q_pos - kpos
    causal_s = rep_g((dist_s >= 0) & (kpos < past + tn))
    near = 2 * LANES
    dist_near = jnp.maximum(dist_s[:, nkeys - near:], 0)
    jw = lax.broadcasted_iota(jnp.int32, (tn, wb + LANES), 1)
    dist_w = wb + t_col - jw
    mask_w = rep_g((dist_w >= 0) & (dist_w <= WINDOW) & (past - wb + jw >= 0) & (jw < wb + tn))
    dist_w_near = jnp.maximum(dist_w[:, wb + LANES - near:], 0)
    gates = jax.nn.sigmoid(gt_ref[...])

    for k in range(NSA_KV_HEADS):
        heads = [k * NSA_GROUP + g for g in range(NSA_GROUP)]
        far = jnp.concatenate([jnp.full((tn, 1), tab_ref[REL_BUCKETS - 1, h], F32) for h in heads], axis=0)
        q = qk[k]
        k_new = pad_rows(kvn_ref[:, (2 * NSA_KV_HEADS + k) * HEAD_DIM:(2 * NSA_KV_HEADS + k + 1) * HEAD_DIM], LANES)
        v_new = pad_rows(kvn_ref[:, (3 * NSA_KV_HEADS + k) * HEAD_DIM:(3 * NSA_KV_HEADS + k + 1) * HEAD_DIM], LANES)
        k_all = jnp.concatenate([pbuf[slot, k].reshape(past, HEAD_DIM), k_new], axis=0).astype(BF16)
        v_all = jnp.concatenate([pbuf[slot, NSA_KV_HEADS + k].reshape(past, HEAD_DIM), v_new], axis=0).astype(BF16)
        sel_k = rep_g(sel[k * tn:(k + 1) * tn, :]).astype(BF16)
        mask_s = (_dot(sel_k, ebuf[...]) > 0.5) & causal_s
        b_near = jnp.concatenate(_t5_bias(dist_near, tab_ref, heads), axis=0) - far
        bias_s = jnp.concatenate([jnp.zeros((rows, nkeys - near), F32), b_near], axis=1)
        s = _dot_nt(q, k_all) + far + bias_s
        e, den = _masked_softmax_unnorm(s, mask_s)
        o_s = _dot(e.astype(BF16), v_all) / den
        kw = jnp.concatenate([wbuf[slot, k],
                              pad_rows(wn_ref[:, k * HEAD_DIM:(k + 1) * HEAD_DIM], LANES)], axis=0).astype(BF16)
        vw = jnp.concatenate(
            [wbuf[slot, NSA_KV_HEADS + k],
             pad_rows(wn_ref[:, (NSA_KV_HEADS + k) * HEAD_DIM:(NSA_KV_HEADS + k + 1) * HEAD_DIM], LANES)],
            axis=0).astype(BF16)
        bw_near = jnp.concatenate(_t5_bias(dist_w_near, tab_ref, heads), axis=0) - far
        bias_w = jnp.concatenate([jnp.zeros((rows, wb + LANES - near), F32), bw_near], axis=1)
        sw = _dot_nt(q, kw) + far + bias_w
        ew, denw = _masked_softmax_unnorm(sw, mask_w)
        o_w = _dot(ew.astype(BF16), vw) / denw
        for g in range(NSA_GROUP):
            c0 = k * LANES + g * 3
            r = slice(g * tn, (g + 1) * tn)
            y = (gates[:, c0:c0 + 1] * o_cmp[k][r] + gates[:, c0 + 1:c0 + 2] * o_s[r]
                 + gates[:, c0 + 2:c0 + 3] * o_w[r])
            h = k * NSA_GROUP + g
            o_ref[:, h * HEAD_DIM:(h + 1) * HEAD_DIM] = y.astype(o_ref.dtype)

    for i in range(n_slc):
        wo_ref[0, pl.ds(i, wb - tn, stride=n_slc), :] = wbuf[slot, i, tn:wb, :]
        wo_ref[0, pl.ds((wb - tn) * n_slc + i, tn, stride=n_slc), :] = wn_ref[:, i * HEAD_DIM:(i + 1) * HEAD_DIM]


def _nsa_sample(z, row0, n_seq, tn, pages, page0, comp_pages, cache_win_l, layer, page_table, rel_bias, cols,
                out_buf, win_buf):
    n_pages = page_table.shape[1]
    page_size = pages.shape[1]
    assert page_size == 2 * CMP_BLOCK and tn == SUBLANES and row0 % tn == 0
    past = n_pages * page_size
    ncs = 2 * n_pages
    ncp = _round_up(ncs + 1, LANES)
    wb = cache_win_l.shape[2]
    assert wb == WINDOW and past >= WINDOW
    half_cols = 2 * NSA_KV_HEADS * HEAD_DIM
    n_slc = 2 * NSA_KV_HEADS
    rb = row0 // tn
    grid_spec = pltpu.PrefetchScalarGridSpec(
        num_scalar_prefetch=1,
        grid=(n_seq,),
        in_specs=[
            pl.BlockSpec(memory_space=pltpu.SMEM),
            pl.BlockSpec((tn, NSA_WIDTH), lambda b, pt: (rb + b, cols["q"] // NSA_WIDTH)),
            pl.BlockSpec((tn, 2 * half_cols), lambda b, pt: (rb + b, cols["kv"] // (2 * half_cols))),
            pl.BlockSpec((tn, half_cols), lambda b, pt: (rb + b, cols["win"] // half_cols)),
            pl.BlockSpec((tn, 2 * LANES), lambda b, pt: (rb + b, cols["ng"] // (2 * LANES))),
            pl.BlockSpec(memory_space=pl.ANY),
            pl.BlockSpec(memory_space=pl.ANY),
            pl.BlockSpec(memory_space=pl.ANY),
            pl.BlockSpec(memory_space=pl.ANY),
            pl.BlockSpec(memory_space=pl.ANY),
        ],
        out_specs=[pl.BlockSpec((tn, NSA_WIDTH), lambda b, pt: (rb + b, 0)),
                   pl.BlockSpec((1, wb * n_slc, HEAD_DIM), lambda b, pt: (layer * n_seq + b, 0, 0))],
        scratch_shapes=[
            pltpu.VMEM((2, n_slc, n_pages, page_size, HEAD_DIM), F32),
            pltpu.VMEM((2, n_pages) + comp_pages.shape[1:], F32),
            pltpu.VMEM((2, n_slc, wb, HEAD_DIM), F32),
            pltpu.VMEM((ncp, past + LANES), BF16),
            pltpu.SemaphoreType.DMA((2,)),
            pltpu.SemaphoreType.DMA((2,)),
            pltpu.SemaphoreType.DMA((2,)),
        ],
    )
    return pl.pallas_call(
        functools.partial(_nsa_sample_kernel, n_seq=n_seq, n_pages=n_pages, page0=page0, layer=layer, tn=tn),
        grid_spec=grid_spec,
        out_shape=[jax.ShapeDtypeStruct(out_buf.shape, out_buf.dtype),
                   jax.ShapeDtypeStruct(win_buf.shape, win_buf.dtype)],
        input_output_aliases={9: 0, 10: 1},
        compiler_params=_cparams(("arbitrary",)),
        name="nsa_sample",
    )(page_table.reshape(-1), rel_bias, z, z, z, z, cache_win_l, pages, comp_pages, out_buf, win_buf)


def _cumsum8(x):
    row = lax.broadcasted_iota(jnp.int32, x.shape, 0)
    for sh in (1, 2, 4):
        x = x + jnp.where(row >= sh, pltpu.roll(x, sh, 0), 0.0)
    return x


def _mixer_kernel(zcb_ref, zcc_ref, zch_ref, zhg_ref, cw_ref, cs_ref, hs_ref, lb_ref, hn_ref, _yb_buf, _yc_buf,
                  yb_ref, yc_ref, cso_ref, hso_ref, uprev, st, qf_s, k_s, lf_s, *, layer, tt):
    ti = pl.program_id(1)
    nt = pl.num_programs(1)

    @pl.when(ti == 0)
    def _():
        uprev[...] = jnp.zeros(uprev.shape, F32)
        uprev[SUBLANES - (CONV_K - 1):SUBLANES, :] = cs_ref[0]
        for h in range(HG_HEADS):
            st[h] = hs_ref[0, h].T

    cb = zcb_ref[...]
    u = zcc_ref[...] * zch_ref[...]
    ext = jnp.concatenate([uprev[...], u], axis=0)
    y = cw_ref[CONV_K - 1:CONV_K, :] * u
    for j in range(CONV_K - 1):
        shift = CONV_K - 1 - j
        y = y + cw_ref[j:j + 1, :] * pltpu.roll(ext, shift, 0)[SUBLANES:, :]
    yb_ref[...] = (cb * y).astype(yb_ref.dtype)
    tail = ext[tt:tt + SUBLANES, :]
    uprev[...] = tail

    @pl.when(ti == nt - 1)
    def _():
        cso_ref[0] = pltpu.roll(tail, CONV_K - 1, 0)[0:CONV_K - 1, :]

    hw = HG_HEADS * HG_DK
    lg = lb_ref[...]
    mx = jnp.max(lg, axis=0, keepdims=True)
    ex = jnp.exp(lg - mx)
    sm = ex / jnp.sum(ex, axis=0, keepdims=True)
    cs = sm[0:1]
    for i in range(1, layer + 1):
        cs = cs + sm[i:i + 1]
    lb = cs - sm[0:1]
    fx = zhg_ref[:, hw:2 * hw]
    lb_pos = lb > 0.0
    log_lb = jnp.log(jnp.where(lb_pos, lb, 1.0))
    log_sig = jnp.minimum(fx, 0.0) - jnp.log1p(jnp.exp(-jnp.abs(fx)))
    a2 = jnp.log1p(-lb) + log_sig
    lae = jnp.maximum(log_lb, a2) + jnp.log1p(jnp.exp(-jnp.abs(log_lb - a2)))
    lf_s[...] = jnp.where(lb_pos, lae, log_sig)
    k_s[...] = (1.0 - lb) * jax.nn.sigmoid(-fx)
    hq = zhg_ref[:, 0:hw]
    qf_s[...] = hq * jax.nn.sigmoid(hq)
    norm_g = hn_ref[...]
    ts = lax.broadcasted_iota(jnp.int32, (HG_BLOCK, HG_BLOCK, 1), 0)
    ss = lax.broadcasted_iota(jnp.int32, (HG_BLOCK, HG_BLOCK, 1), 1)
    causal = ss <= ts

    def block(i, c):
        r0 = pl.multiple_of(i * HG_BLOCK, HG_BLOCK)
        rs = pl.ds(r0, HG_BLOCK)
        for h in range(HG_HEADS):
            kc = slice(h * HG_DK, (h + 1) * HG_DK)
            vcol = slice(2 * hw + h * HG_DV, 2 * hw + (h + 1) * HG_DV)
            gcol = slice(2 * hw + HG_HEADS * HG_DV + h * HG_DV, 2 * hw + HG_HEADS * HG_DV + (h + 1) * HG_DV)
            qf = qf_s[rs, kc]
            kk = k_s[rs, kc]
            g = _cumsum8(lf_s[rs, kc])
            v = zhg_ref[rs, vcol]
            gl = g[HG_BLOCK - 1:HG_BLOCK, :]
            st_h = st[h]
            o = _dot_nt((qf * jnp.exp(g)).astype(BF16), st_h.astype(BF16))
            diff = jnp.where(causal, g[:, None, :] - g[None, :, :], 0.0)
            x3 = jnp.where(causal, qf[:, None, :] * kk[None, :, :] * jnp.exp(diff), 0.0)
            a = jnp.sum(x3, axis=-1, keepdims=True)
            o = o + jnp.sum(a * v[None, :, :], axis=1)
            kd = kk * jnp.exp(gl - g)
            st[h] = st_h * jnp.exp(gl) + _dot_tn(v.astype(BF16), kd.astype(BF16))
            on = o * lax.rsqrt(jnp.mean(o * o, axis=-1, keepdims=True) + NORM_EPS) * norm_g
            gx = zhg_ref[rs, gcol]
            yc_ref[rs, h * HG_DV:(h + 1) * HG_DV] = (on * (gx * jax.nn.sigmoid(gx))).astype(yc_ref.dtype)
        return c

    n_blk = tt // HG_BLOCK
    lax.fori_loop(0, n_blk, block, 0, unroll=min(4, n_blk))

    @pl.when(ti == nt - 1)
    def _():
        for h in range(HG_HEADS):
            hso_ref[0, h] = st[h].T


def _mixer(z, row0, n_seq, seq, conv_w_l, conv_state, hg_state, lb_logits, hg_norm_l, layer, cols, yb_buf, yc_buf):
    cw = conv_w_l.shape[-1]
    hgw = 2 * HG_HEADS * HG_DK + 2 * HG_HEADS * HG_DV
    tt = _pick_tile(seq, 256, SUBLANES)
    nt = seq // tt
    rb = row0 // tt
    assert row0 % tt == 0 and cols["cv"] % cw == 0 and cols["hg"] % hgw == 0
    depth = lb_logits.shape[0]
    cvb = cols["cv"] // cw
    return pl.pallas_call(
        functools.partial(_mixer_kernel, layer=layer, tt=tt),
        grid=(n_seq, nt),
        in_specs=[
            pl.BlockSpec((tt, cw), lambda b, t: (rb + b * nt + t, cvb)),
            pl.BlockSpec((tt, cw), lambda b, t: (rb + b * nt + t, cvb + 1)),
            pl.BlockSpec((tt, cw), lambda b, t: (rb + b * nt + t, cvb + 2)),
            pl.BlockSpec((tt, hgw), lambda b, t: (rb + b * nt + t, cols["hg"] // hgw)),
            pl.BlockSpec((CONV_K, cw), lambda b, t: (0, 0)),
            pl.BlockSpec((1, CONV_K - 1, cw), lambda b, t: (b, 0, 0)),
            pl.BlockSpec((1, HG_HEADS, HG_DK, HG_DV), lambda b, t: (b, 0, 0, 0)),
            pl.BlockSpec((depth, HG_HEADS * HG_DK), lambda b, t: (0, 0)),
            pl.BlockSpec((1, HG_DV), lambda b, t: (0, 0)),
            pl.BlockSpec(memory_space=pl.ANY),
            pl.BlockSpec(memory_space=pl.ANY),
        ],
        out_specs=[
            pl.BlockSpec((tt, cw), lambda b, t: (rb + b * nt + t, 0)),
            pl.BlockSpec((tt, HG_HEADS * HG_DV), lambda b, t: (rb + b * nt + t, 0)),
            pl.BlockSpec((1, CONV_K - 1, cw), lambda b, t: (b, 0, 0)),
            pl.BlockSpec((1, HG_HEADS, HG_DK, HG_DV), lambda b, t: (b, 0, 0, 0)),
        ],
        out_shape=[
            jax.ShapeDtypeStruct(yb_buf.shape, yb_buf.dtype),
            jax.ShapeDtypeStruct(yc_buf.shape, yc_buf.dtype),
            jax.ShapeDtypeStruct((n_seq, CONV_K - 1, cw), F32),
            jax.ShapeDtypeStruct((n_seq, HG_HEADS, HG_DK, HG_DV), F32),
        ],
        input_output_aliases={9: 0, 10: 1},
        scratch_shapes=[
            pltpu.VMEM((SUBLANES, cw), F32),
            pltpu.VMEM((HG_HEADS, HG_DV, HG_DK), F32),
            pltpu.VMEM((tt, HG_HEADS * HG_DK), F32),
            pltpu.VMEM((tt, HG_HEADS * HG_DK), F32),
            pltpu.VMEM((tt, HG_HEADS * HG_DK), F32),
        ],
        compiler_params=_cparams(("arbitrary", "arbitrary")),
        name="mixer_conv_hgrn",
    )(z, z, z, z, conv_w_l, conv_state, hg_state, lb_logits, hg_norm_l.reshape(1, HG_DV), yb_buf, yc_buf)


def _ep_identity(prods, extras):
    return prods[0]


def _ep_mix(prods, extras):
    ga, gb, gc = (jax.nn.sigmoid(e) for e in extras)
    return ga * prods[0] + gb * prods[1] + gc * prods[2]


def _ep_residual(prods, extras):
    return extras[0] + prods[0]


def _ep_swiglu(prods, extras):
    return prods[0] * jax.nn.sigmoid(prods[0]) * prods[1]


def _ep_ple(prods, extras):
    return extras[0] + jax.nn.sigmoid(prods[0]) * prods[1]


def kernel(x_prompt, x_sample, p_prompt, p_sample, cache_kv, cache_win, state_conv, state_hgrn, page_table,
           rel_bias, g_mix, w_in, phi_pe, phi_w1, phi_w2, conv_w, hg_lb_logits, hg_norm, w_pa, w_pb, w_pc, w_o,
           g_ffn, w_gate, w_up, w_down, w_pg, w_pp, g_final):
    bp, seq, d = x_prompt.shape
    bs, tn, _ = x_sample.shape
    depth = w_in.shape[0]
    n_in = w_in.shape[-1]
    cw = conv_w.shape[-1]
    d_ff = w_gate.shape[-1]
    ple = w_pp.shape[1]
    n_pool, page_size = cache_kv.shape[1], cache_kv.shape[2]
    mp, ms = bp * seq, bs * tn
    m = mp + ms
    kvw = 6 * NSA_KV_HEADS * HEAD_DIM
    hgw = 2 * HG_HEADS * HG_DK + 2 * HG_HEADS * HG_DV
    assert n_in == NSA_WIDTH + kvw + N_GATES + 3 * cw + hgw + 3 * d
    assert seq % Q_BLOCK == 0 and seq >= WINDOW

    o_ng = NSA_WIDTH + kvw
    cols = {"q": 0, "kv": NSA_WIDTH, "win": NSA_WIDTH + N_KV_ROWS * NSA_KV_HEADS * HEAD_DIM,
            "cv": o_ng, "hg": o_ng + 3 * cw, "mg": o_ng + 3 * cw + hgw, "ng": n_in - N_GATES}
    nz = cols["ng"] + GATE_PAD

    tm = _pick_tile(m, 1024, 256) if m % 256 == 0 else _pick_tile(m, 1024, SUBLANES)
    h = jnp.concatenate([x_prompt.reshape(mp, d), x_sample.reshape(ms, d)], axis=0)
    n_rows = N_KV_ROWS * NSA_KV_HEADS
    pages = cache_kv.reshape(depth * n_pool, page_size, n_rows, HEAD_DIM)
    pool_blocks = cache_kv.reshape(depth * n_pool * (page_size // CMP_BLOCK), CMP_BLOCK, n_rows, HEAD_DIM)
    nb_pool = n_pool * (page_size // CMP_BLOCK)
    cache_win_r = cache_win.reshape(depth, bs, cache_win.shape[2], 2 * NSA_KV_HEADS, HEAD_DIM)
    zeros_conv = jnp.zeros((bp, CONV_K - 1, cw), F32)
    zeros_hg = jnp.zeros((bp, HG_HEADS, HG_DK, HG_DV), F32)

    def w2d(w):
        return w.reshape(w.shape[0] * w.shape[1], w.shape[2])

    w_in_t = jnp.swapaxes(w_in, 1, 2)
    keep = min(WINDOW, seq)
    kv_p = jnp.zeros((depth * mp * n_rows, HEAD_DIM), F32)
    kv_s = jnp.zeros((depth * ms * n_rows, HEAD_DIM), F32)
    win_p = jnp.zeros((depth * bp * keep * (n_rows // 2), HEAD_DIM), F32)
    wb = cache_win.shape[2]
    win_s = jnp.zeros((depth * bs, wb * (n_rows // 2), HEAD_DIM), F32)
    conv_p, conv_s, hg_p, hg_s = [], [], [], []
    for l in range(depth):
        xn = _rmsnorm(h, g_mix[l], BF16)
        z = _fused_matmul([xn], [0], [(_w_in_relayout(w_in_t, l, o_ng, nz), 0)], [], _ep_identity, nz, F32,
                          tm, _pick_tile(nz, 2560, 2 * LANES), "in_proj", w_rows_out=True)

        comp_prompt = _compress(z, mp // CMP_BLOCK, cols["kv"] // HEAD_DIM, phi_pe[l], phi_w1[l], phi_w2[l],
                                _pick_tile(mp // CMP_BLOCK, 256, SUBLANES), "compress_prompt")
        cb_pool = _pick_tile(nb_pool, 64, SUBLANES)
        comp_pool = _compress_rows(pool_blocks, l * nb_pool, nb_pool, phi_pe[l], phi_w1[l], phi_w2[l],
                                   cb_pool, "compress_pool")
        comp_pages = comp_pool.reshape(n_pool, (page_size // CMP_BLOCK) * n_rows, 2 * HEAD_DIM)
        ya = _nsa_prompt(z, comp_prompt, rel_bias, bp, seq, cols, jnp.zeros((m, NSA_WIDTH), F32))
        ya, win_s = _nsa_sample(z, mp, bs, tn, pages, l * n_pool, comp_pages, cache_win_r, l, page_table,
                                rel_bias, cols, ya, win_s)
        yb, yc, cs_p, hs_p = _mixer(z, 0, bp, seq, conv_w[l], zeros_conv, zeros_hg, hg_lb_logits,
                                    hg_norm[l], l, cols, jnp.zeros((m, cw), F32),
                                    jnp.zeros((m, HG_HEADS * HG_DV), F32))
        yb, yc, cs_s, hs_s = _mixer(z, mp, bs, tn, conv_w[l], state_conv[l], state_hgrn[l], hg_lb_logits,
                                    hg_norm[l], l, cols, yb, yc)

        tn_d = _pick_tile(d, 1024, LANES)
        tn_mix = tn_d
        mgb = cols["mg"] // tn_mix
        mixed = _fused_matmul([ya, yb, yc], [0, 1, 2],
                              [(w2d(w_pa).astype(BF16), l), (w2d(w_pb).astype(BF16), l), (w2d(w_pc).astype(BF16), l)],
                              [(z, mgb), (z, mgb + d // tn_mix), (z, mgb + 2 * (d // tn_mix))],
                              _ep_mix, d, BF16, tm, tn_mix, "mix_proj")
        h = _fused_matmul([mixed], [0], [(w2d(w_o).astype(BF16), l)], [(h, 0)], _ep_residual, d, F32, tm,
                          _pick_tile(d, 2048, LANES), "out_proj", single_buffer_w=True)
        xf = _rmsnorm(h, g_ffn[l], BF16)
        act = _fused_matmul([xf], [0, 0], [(w2d(w_gate), l), (w2d(w_up), l)], [], _ep_swiglu, d_ff, BF16,
                            tm, _pick_tile(d_ff, 512, LANES), "ffn_up")
        h = _fused_matmul([act], [0], [(w_down[l].astype(BF16), 0)], [(h, 0)], _ep_residual, d, F32,
                          tm, _pick_tile(d, 1024, LANES), "ffn_down", single_buffer_w=True)
        p_all = jnp.concatenate([p_prompt[l].reshape(mp, ple), p_sample[l].reshape(ms, ple)], axis=0)
        h = _fused_matmul([h, p_all], [0, 1], [(w2d(w_pg).astype(BF16), l), (w2d(w_pp).astype(BF16), l)],
                          [(h, 0)], _ep_ple, d, F32, tm, _pick_tile(d, 2048, LANES), "ple_gate",
                          single_buffer_w=True)

        kv_p = _rows_out(z, cols["kv"], n_rows, 1, mp, lambda s: 0, kv_p, l, "kv_rows_prompt")
        kv_s = _rows_out(z, cols["kv"], n_rows, 1, ms, lambda s: mp, kv_s, l, "kv_rows_sample")
        win_p = _rows_out(z, cols["win"], n_rows // 2, bp, keep, lambda s: s * seq + seq - keep, win_p, l,
                          "win_rows_prompt")
        conv_p.append(cs_p); conv_s.append(cs_s); hg_p.append(hs_p); hg_s.append(hs_s)

    y_p, y_s = _rmsnorm_split(h, g_final, mp)
    return (y_p.reshape(bp, seq, d), y_s.reshape(bs, tn, d),
            kv_p.reshape(depth, bp, seq, N_KV_ROWS, NSA_KV_HEADS, HEAD_DIM),
            kv_s.reshape(depth, bs, tn, N_KV_ROWS, NSA_KV_HEADS, HEAD_DIM),
            win_p.reshape(depth, bp, keep, 2, NSA_KV_HEADS, HEAD_DIM),
            win_s.reshape(depth, bs, wb, 2, NSA_KV_HEADS, HEAD_DIM), jnp.stack(conv_p), jnp.stack(conv_s),
            jnp.stack(hg_p), jnp.stack(hg_s))
```
